```python
import math
import jax, jax.numpy as jnp
from jax import lax
import numpy as np

D_MODEL = 1024
BATCH = 4
SEQ = 4096
DEPTH = 1

MEM_LEN = 256
ATT_HEADS = 8
ATT_HEAD_DIM = 64
ATT_WIDTH = ATT_HEADS * ATT_HEAD_DIM
MOBA_BLOCK = 256
MOBA_TOPK = 3
QUERY_CHUNK = 32
REL_BUCKETS = 32
REL_MAX_DIST = 128
CONV_WIDTH = D_MODEL // 2
CONV_K = 3
XATT_HEADS = 4
XATT_HEAD_DIM = 128
XATT_WIDTH = XATT_HEADS * XATT_HEAD_DIM
N_BRANCH = 3
N_EXPERTS = 32
TOP_K = 4
D_EXPERT = D_MODEL
SWIGLU_LIMIT = 7.0
SWIGLU_ALPHA = 1.702
ROUTE_BLOCK = 128
EPS = 1e-5
NEG = -1e30

IN_COLS = 3 * ATT_WIDTH + 3 * CONV_WIDTH + XATT_WIDTH + N_BRANCH * D_MODEL

kernel_name = "hybrid_moba_shortconv_xattn_moe"


def rmsnorm(x, g):
    xf = x.astype(jnp.float32)
    y = xf * lax.rsqrt(jnp.mean(xf * xf, axis=-1, keepdims=True) + EPS)
    return (y * g.astype(jnp.float32)).astype(x.dtype)


def t5_bucket(dist):
    n = jnp.maximum(dist, 0)
    max_exact = REL_BUCKETS // 2
    nf = jnp.maximum(n, 1).astype(jnp.float32)
    large = max_exact + (jnp.log(nf / max_exact) / math.log(REL_MAX_DIST / max_exact)
                         * (REL_BUCKETS - max_exact)).astype(jnp.int32)
    large = jnp.minimum(large, REL_BUCKETS - 1)
    return jnp.where(n < max_exact, n, large)


def moba_attention(q, k, v, rel_table):
    B, S, H, Dh = q.shape
    L = MOBA_BLOCK
    nb = -(-S // L)
    pad = nb * L - S
    kp = jnp.pad(k, ((0, 0), (0, pad), (0, 0), (0, 0)))
    vp = jnp.pad(v, ((0, 0), (0, pad), (0, 0), (0, 0)))
    kb = kp.reshape(B, nb, L, H, Dh).transpose(0, 3, 1, 2, 4)
    vb = vp.reshape(B, nb, L, H, Dh).transpose(0, 3, 1, 2, 4)
    k_mean = jnp.mean(kb.astype(jnp.float32), axis=3)
    qh = q.transpose(0, 2, 1, 3) * (1.0 / math.sqrt(Dh))
    n_sel = min(MOBA_TOPK, nb)
    bi = jnp.arange(B)[:, None, None, None]
    hi = jnp.arange(H)[None, :, None, None]
    blk_ids = jnp.arange(nb)
    offs = jnp.arange(L)

    def chunk_fn(c):
        start = c * QUERY_CHUNK
        qc = lax.dynamic_slice_in_dim(qh, start, QUERY_CHUNK, axis=2)
        q_pos = start + jnp.arange(QUERY_CHUNK)
        blk = start // L
        gate = jnp.einsum('bhqd,bhnd->bhqn', qc.astype(jnp.float32), k_mean)
        gate = jnp.where(blk_ids < blk, gate, -jnp.inf)
        _, sel = lax.top_k(gate, n_sel)
        sel_valid = sel < blk
        ks = kb[bi, hi, sel]
        vs = vb[bi, hi, sel]
        s_sel = jnp.einsum('bhqd,bhqnld->bhqnl', qc, ks).astype(jnp.float32)
        k_pos_sel = sel[..., None] * L + offs
        bias_sel = rel_table[t5_bucket(q_pos[:, None, None] - k_pos_sel), hi[..., None]]
        s_sel = jnp.where(sel_valid[..., None], s_sel + bias_sel.astype(jnp.float32), NEG)
        ko = lax.dynamic_index_in_dim(kb, blk, axis=2, keepdims=False)
        vo = lax.dynamic_index_in_dim(vb, blk, axis=2, keepdims=False)
        s_own = jnp.einsum('bhqd,bhld->bhql', qc, ko).astype(jnp.float32)
        dist = q_pos[:, None] - (blk * L + offs)[None, :]
        bias_own = rel_table[t5_bucket(dist)].transpose(2, 0, 1)
        s_own = jnp.where(dist >= 0, s_own + bias_own.astype(jnp.float32), NEG)
        logits = jnp.concatenate([s_sel.reshape(B, H, QUERY_CHUNK, n_sel * L), s_own], axis=-1)
        p = jax.nn.softmax(logits, axis=-1).astype(v.dtype)
        p_sel = p[..., :n_sel * L].reshape(B, H, QUERY_CHUNK, n_sel, L)
        p_own = p[..., n_sel * L:]
        return (jnp.einsum('bhqnl,bhqnld->bhqd', p_sel, vs)
                + jnp.einsum('bhql,bhld->bhqd', p_own, vo))

    outs = lax.map(chunk_fn, jnp.arange(S // QUERY_CHUNK))
    return outs.transpose(1, 0, 3, 2, 4).reshape(B, S, H * Dh)


def short_conv(b_gate, c_gate, u, conv_w):
    z = c_gate * u
    y = lax.conv_general_dilated(z, conv_w[:, None, :].astype(z.dtype), window_strides=(1,),
                                 padding=[(CONV_K - 1, 0)],
                                 dimension_numbers=('NWC', 'WIO', 'NWC'),
                                 feature_group_count=CONV_WIDTH)
    return b_gate * y


def memory_cross_attention(q, mem_n, w_mem_kv):
    B, S = q.shape[:2]
    kv = mem_n @ w_mem_kv
    k = kv[..., :XATT_WIDTH].reshape(B, -1, XATT_HEADS, XATT_HEAD_DIM)
    v = kv[..., XATT_WIDTH:].reshape(B, -1, XATT_HEADS, XATT_HEAD_DIM)
    s = jnp.einsum('bshd,bmhd->bhsm', q, k).astype(jnp.float32) * (1.0 / math.sqrt(XATT_HEAD_DIM))
    p = jax.nn.softmax(s, axis=-1).astype(v.dtype)
    return jnp.einsum('bhsm,bmhd->bshd', p, v).reshape(B, S, XATT_WIDTH)


def moe_ffn(h, w_router, b_router, w_gu, b_gu, w_down, b_down):
    B, S, D = h.shape
    T = B * S
    hf = h.reshape(T, D)
    logits = (hf @ w_router).astype(jnp.float32) + b_router.astype(jnp.float32)
    top_logit, top_idx = lax.top_k(logits, TOP_K)
    top_w = jax.nn.softmax(top_logit, axis=-1)
    flat_e = top_idx.reshape(-1)
    flat_tok = jnp.arange(T * TOP_K, dtype=jnp.int32) // TOP_K
    flat_w = top_w.reshape(-1)
    order = jnp.argsort(flat_e)
    sorted_e = flat_e[order]
    sorted_tok = flat_tok[order]
    sorted_w = flat_w[order]
    counts = jnp.bincount(flat_e, length=N_EXPERTS)
    starts = jnp.cumsum(counts) - counts
    padded = (counts + ROUTE_BLOCK - 1) // ROUTE_BLOCK * ROUTE_BLOCK
    pad_ends = jnp.cumsum(padded)
    pad_starts = pad_ends - padded
    dest = pad_starts[sorted_e] + (jnp.arange(T * TOP_K) - starts[sorted_e])
    n_rows = T * TOP_K + N_EXPERTS * ROUTE_BLOCK
    row_tok = jnp.full((n_rows,), T, jnp.int32).at[dest].set(sorted_tok)
    row_w = jnp.zeros((n_rows,), jnp.float32).at[dest].set(sorted_w)
    n_blocks = n_rows // ROUTE_BLOCK
    block_expert = jnp.minimum(
        jnp.searchsorted(pad_ends, jnp.arange(n_blocks) * ROUTE_BLOCK, side='right'),
        N_EXPERTS - 1)
    hpad = jnp.concatenate([hf, jnp.zeros((1, D), hf.dtype)], axis=0)
    xs = hpad[row_tok].reshape(n_blocks, ROUTE_BLOCK, D)

    def expert_block(args):
        xb, e = args
        gu = xb @ w_gu[e] + b_gu[e]
        g = jnp.minimum(gu[:, :D_EXPERT], SWIGLU_LIMIT)
        lin = jnp.clip(gu[:, D_EXPERT:], -SWIGLU_LIMIT, SWIGLU_LIMIT)
        glu = g * jax.nn.sigmoid(SWIGLU_ALPHA * g)
        return ((lin + 1.0) * glu) @ w_down[e] + b_down[e]

    ys = lax.map(expert_block, (xs, block_expert)).reshape(n_rows, D)
    ys = ys * row_w[:, None].astype(ys.dtype)
    out = jax.ops.segment_sum(ys, row_tok, num_segments=T + 1)[:T]
    return out.reshape(B, S, D)


def setup_inputs(seed: int = 0) -> dict:
    key = jax.random.key(seed)
    ks = jax.random.split(key, 24)
    f32 = jnp.float32
    L = DEPTH

    def nrm(k, shape, fan_in):
        return jax.random.normal(k, shape, f32) * (fan_in ** -0.5)

    def gain(k, shape):
        return 1.0 + 0.05 * jax.random.normal(k, shape, f32)

    return {
        "x": jax.random.normal(ks[0], (BATCH, SEQ, D_MODEL), f32),
        "mem": jax.random.normal(ks[1], (BATCH, MEM_LEN, D_MODEL), f32),
        "rel_bias": 0.3 * jax.random.normal(ks[2], (REL_BUCKETS, ATT_HEADS), f32),
        "norm_mix_g": gain(ks[3], (L, D_MODEL)),
        "w_in": nrm(ks[4], (L, D_MODEL, IN_COLS), D_MODEL),
        "b_gate": 0.1 * jax.random.normal(ks[5], (L, N_BRANCH, D_MODEL), f32),
        "conv_w": nrm(ks[6], (L, CONV_K, CONV_WIDTH), CONV_K),
        "norm_mem_g": gain(ks[7], (L, D_MODEL)),
        "w_mem_kv": nrm(ks[8], (L, D_MODEL, 2 * XATT_WIDTH), D_MODEL),
        "w_br_att": nrm(ks[9], (L, ATT_WIDTH, D_MODEL), ATT_WIDTH),
        "w_br_conv": nrm(ks[10], (L, CONV_WIDTH, D_MODEL), CONV_WIDTH),
        "w_br_xatt": nrm(ks[11], (L, XATT_WIDTH, D_MODEL), XATT_WIDTH),
        "w_out": nrm(ks[12], (L, D_MODEL, D_MODEL), D_MODEL),
        "norm_ffn_g": gain(ks[13], (L, D_MODEL)),
        "w_router": nrm(ks[14], (L, D_MODEL, N_EXPERTS), D_MODEL),
        "b_router": 0.01 * jax.random.normal(ks[15], (L, N_EXPERTS), f32),
        "w_gu": nrm(ks[16], (L, N_EXPERTS, D_MODEL, 2 * D_EXPERT), D_MODEL),
        "b_gu": 0.01 * jax.random.normal(ks[17], (L, N_EXPERTS, 2 * D_EXPERT), f32),
        "w_down": nrm(ks[18], (L, N_EXPERTS, D_EXPERT, D_MODEL), D_EXPERT),
        "b_down": 0.01 * jax.random.normal(ks[19], (L, N_EXPERTS, D_MODEL), f32),
        "norm_final_g": gain(ks[20], (D_MODEL,)),
    }


def reference(x, mem, rel_bias, norm_mix_g, w_in, b_gate, conv_w, norm_mem_g, w_mem_kv,
              w_br_att, w_br_conv, w_br_xatt, w_out, norm_ffn_g, w_router, b_router,
              w_gu, b_gu, w_down, b_down, norm_final_g):
    B, S, D = x.shape
    split_at = np.cumsum([ATT_WIDTH, ATT_WIDTH, ATT_WIDTH, CONV_WIDTH, CONV_WIDTH, CONV_WIDTH,
                          XATT_WIDTH]).tolist()
    for l in range(DEPTH):
        h = rmsnorm(x, norm_mix_g[l])
        proj = h @ w_in[l]
        qa, ka, va, cb, cc, cu, qx, glog = jnp.split(proj, split_at, axis=-1)
        shp_a = (B, S, ATT_HEADS, ATT_HEAD_DIM)
        y_att = moba_attention(qa.reshape(shp_a), ka.reshape(shp_a), va.reshape(shp_a), rel_bias)
        y_conv = short_conv(cb, cc, cu, conv_w[l])
        mem_n = rmsnorm(mem, norm_mem_g[l])
        y_xatt = memory_cross_attention(qx.reshape(B, S, XATT_HEADS, XATT_HEAD_DIM), mem_n, w_mem_kv[l])
        gates = jax.nn.sigmoid(glog.reshape(B, S, N_BRANCH, D) + b_gate[l])
        merged = (gates[:, :, 0] * (y_att @ w_br_att[l])
                  + gates[:, :, 1] * (y_conv @ w_br_conv[l])
                  + gates[:, :, 2] * (y_xatt @ w_br_xatt[l]))
        x = x + merged @ w_out[l]
        h2 = rmsnorm(x, norm_ffn_g[l])
        x = x + moe_ffn(h2, w_router[l], b_router[l], w_gu[l], b_gu[l], w_down[l], b_down[l])
    return rmsnorm(x, norm_final_g)
```

```python
import functools
import math

import jax
import jax.numpy as jnp
import numpy as np
from jax import lax
from jax.experimental import pallas as pl
from jax.experimental.pallas import tpu as pltpu

F32 = jnp.float32
BF16 = jnp.bfloat16
HIGHEST = lax.Precision.HIGHEST

ATT_HEADS = 8
ATT_HEAD_DIM = 64
ATT_WIDTH = ATT_HEADS * ATT_HEAD_DIM
MOBA_BLOCK = 256
MOBA_TOPK = 3
REL_BUCKETS = 32
REL_MAX_DIST = 128
CONV_K = 3
XATT_HEADS = 4
XATT_HEAD_DIM = 128
N_BRANCH = 3
N_EXPERTS = 32
TOP_K = 4
SWIGLU_LIMIT = 7.0
SWIGLU_ALPHA = 1.702
EPS = 1e-5
NEG = -1e30

LANES = 128
SEL_SLOTS = 16
TOKEN_TILE = 256
EXPERT_ROWS = 256
VMEM_LIMIT = 56 * 1024 * 1024


def _rms(x, g):
    return x * lax.rsqrt(jnp.mean(x * x, axis=-1, keepdims=True) + EPS) * g


def _dot(a, b):
    return jnp.dot(a, b, preferred_element_type=F32)


def _dot_nt(a, b, precision=None):
    return lax.dot_general(a, b, (((1,), (1,)), ((), ())), precision=precision,
                           preferred_element_type=F32)


def _sigmoid(x):
    return 1.0 / (1.0 + jnp.exp(-x))


def _mem_kv_kernel(mem_ref, g_ref, wkT_ref, wv_ref, mkT_ref, mv_ref):
    mn = _rms(mem_ref[0], g_ref[...]).astype(BF16)
    mkT_ref[0] = _dot_nt(wkT_ref[...], mn).astype(BF16)
    mv_ref[0] = _dot(mn, wv_ref[...]).astype(BF16)


def _mem_kv(mem, g, wkT, wv):
    B, M, D = mem.shape
    XW = wv.shape[1]
    return pl.pallas_call(
        _mem_kv_kernel,
        grid=(B,),
        in_specs=[pl.BlockSpec((1, M, D), lambda b: (b, 0, 0)),
                  pl.BlockSpec((1, D), lambda b: (0, 0)),
                  pl.BlockSpec((XW, D), lambda b: (0, 0)),
                  pl.BlockSpec((D, XW), lambda b: (0, 0))],
        out_specs=[pl.BlockSpec((1, XW, M), lambda b: (b, 0, 0)),
                   pl.BlockSpec((1, M, XW), lambda b: (b, 0, 0))],
        out_shape=[jax.ShapeDtypeStruct((B, XW, M), BF16),
                   jax.ShapeDtypeStruct((B, M, XW), BF16)],
        name="mem_kv",
    )(mem, g, wkT, wv)


def _qkv_select_kernel(x_ref, g_ref, wqkv_ref, wkT_ref, q_ref, kT_ref, ve_ref, vo_ref, selb_ref,
                       km_ref):
    i = pl.program_id(1)
    tq = x_ref.shape[1]
    W = ATT_WIDTH

    @pl.when(i == 0)
    def _():
        km_ref[...] = jnp.zeros_like(km_ref)

    h = _rms(x_ref[0], g_ref[...]).astype(BF16)
    qkv = _dot(h, wqkv_ref[...])
    q = qkv[:, :W] * (1.0 / math.sqrt(ATT_HEAD_DIM))
    k = qkv[:, W:2 * W]
    q_ref[0] = q.astype(BF16)
    v = qkv[:, 2 * W:]
    even_head = (lax.broadcasted_iota(jnp.int32, (1, W), 1) // ATT_HEAD_DIM) % 2 == 0
    ve_ref[0] = jnp.where(even_head, v, 1.0).astype(BF16)
    vo_ref[0] = jnp.where(even_head, 1.0, v).astype(BF16)
    kT_ref[0] = _dot_nt(wkT_ref[...], h).astype(BF16)

    gate = _dot_nt(km_ref[...], q, precision=HIGHEST)
    g3 = gate.reshape(ATT_HEADS, SEL_SLOTS, tq)
    jj = lax.broadcasted_iota(jnp.int32, g3.shape, 1).astype(F32)
    valid = jj < i.astype(F32)
    g3 = jnp.where(valid, g3, -jnp.inf)
    sel = jnp.zeros(g3.shape, F32)
    for _ in range(MOBA_TOPK):
        m = jnp.max(g3, axis=1, keepdims=True)
        first = jnp.min(jnp.where(g3 == m, jj, float(SEL_SLOTS)), axis=1, keepdims=True)
        pick = jj == first
        sel = jnp.where(pick, 1.0, sel)
        g3 = jnp.where(pick, -jnp.inf, g3)
    keep = ((sel > 0.5) & valid) | (jj == i.astype(F32))
    sb = jnp.where(keep, 0.0, NEG).reshape(ATT_HEADS * SEL_SLOTS, tq)
    selb_ref[0] = sb.T.astype(BF16)

    kmean = jnp.sum(k, axis=0, keepdims=True) * (1.0 / tq)
    lane_head = lax.broadcasted_iota(jnp.int32, (1, W), 1) // ATT_HEAD_DIM
    for hh in range(ATT_HEADS):
        km_ref[pl.ds(hh * SEL_SLOTS + i, 1), :] = jnp.where(lane_head == hh, kmean, 0.0)


def _qkv_select(x, g, wqkv, wkT):
    B, S, D = x.shape
    W = ATT_WIDTH
    tq = MOBA_BLOCK
    nb = S // tq
    assert nb <= SEL_SLOTS and ATT_HEADS * SEL_SLOTS == LANES
    return pl.pallas_call(
        _qkv_select_kernel,
        grid=(B, nb),
        in_specs=[pl.BlockSpec((1, tq, D), lambda b, i: (b, i, 0)),
                  pl.BlockSpec((1, D), lambda b, i: (0, 0)),
                  pl.BlockSpec((D, 3 * W), lambda b, i: (0, 0)),
                  pl.BlockSpec((W, D), lambda b, i: (0, 0))],
        out_specs=[pl.BlockSpec((1, tq, W), lambda b, i: (b, i, 0)),
                   pl.BlockSpec((1, W, tq), lambda b, i: (b, 0, i)),
                   pl.BlockSpec((1, tq, W), lambda b, i: (b, i, 0)),
                   pl.BlockSpec((1, tq, W), lambda b, i: (b, i, 0)),
                   pl.BlockSpec((1, tq, LANES), lambda b, i: (b, i, 0))],
        out_shape=[jax.ShapeDtypeStruct((B, S, W), BF16),
                   jax.ShapeDtypeStruct((B, W, S), BF16),
                   jax.ShapeDtypeStruct((B, S, W), BF16),
                   jax.ShapeDtypeStruct((B, S, W), BF16),
                   jax.ShapeDtypeStruct((B, S, LANES), BF16)],
        scratch_shapes=[pltpu.VMEM((LANES, W), F32)],
        compiler_params=pltpu.CompilerParams(
            dimension_semantics=("arbitrary", "arbitrary"), vmem_limit_bytes=VMEM_LIMIT),
        name="qkv_select",
    )(x, g, wqkv, wkT)


def _moba_kernel(q_ref, selb_ref, kT_ref, ve_ref, vo_ref, tb_ref, e_ref, o_ref):
    i = pl.program_id(1)
    tq = q_ref.shape[1]
    L = MOBA_BLOCK
    selb = selb_ref[0].astype(F32)
    lane = lax.broadcasted_iota(jnp.int32, (tq, LANES), 1)
    v_refs = (ve_ref, vo_ref)

    for p in range(ATT_HEADS // 2):
        cols = slice(p * LANES, (p + 1) * LANES)
        qp = q_ref[0, :, cols].astype(F32)
        in_half = [lane < ATT_HEAD_DIM, lane >= ATT_HEAD_DIM]
        lhs = [jnp.concatenate(
            [jnp.where(in_half[hh], qp, 0.0).astype(BF16),
             jnp.where(lane // SEL_SLOTS == 2 * p + hh, selb, 0.0).astype(BF16)], axis=1)
            for hh in (0, 1)]

        def body(j, carry):
            koff = pl.multiple_of(j * L, L)
            rhs = jnp.concatenate([kT_ref[0, cols, pl.ds(koff, L)], e_ref[j]], axis=0)
            t = jnp.minimum(i - j, 2)
            out = []
            for hh in (0, 1):
                m, acc = carry[2 * hh], carry[2 * hh + 1]
                s = _dot(lhs[hh], rhs) + tb_ref[t, 2 * p + hh]
                m_new = jnp.maximum(m, jnp.max(s, axis=1, keepdims=True))
                alpha = jnp.exp(m - m_new)
                pr = jnp.exp(s - m_new).astype(BF16)
                acc = alpha * acc + _dot(pr, v_refs[hh][0, pl.ds(koff, L), cols])
                out += [m_new, acc]
            return tuple(out)

        init = (jnp.full((tq, 1), -jnp.inf, F32), jnp.zeros((tq, LANES), F32)) * 2
        _, a0, _, a1 = lax.fori_loop(0, i + 1, body, init)
        o0 = a0 / pltpu.roll(a0, ATT_HEAD_DIM, 1)
        o1 = a1 / pltpu.roll(a1, ATT_HEAD_DIM, 1)
        o_ref[0, :, cols] = jnp.where(in_half[0], o0, o1).astype(BF16)


def _moba(q, selb, kT, v_even, v_odd, tb):
    B, S, W = q.shape
    tq = MOBA_BLOCK
    nb = S // tq
    r = np.arange(LANES)[None, :, None] % SEL_SLOTS
    e_all = jnp.asarray(np.broadcast_to(r == np.arange(SEL_SLOTS)[:, None, None],
                                        (SEL_SLOTS, LANES, MOBA_BLOCK)), BF16)
    return pl.pallas_call(
        _moba_kernel,
        grid=(B, nb),
        in_specs=[pl.BlockSpec((1, tq, W), lambda b, i: (b, i, 0)),
                  pl.BlockSpec((1, tq, LANES), lambda b, i: (b, i, 0)),
                  pl.BlockSpec((1, W, S), lambda b, i: (b, 0, 0)),
                  pl.BlockSpec((1, S, W), lambda b, i: (b, 0, 0)),
                  pl.BlockSpec((1, S, W), lambda b, i: (b, 0, 0)),
                  pl.BlockSpec(tb.shape, lambda b, i: (0, 0, 0, 0)),
                  pl.BlockSpec(e_all.shape, lambda b, i: (0, 0, 0))],
        out_specs=pl.BlockSpec((1, tq, W), lambda b, i: (b, i, 0)),
        out_shape=jax.ShapeDtypeStruct((B, S, W), BF16),
        compiler_params=pltpu.CompilerParams(
            dimension_semantics=("arbitrary", "arbitrary"), vmem_limit_bytes=VMEM_LIMIT),
        name="moba",
    )(q, selb, kT, v_even, v_odd, tb, e_all)


def _merge_kernel(tiles_per_seq,
                  x_ref, ya_ref, mkT_ref, mv_ref, g1_ref, wr_ref, cw_ref, bg_ref,
                  wba_ref, wbc_ref, wbx_ref, wo_ref, g2_ref, wrt_ref, br_ref,
                  x1_ref, h2_ref, idx_ref, rank_ref, wt_ref, cnt_ref, zprev_ref):
    i = pl.program_id(0)
    tm, D = x_ref.shape
    CW = cw_ref.shape[1]
    XW = mv_ref.shape[2]

    @pl.when(i == 0)
    def _():
        cnt_ref[...] = jnp.zeros_like(cnt_ref)

    @pl.when(i % tiles_per_seq == 0)
    def _():
        zprev_ref[...] = jnp.zeros_like(zprev_ref)

    x = x_ref[...]
    h = _rms(x, g1_ref[...]).astype(BF16)
    pr = _dot(h, wr_ref[...])

    cb = pr[:, :CW]
    z = pr[:, CW:2 * CW] * pr[:, 2 * CW:3 * CW]
    row = lax.broadcasted_iota(jnp.int32, (tm, CW), 0)
    zp = zprev_ref[...]
    z1 = jnp.where(row == 0, zp[7:8], pltpu.roll(z, 1, 0))
    z2 = jnp.where(row == 0, zp[6:7], jnp.where(row == 1, zp[7:8], pltpu.roll(z, 2, 0)))
    zprev_ref[...] = z[tm - 8:]
    cw = cw_ref[...]
    y_conv = cb * (cw[0:1] * z2 + cw[1:2] * z1 + cw[2:3] * z)

    o0 = 3 * CW
    scale = 1.0 / math.sqrt(XATT_HEAD_DIM)
    ys = []
    for hh in range(XATT_HEADS):
        hs = slice(hh * XATT_HEAD_DIM, (hh + 1) * XATT_HEAD_DIM)
        qx = pr[:, o0 + hh * XATT_HEAD_DIM:o0 + (hh + 1) * XATT_HEAD_DIM].astype(BF16)
        s = _dot(qx, mkT_ref[0, hs, :]) * scale
        e = jnp.exp(s - jnp.max(s, axis=1, keepdims=True))
        l = jnp.sum(e, axis=1, keepdims=True)
        ys.append(_dot(e.astype(BF16), mv_ref[0, :, hs]) / l)
    y_x = jnp.concatenate(ys, axis=1)

    o1 = o0 + XW
    bg = bg_ref[...]
    merged = (_sigmoid(pr[:, o1:o1 + D] + bg[0:1]) * _dot(ya_ref[...], wba_ref[...])
              + _sigmoid(pr[:, o1 + D:o1 + 2 * D] + bg[1:2]) * _dot(y_conv.astype(BF16), wbc_ref[...])
              + _sigmoid(pr[:, o1 + 2 * D:o1 + 3 * D] + bg[2:3]) * _dot(y_x.astype(BF16), wbx_ref[...]))
    x1 = x + _dot(merged.astype(BF16), wo_ref[...])
    x1_ref[...] = x1
    h2 = _rms(x1, g2_ref[...])
    h2_ref[...] = h2

    lg = _dot_nt(wrt_ref[...], h2, precision=HIGHEST) + br_ref[...]
    E = lg.shape[0]
    ee = lax.broadcasted_iota(jnp.int32, (E, tm), 0).astype(F32)
    work = lg
    member = jnp.zeros((E, tm), F32)
    picks, vals = [], []
    for _ in range(TOP_K):
        m = jnp.max(work, axis=0, keepdims=True)
        first = jnp.min(jnp.where(work == m, ee, float(E)), axis=0, keepdims=True)
        pick = ee == first
        work = jnp.where(pick, -jnp.inf, work)
        member = jnp.where(pick, 1.0, member)
        picks.append((pick, first))
        vals.append(m)
    exps = [jnp.exp(v - vals[0]) for v in vals]
    denom = exps[0] + exps[1] + exps[2] + exps[3]

    r_i = lax.broadcasted_iota(jnp.int32, (tm, tm), 0)
    c_i = lax.broadcasted_iota(jnp.int32, (tm, tm), 1)
    upper = jnp.where(r_i < c_i, 1.0, 0.0).astype(BF16)
    before = _dot(member.astype(BF16), upper) + cnt_ref[:, 0:1]
    cnt_ref[...] = cnt_ref[...] + jnp.sum(member, axis=1, keepdims=True)

    wrow = lax.broadcasted_iota(jnp.int32, (LANES, tm), 0)
    wpad = jnp.zeros((LANES, tm), F32)
    for kk in range(TOP_K):
        pick, first = picks[kk]
        idx_ref[kk:kk + 1, :] = first.astype(jnp.int32)
        rank_ref[kk:kk + 1, :] = jnp.sum(jnp.where(pick, before, 0.0), axis=0,
                                         keepdims=True).astype(jnp.int32)
        wpad = jnp.where(wrow == kk, exps[kk] / denom, wpad)
    wt_ref[...] = wpad.T


def _merge(x2, yatt2, mkT, mv, g1, w_rest, conv_w, b_gate, wba, wbc, wbx, wo, g2, wrt, br, S):
    T, D = x2.shape
    tm = TOKEN_TILE
    nt = T // tm
    tps = S // tm
    CW = conv_w.shape[1]
    XW, M = mkT.shape[1], mkT.shape[2]
    E = wrt.shape[0]
    const = lambda shape: pl.BlockSpec(shape, lambda i: (0,) * len(shape))
    return pl.pallas_call(
        functools.partial(_merge_kernel, tps),
        grid=(nt,),
        in_specs=[pl.BlockSpec((tm, D), lambda i: (i, 0)),
                  pl.BlockSpec((tm, yatt2.shape[1]), lambda i: (i, 0)),
                  pl.BlockSpec((1, XW, M), lambda i: (i // tps, 0, 0)),
                  pl.BlockSpec((1, M, XW), lambda i: (i // tps, 0, 0)),
                  const((1, D)), const(w_rest.shape), const(conv_w.shape), const(b_gate.shape),
                  const(wba.shape), const(wbc.shape), const(wbx.shape), const(wo.shape),
                  const((1, D)), const(wrt.shape), const(br.shape)],
        out_specs=[pl.BlockSpec((tm, D), lambda i: (i, 0)),
                   pl.BlockSpec((tm, D), lambda i: (i, 0)),
                   pl.BlockSpec((TOP_K, tm), lambda i: (0, i)),
                   pl.BlockSpec((TOP_K, tm), lambda i: (0, i)),
                   pl.BlockSpec((tm, LANES), lambda i: (i, 0)),
                   pl.BlockSpec((E, LANES), lambda i: (0, 0))],
        out_shape=[jax.ShapeDtypeStruct((T, D), F32),
                   jax.ShapeDtypeStruct((T, D), F32),
                   jax.ShapeDtypeStruct((TOP_K, T), jnp.int32),
                   jax.ShapeDtypeStruct((TOP_K, T), jnp.int32),
                   jax.ShapeDtypeStruct((T, LANES), F32),
                   jax.ShapeDtypeStruct((E, LANES), F32)],
        scratch_shapes=[pltpu.VMEM((8, CW), F32)],
        compiler_params=pltpu.CompilerParams(
            dimension_semantics=("arbitrary",), vmem_limit_bytes=VMEM_LIMIT),
        name="merge_route",
    )(x2, yatt2, mkT, mv, g1, w_rest, conv_w, b_gate, wba, wbc, wbx, wo, g2, wrt, br)


def _dispatch_kernel(T, dest_ref, h2_ref, xs_in_ref, xs_ref, sem):
    del xs_in_ref
    i = pl.program_id(0)
    tm = h2_ref.shape[0]

    def row_copy(t, d):
        return pltpu.make_async_copy(h2_ref.at[pl.ds(t, 1)], xs_ref.at[pl.ds(d, 1)], sem)

    def issue(t, c):
        for kk in range(TOP_K):
            row_copy(t, dest_ref[kk * T + i * tm + t]).start()
        return c

    lax.fori_loop(0, tm, issue, 0)

    def drain(t, c):
        for kk in range(TOP_K):
            row_copy(t, dest_ref[kk * T + i * tm + t]).wait()
        return c

    lax.fori_loop(0, tm, drain, 0)


def _dispatch(dest_flat, h2, xs_init):
    T, D = h2.shape
    tm = TOKEN_TILE
    return pl.pallas_call(
        functools.partial(_dispatch_kernel, T),
        grid_spec=pltpu.PrefetchScalarGridSpec(
            num_scalar_prefetch=1,
            grid=(T // tm,),
            in_specs=[pl.BlockSpec((tm, D), lambda i, d: (i, 0)),
                      pl.BlockSpec(memory_space=pl.ANY)],
            out_specs=pl.BlockSpec(memory_space=pl.ANY),
            scratch_shapes=[pltpu.SemaphoreType.DMA(())]),
        out_shape=jax.ShapeDtypeStruct(xs_init.shape, xs_init.dtype),
        input_output_aliases={2: 0},
        compiler_params=pltpu.CompilerParams(dimension_semantics=("arbitrary",)),
        name="dispatch",
    )(dest_flat, h2, xs_init)


def _expert_kernel(be_ref, nu_ref, xs_ref, wgu_ref, bgu_ref, wd_ref, bd_ref, ys_ref):
    b = pl.program_id(0)
    F = wd_ref.shape[1]

    @pl.when(b < nu_ref[0])
    def _():
        gu = _dot(xs_ref[...].astype(BF16), wgu_ref[0]) + bgu_ref[0]
        g = jnp.minimum(gu[:, :F], SWIGLU_LIMIT)
        lin = jnp.clip(gu[:, F:], -SWIGLU_LIMIT, SWIGLU_LIMIT)
        act = (lin + 1.0) * (g * _sigmoid(SWIGLU_ALPHA * g))
        ys_ref[...] = _dot(act.astype(BF16), wd_ref[0]) + bd_ref[0]

    @pl.when(b >= nu_ref[0])
    def _():
        ys_ref[...] = jnp.zeros_like(ys_ref)


def _experts(block_expert, n_used, xs, wgu, bgu, wd, bd):
    NR, D = xs.shape
    bm = EXPERT_ROWS
    E, _, F2 = wgu.shape
    F = wd.shape[1]
    row_blk = lambda b, be, nu: (jnp.minimum(b, nu[0] - 1), 0)
    per_e = lambda b, be, nu: (be[b], 0, 0)
    return pl.pallas_call(
        _expert_kernel,
        grid_spec=pltpu.PrefetchScalarGridSpec(
            num_scalar_prefetch=2,
            grid=(NR // bm,),
            in_specs=[pl.BlockSpec((bm, D), row_blk),
                      pl.BlockSpec((1, D, F2), per_e),
                      pl.BlockSpec((1, 1, F2), per_e),
                      pl.BlockSpec((1, F, D), per_e),
                      pl.BlockSpec((1, 1, D), per_e)],
            out_specs=pl.BlockSpec((bm, D), lambda b, be, nu: (b, 0))),
        out_shape=jax.ShapeDtypeStruct((NR, D), F32),
        compiler_params=pltpu.CompilerParams(
            dimension_semantics=("arbitrary",), vmem_limit_bytes=VMEM_LIMIT),
        name="experts",
    )(block_expert, n_used, xs, wgu, bgu, wd, bd)


def _combine_kernel(T, dest_ref, x1_ref, wt_ref, g_ref, ys_ref, o_ref, buf_ref, sem):
    i = pl.program_id(0)
    n = pl.num_programs(0)
    tm = x1_ref.shape[0]

    def row_copy(step, slot, kk, t):
        d = dest_ref[kk * T + step * tm + t]
        return pltpu.make_async_copy(ys_ref.at[pl.ds(d, 1)], buf_ref.at[slot, kk, pl.ds(t, 1)],
                                     sem.at[slot])

    def issue(step, slot):
        def f(t, c):
            for kk in range(TOP_K):
                row_copy(step, slot, kk, t).start()
            return c
        lax.fori_loop(0, tm, f, 0)

    @pl.when(i == 0)
    def _():
        issue(0, 0)

    @pl.when(i + 1 < n)
    def _():
        issue(i + 1, (i + 1) % 2)

    slot = i % 2

    def drain(t, c):
        for kk in range(TOP_K):
            row_copy(i, slot, kk, t).wait()
        return c

    lax.fori_loop(0, tm, drain, 0)

    wt = wt_ref[...]
    y = x1_ref[...]
    for kk in range(TOP_K):
        y = y + wt[:, kk:kk + 1] * buf_ref[slot, kk]
    o_ref[...] = _rms(y, g_ref[...])


def _combine(dest_flat, x1, wt, g, ys):
    T, D = x1.shape
    tm = TOKEN_TILE
    return pl.pallas_call(
        functools.partial(_combine_kernel, T),
        grid_spec=pltpu.PrefetchScalarGridSpec(
            num_scalar_prefetch=1,
            grid=(T // tm,),
            in_specs=[pl.BlockSpec((tm, D), lambda i, d: (i, 0)),
                      pl.BlockSpec((tm, LANES), lambda i, d: (i, 0)),
                      pl.BlockSpec((1, D), lambda i, d: (0, 0)),
                      pl.BlockSpec(memory_space=pl.ANY)],
            out_specs=pl.BlockSpec((tm, D), lambda i, d: (i, 0)),
            scratch_shapes=[pltpu.VMEM((2, TOP_K, tm, D), F32),
                            pltpu.SemaphoreType.DMA((2,))]),
        out_shape=jax.ShapeDtypeStruct((T, D), F32),
        compiler_params=pltpu.CompilerParams(
            dimension_semantics=("arbitrary",), vmem_limit_bytes=VMEM_LIMIT),
        name="combine",
    )(dest_flat, x1, wt, g, ys)


def _t5_bucket(dist):
    n = jnp.maximum(dist, 0)
    max_exact = REL_BUCKETS // 2
    nf = jnp.maximum(n, 1).astype(F32)
    large = max_exact + (jnp.log(nf / max_exact) / math.log(REL_MAX_DIST / max_exact)
                         * (REL_BUCKETS - max_exact)).astype(jnp.int32)
    large = jnp.minimum(large, REL_BUCKETS - 1)
    return jnp.where(n < max_exact, n, large)


def _bias_tiles(rel_bias):
    L = MOBA_BLOCK
    assert REL_MAX_DIST <= L
    r = jnp.arange(L)[:, None]
    c = jnp.arange(L)[None, :]
    dist = jnp.arange(3)[:, None, None] * L + (r - c)[None]
    tiles = jnp.transpose(rel_bias[_t5_bucket(dist)], (0, 3, 1, 2)).astype(F32)
    return jnp.where((dist >= 0)[:, None], tiles, NEG)


def kernel(x, mem, rel_bias, norm_mix_g, w_in, b_gate, conv_w, norm_mem_g, w_mem_kv, w_br_att,
           w_br_conv, w_br_xatt, w_out, norm_ffn_g, w_router, b_router, w_gu, b_gu, w_down,
           b_down, norm_final_g):
    B, S, D = x.shape
    T = B * S
    depth = w_in.shape[0]
    assert depth == 1, "the combine step applies the final norm: single-layer configuration only"
    W = ATT_WIDTH
    CW = conv_w.shape[2]
    XW = XATT_HEADS * XATT_HEAD_DIM
    E = w_router.shape[2]
    bm = EXPERT_ROWS
    tb = _bias_tiles(rel_bias)

    xc = x
    for l in range(depth):
        w_l = w_in[l]
        wqkv = w_l[:, :3 * W].astype(BF16)
        wkT = w_l[:, W:2 * W].T.astype(BF16)
        w_rest = w_l[:, 3 * W:].astype(BF16)
        wm = w_mem_kv[l]

        mkT, mv = _mem_kv(mem, norm_mem_g[l][None], wm[:, :XW].T.astype(BF16),
                          wm[:, XW:].astype(BF16))
        q, kT, v_even, v_odd, selb = _qkv_select(xc, norm_mix_g[l][None], wqkv, wkT)
        y_att = _moba(q, selb, kT, v_even, v_odd, tb)

        x1, h2, idx, rank, wt, cnt = _merge(
            xc.reshape(T, D), y_att.reshape(T, W), mkT, mv, norm_mix_g[l][None], w_rest,
            conv_w[l], b_gate[l], w_br_att[l].astype(BF16), w_br_conv[l].astype(BF16),
            w_br_xatt[l].astype(BF16), w_out[l].astype(BF16), norm_ffn_g[l][None],
            w_router[l].T, b_router[l][:, None], S)

        counts = cnt[:, 0].astype(jnp.int32)
        padded = (counts + bm - 1) // bm * bm
        pad_ends = jnp.cumsum(padded)
        pad_starts = pad_ends - padded
        onehot = idx[..., None] == jnp.arange(E, dtype=jnp.int32)
        dest = (jnp.sum(jnp.where(onehot, pad_starts, 0), axis=-1) + rank).reshape(-1)
        n_rows = T * TOP_K + E * bm
        n_blocks = n_rows // bm
        n_used = (pad_ends[-1] // bm).astype(jnp.int32)
        blk = jnp.minimum(jnp.arange(n_blocks, dtype=jnp.int32), n_used - 1) * bm
        block_expert = jnp.minimum(jnp.searchsorted(pad_ends, blk, side='right'),
                                   E - 1).astype(jnp.int32)

        xs = _dispatch(dest, h2, jnp.zeros((n_rows, D), F32))
        ys = _experts(block_expert, n_used[None], xs, w_gu[l].astype(BF16), b_gu[l][:, None],
                      w_down[l].astype(BF16), b_down[l][:, None])
        xc = _combine(dest, x1, wt, norm_final_g[None], ys).reshape(B, S, D)
    return xc
```

```python
import functools
import math

import jax
import jax.numpy as jnp
import numpy as np
from jax import lax
from jax.experimental import pallas as pl
from jax.experimental.pallas import tpu as pltpu

F32 = jnp.float32
BF16 = jnp.bfloat16
HIGHEST = lax.Precision.HIGHEST

ATT_HEADS = 8
ATT_HEAD_DIM = 64
ATT_WIDTH = ATT_HEADS * ATT_HEAD_DIM
MOBA_BLOCK = 256
MOBA_TOPK = 3
REL_BUCKETS = 32
REL_MAX_DIST = 128
CONV_K = 3
XATT_HEADS = 4
XATT_HEAD_DIM = 128
N_BRANCH = 3
N_EXPERTS = 32
TOP_K = 4
SWIGLU_LIMIT = 7.0
SWIGLU_ALPHA = 1.702
EPS = 1e-5
NEG = -1e30
LOG2E = math.log2(math.e)

LANES = 128
SEL_SLOTS = 16
TOKEN_TILE = 256
EXPERT_ROWS = 256
VMEM_LIMIT = 56 * 1024 * 1024


def _rms(x, g):
    return x * lax.rsqrt(jnp.mean(x * x, axis=-1, keepdims=True) + EPS) * g


def _dot(a, b):
    return jnp.dot(a, b, preferred_element_type=F32)


def _dot_nt(a, b, precision=None):
    return lax.dot_general(a, b, (((1,), (1,)), ((), ())), precision=precision,
                           preferred_element_type=F32)


def _sigmoid(x):
    return 1.0 / (1.0 + jnp.exp(-x))


def _mem_kv_kernel(mem_ref, g_ref, wkT_ref, wv_ref, mkT_ref, mv_ref):
    mn = _rms(mem_ref[0], g_ref[...]).astype(BF16)
    mkT_ref[0] = _dot_nt(wkT_ref[...], mn).astype(BF16)
    mv_ref[0] = _dot(mn, wv_ref[...]).astype(BF16)


def _mem_kv(mem, g, wkT, wv):
    B, M, D = mem.shape
    XW = wv.shape[1]
    return pl.pallas_call(
        _mem_kv_kernel,
        grid=(B,),
        in_specs=[pl.BlockSpec((1, M, D), lambda b: (b, 0, 0)),
                  pl.BlockSpec((1, D), lambda b: (0, 0)),
                  pl.BlockSpec((XW, D), lambda b: (0, 0)),
                  pl.BlockSpec((D, XW), lambda b: (0, 0))],
        out_specs=[pl.BlockSpec((1, XW, M), lambda b: (b, 0, 0)),
                   pl.BlockSpec((1, M, XW), lambda b: (b, 0, 0))],
        out_shape=[jax.ShapeDtypeStruct((B, XW, M), BF16),
                   jax.ShapeDtypeStruct((B, M, XW), BF16)],
        name="mem_kv",
    )(mem, g, wkT, wv)


def _qkv_select_kernel(x_ref, g_ref, wk_ref, wqvT_ref, qT_ref, k_ref, vTe_ref, vTo_ref, selbT_ref,
                       km_ref):
    i = pl.program_id(1)
    tq = x_ref.shape[1]
    W = ATT_WIDTH

    @pl.when(i == 0)
    def _():
        km_ref[...] = jnp.zeros_like(km_ref)

    h = _rms(x_ref[0], g_ref[...]).astype(BF16)
    k = _dot(h, wk_ref[...])
    k_ref[0] = k.astype(BF16)
    qvT = _dot_nt(wqvT_ref[...], h)
    qT = qvT[:W] * (1.0 / math.sqrt(ATT_HEAD_DIM))
    qT_ref[0] = (qT * LOG2E).astype(BF16)
    vT = qvT[W:]
    even_head = (lax.broadcasted_iota(jnp.int32, (W, tq), 0) // ATT_HEAD_DIM) % 2 == 0
    vTe_ref[0] = jnp.where(even_head, vT, 1.0).astype(BF16)
    vTo_ref[0] = jnp.where(even_head, 1.0, vT).astype(BF16)

    gate = jnp.dot(km_ref[...], qT, precision=HIGHEST, preferred_element_type=F32)
    g3 = gate.reshape(ATT_HEADS, SEL_SLOTS, tq)
    jj = lax.broadcasted_iota(jnp.int32, g3.shape, 1).astype(F32)
    valid = jj < i.astype(F32)
    g3 = jnp.where(valid, g3, -jnp.inf)
    sel = jnp.zeros(g3.shape, F32)
    for _ in range(MOBA_TOPK):
        m = jnp.max(g3, axis=1, keepdims=True)
        first = jnp.min(jnp.where(g3 == m, jj, float(SEL_SLOTS)), axis=1, keepdims=True)
        pick = jj == first
        sel = jnp.where(pick, 1.0, sel)
        g3 = jnp.where(pick, -jnp.inf, g3)
    keep = ((sel > 0.5) & valid) | (jj == i.astype(F32))
    selbT_ref[0] = jnp.where(keep, 0.0, NEG).reshape(ATT_HEADS * SEL_SLOTS, tq).astype(BF16)

    kmean = jnp.sum(k, axis=0, keepdims=True) * (1.0 / tq)
    lane_head = lax.broadcasted_iota(jnp.int32, (1, W), 1) // ATT_HEAD_DIM
    for hh in range(ATT_HEADS):
        km_ref[pl.ds(hh * SEL_SLOTS + i, 1), :] = jnp.where(lane_head == hh, kmean, 0.0)


def _qkv_select(x, g, wk, wqvT):
    B, S, D = x.shape
    W = ATT_WIDTH
    tq = MOBA_BLOCK
    nb = S // tq
    assert nb <= SEL_SLOTS and ATT_HEADS * SEL_SLOTS == LANES
    by_row = pl.BlockSpec((1, tq, W), lambda b, i: (b, i, 0))
    by_col = pl.BlockSpec((1, W, tq), lambda b, i: (b, 0, i))
    return pl.pallas_call(
        _qkv_select_kernel,
        grid=(B, nb),
        in_specs=[pl.BlockSpec((1, tq, D), lambda b, i: (b, i, 0)),
                  pl.BlockSpec((1, D), lambda b, i: (0, 0)),
                  pl.BlockSpec((D, W), lambda b, i: (0, 0)),
                  pl.BlockSpec((2 * W, D), lambda b, i: (0, 0))],
        out_specs=[by_col, by_row, by_col, by_col,
                   pl.BlockSpec((1, LANES, tq), lambda b, i: (b, 0, i))],
        out_shape=[jax.ShapeDtypeStruct((B, W, S), BF16),
                   jax.ShapeDtypeStruct((B, S, W), BF16),
                   jax.ShapeDtypeStruct((B, W, S), BF16),
                   jax.ShapeDtypeStruct((B, W, S), BF16),
                   jax.ShapeDtypeStruct((B, LANES, S), BF16)],
        scratch_shapes=[pltpu.VMEM((LANES, W), F32)],
        compiler_params=pltpu.CompilerParams(
            dimension_semantics=("arbitrary", "arbitrary"), vmem_limit_bytes=VMEM_LIMIT),
        name="qkv_select",
    )(x, g, wk, wqvT)


def _moba_kernel(qT_ref, selbT_ref, k_ref, vTe_ref, vTo_ref, tbT_ref, eT_ref, o_ref,
                 rhs_ref, m_ref, acc_ref):
    i = pl.program_id(1)
    tq = o_ref.shape[1]
    L = MOBA_BLOCK
    HD = ATT_HEAD_DIM
    H = ATT_HEADS
    selbT = selbT_ref[0].astype(F32)
    row = lax.broadcasted_iota(jnp.int32, (LANES, tq), 0)
    vT_refs = (vTe_ref, vTo_ref)
    slabs = [slice(p * LANES, (p + 1) * LANES) for p in range(H // 2)]

    for h in range(H):
        qTp = qT_ref[0, slabs[h // 2], :].astype(F32)
        own = (row >= HD) if h % 2 else (row < HD)
        rhs_ref[h, :LANES, :] = jnp.where(own, qTp, 0.0).astype(BF16)
        rhs_ref[h, LANES:, :] = jnp.where(row // SEL_SLOTS == h, selbT, 0.0).astype(BF16)
    m_ref[...] = jnp.full(m_ref.shape, -jnp.inf, F32)
    acc_ref[...] = jnp.zeros_like(acc_ref)

    def body(j, carry):
        koff = pl.multiple_of(j * L, L)
        t = jnp.minimum(i - j, 2)
        s = []
        for p in range(H // 2):
            lhs = jnp.concatenate([k_ref[0, pl.ds(koff, L), slabs[p]], eT_ref[j]], axis=1)
            s += [_dot(lhs, rhs_ref[2 * p + hh]) for hh in (0, 1)]
        pr, alpha = [], []
        for h in range(H):
            sh = s[h] + tbT_ref[t, h]
            cm = jnp.max(jnp.max(sh.reshape(8, L // 8, tq), axis=0), axis=0, keepdims=True)
            m_old = m_ref[h:h + 1, :]
            m_new = jnp.maximum(m_old, cm)
            m_ref[h:h + 1, :] = m_new
            pr.append(jnp.exp2(sh - m_new).astype(BF16))
            alpha.append(jnp.exp2(m_old - m_new))
        for h in range(H):
            pv = _dot(vT_refs[h % 2][0, slabs[h // 2], pl.ds(koff, L)], pr[h])
            acc_ref[h] = alpha[h] * acc_ref[h] + pv
        return carry

    lax.fori_loop(0, i + 1, body, 0)
    for p in range(H // 2):
        a0, a1 = acc_ref[2 * p], acc_ref[2 * p + 1]
        oT = jnp.concatenate([a0[:HD] / a0[HD:HD + 1], a1[HD:] / a1[0:1]], axis=0)
        o_ref[0, :, slabs[p]] = oT.T.astype(BF16)


def _moba(qT, selbT, k, vT_even, vT_odd, tbT):
    B, S, W = k.shape
    tq = MOBA_BLOCK
    nb = S // tq
    lane = np.arange(LANES)[None, None, :] % SEL_SLOTS
    eT_all = jnp.asarray(np.broadcast_to(lane == np.arange(SEL_SLOTS)[:, None, None],
                                         (SEL_SLOTS, MOBA_BLOCK, LANES)), BF16)
    whole_T = pl.BlockSpec((1, W, S), lambda b, i: (b, 0, 0))
    return pl.pallas_call(
        _moba_kernel,
        grid=(B, nb),
        in_specs=[pl.BlockSpec((1, W, tq), lambda b, i: (b, 0, i)),
                  pl.BlockSpec((1, LANES, tq), lambda b, i: (b, 0, i)),
                  pl.BlockSpec((1, S, W), lambda b, i: (b, 0, 0)),
                  whole_T, whole_T,
                  pl.BlockSpec(tbT.shape, lambda b, i: (0, 0, 0, 0)),
                  pl.BlockSpec(eT_all.shape, lambda b, i: (0, 0, 0))],
        out_specs=pl.BlockSpec((1, tq, W), lambda b, i: (b, i, 0)),
        out_shape=jax.ShapeDtypeStruct((B, S, W), BF16),
        scratch_shapes=[pltpu.VMEM((ATT_HEADS, 2 * LANES, tq), BF16),
                        pltpu.VMEM((ATT_HEADS, tq), F32),
                        pltpu.VMEM((ATT_HEADS, LANES, tq), F32)],
        compiler_params=pltpu.CompilerParams(
            dimension_semantics=("arbitrary", "arbitrary"), vmem_limit_bytes=VMEM_LIMIT),
        name="moba",
    )(qT, selbT, k, vT_even, vT_odd, tbT, eT_all)


def _merge_kernel(tiles_per_seq,
                  x_ref, ya_ref, mkT_ref, mv_ref, g1_ref, wr_ref, cw_ref, bg_ref,
                  wba_ref, wbc_ref, wbx_ref, wo_ref, g2_ref, wrt_ref, br_ref,
                  x1_ref, h2_ref, idx_ref, rank_ref, wt_ref, cnt_ref, zprev_ref):
    i = pl.program_id(0)
    tm, D = x_ref.shape
    CW = cw_ref.shape[1]
    XW = mv_ref.shape[2]

    @pl.when(i == 0)
    def _():
        cnt_ref[...] = jnp.zeros_like(cnt_ref)

    @pl.when(i % tiles_per_seq == 0)
    def _():
        zprev_ref[...] = jnp.zeros_like(zprev_ref)

    x = x_ref[...]
    h = _rms(x, g1_ref[...]).astype(BF16)
    pr = _dot(h, wr_ref[...])

    cb = pr[:, :CW]
    z = pr[:, CW:2 * CW] * pr[:, 2 * CW:3 * CW]
    row = lax.broadcasted_iota(jnp.int32, (tm, CW), 0)
    zp = zprev_ref[...]
    z1 = jnp.where(row == 0, zp[7:8], pltpu.roll(z, 1, 0))
    z2 = jnp.where(row == 0, zp[6:7], jnp.where(row == 1, zp[7:8], pltpu.roll(z, 2, 0)))
    zprev_ref[...] = z[tm - 8:]
    cw = cw_ref[...]
    y_conv = cb * (cw[0:1] * z2 + cw[1:2] * z1 + cw[2:3] * z)

    o0 = 3 * CW
    scale = 1.0 / math.sqrt(XATT_HEAD_DIM)
    ys = []
    for hh in range(XATT_HEADS):
        hs = slice(hh * XATT_HEAD_DIM, (hh + 1) * XATT_HEAD_DIM)
        qx = pr[:, o0 + hh * XATT_HEAD_DIM:o0 + (hh + 1) * XATT_HEAD_DIM].astype(BF16)
        s = _dot(qx, mkT_ref[0, hs, :]) * scale
        e = jnp.exp(s - jnp.max(s, axis=1, keepdims=True))
        l = jnp.sum(e, axis=1, keepdims=True)
        ys.append(_dot(e.astype(BF16), mv_ref[0, :, hs]) / l)
    y_x = jnp.concatenate(ys, axis=1)

    o1 = o0 + XW
    bg = bg_ref[...]
    merged = (_sigmoid(pr[:, o1:o1 + D] + bg[0:1]) * _dot(ya_ref[...], wba_ref[...])
              + _sigmoid(pr[:, o1 + D:o1 + 2 * D] + bg[1:2]) * _dot(y_conv.astype(BF16), wbc_ref[...])
              + _sigmoid(pr[:, o1 + 2 * D:o1 + 3 * D] + bg[2:3]) * _dot(y_x.astype(BF16), wbx_ref[...]))
    x1 = x + _dot(merged.astype(BF16), wo_ref[...])
    x1_ref[...] = x1
    h2 = _rms(x1, g2_ref[...])
    h2_ref[...] = h2

    lg = _dot_nt(wrt_ref[...], h2, precision=HIGHEST) + br_ref[...]
    E = lg.shape[0]
    ee = lax.broadcasted_iota(jnp.int32, (E, tm), 0).astype(F32)
    work = lg
    member = jnp.zeros((E, tm), F32)
    picks, vals = [], []
    for _ in range(TOP_K):
        m = jnp.max(work, axis=0, keepdims=True)
        first = jnp.min(jnp.where(work == m, ee, float(E)), axis=0, keepdims=True)
        pick = ee == first
        work = jnp.where(pick, -jnp.inf, work)
        member = jnp.where(pick, 1.0, member)
        picks.append((pick, first))
        vals.append(m)
    exps = [jnp.exp(v - vals[0]) for v in vals]
    denom = exps[0] + exps[1] + exps[2] + exps[3]

    r_i = lax.broadcasted_iota(jnp.int32, (tm, tm), 0)
    c_i = lax.broadcasted_iota(jnp.int32, (tm, tm), 1)
    upper = jnp.where(r_i < c_i, 1.0, 0.0).astype(BF16)
    before = _dot(member.astype(BF16), upper) + cnt_ref[:, 0:1]
    cnt_ref[...] = cnt_ref[...] + jnp.sum(member, axis=1, keepdims=True)

    wrow = lax.broadcasted_iota(jnp.int32, (LANES, tm), 0)
    wpad = jnp.zeros((LANES, tm), F32)
    for kk in range(TOP_K):
        pick, first = picks[kk]
        idx_ref[kk:kk + 1, :] = first.astype(jnp.int32)
        rank_ref[kk:kk + 1, :] = jnp.sum(jnp.where(pick, before, 0.0), axis=0,
                                         keepdims=True).astype(jnp.int32)
        wpad = jnp.where(wrow == kk, exps[kk] / denom, wpad)
    wt_ref[...] = wpad.T


def _merge(x2, yatt2, mkT, mv, g1, w_rest, conv_w, b_gate, wba, wbc, wbx, wo, g2, wrt, br, S):
    T, D = x2.shape
    tm = TOKEN_TILE
    nt = T // tm
    tps = S // tm
    CW = conv_w.shape[1]
    XW, M = mkT.shape[1], mkT.shape[2]
    E = wrt.shape[0]
    const = lambda shape: pl.BlockSpec(shape, lambda i: (0,) * len(shape))
    return pl.pallas_call(
        functools.partial(_merge_kernel, tps),
        grid=(nt,),
        in_specs=[pl.BlockSpec((tm, D), lambda i: (i, 0)),
                  pl.BlockSpec((tm, yatt2.shape[1]), lambda i: (i, 0)),
                  pl.BlockSpec((1, XW, M), lambda i: (i // tps, 0, 0)),
                  pl.BlockSpec((1, M, XW), lambda i: (i // tps, 0, 0)),
                  const((1, D)), const(w_rest.shape), const(conv_w.shape), const(b_gate.shape),
                  const(wba.shape), const(wbc.shape), const(wbx.shape), const(wo.shape),
                  const((1, D)), const(wrt.shape), const(br.shape)],
        out_specs=[pl.BlockSpec((tm, D), lambda i: (i, 0)),
                   pl.BlockSpec((tm, D), lambda i: (i, 0)),
                   pl.BlockSpec((TOP_K, tm), lambda i: (0, i)),
                   pl.BlockSpec((TOP_K, tm), lambda i: (0, i)),
                   pl.BlockSpec((tm, LANES), lambda i: (i, 0)),
                   pl.BlockSpec((E, LANES), lambda i: (0, 0))],
        out_shape=[jax.ShapeDtypeStruct((T, D), F32),
                   jax.ShapeDtypeStruct((T, D), F32),
                   jax.ShapeDtypeStruct((TOP_K, T), jnp.int32),
                   jax.ShapeDtypeStruct((TOP_K, T), jnp.int32),
                   jax.ShapeDtypeStruct((T, LANES), F32),
                   jax.ShapeDtypeStruct((E, LANES), F32)],
        scratch_shapes=[pltpu.VMEM((8, CW), F32)],
        compiler_params=pltpu.CompilerParams(
            dimension_semantics=("arbitrary",), vmem_limit_bytes=VMEM_LIMIT),
        name="merge_route",
    )(x2, yatt2, mkT, mv, g1, w_rest, conv_w, b_gate, wba, wbc, wbx, wo, g2, wrt, br)


def _dispatch_kernel(T, dest_ref, pe_ref, h2_ref, xs_ref, zero_ref, sem, zsem):
    i = pl.program_id(0)
    tm = h2_ref.shape[0]
    bm = zero_ref.shape[0]

    @pl.when(i == 0)
    def _():
        zero_ref[...] = jnp.zeros_like(zero_ref)

        def last_block(e):
            end = pe_ref[e]
            start = pl.multiple_of(jnp.maximum(end - bm, 0), bm)
            has_rows = end > (pe_ref[e - 1] if e else 0)
            return has_rows, pltpu.make_async_copy(zero_ref, xs_ref.at[pl.ds(start, bm)], zsem)

        def tail_block(b):
            start = pe_ref[N_EXPERTS - 1] + b * bm
            in_range = start < xs_ref.shape[0]
            start = pl.multiple_of(jnp.minimum(start, xs_ref.shape[0] - bm), bm)
            return in_range, pltpu.make_async_copy(zero_ref, xs_ref.at[pl.ds(start, bm)], zsem)

        fills = [last_block(e) for e in range(N_EXPERTS)] + [tail_block(b) for b in range(N_EXPERTS)]
        for cond, cp in fills:
            pl.when(cond)(cp.start)
        for cond, cp in fills:
            pl.when(cond)(cp.wait)

    def row_copy(t, d):
        return pltpu.make_async_copy(h2_ref.at[pl.ds(t, 1)], xs_ref.at[pl.ds(d, 1)], sem)

    def issue(t, c):
        for kk in range(TOP_K):
            row_copy(t, dest_ref[kk * T + i * tm + t]).start()
        return c

    lax.fori_loop(0, tm, issue, 0)

    def drain(t, c):
        for kk in range(TOP_K):
            row_copy(t, dest_ref[kk * T + i * tm + t]).wait()
        return c

    lax.fori_loop(0, tm, drain, 0)


def _dispatch(dest_flat, pad_ends, h2, n_rows):
    T, D = h2.shape
    tm = TOKEN_TILE
    return pl.pallas_call(
        functools.partial(_dispatch_kernel, T),
        grid_spec=pltpu.PrefetchScalarGridSpec(
            num_scalar_prefetch=2,
            grid=(T // tm,),
            in_specs=[pl.BlockSpec((tm, D), lambda i, d, pe: (i, 0))],
            out_specs=pl.BlockSpec(memory_space=pl.ANY),
            scratch_shapes=[pltpu.VMEM((EXPERT_ROWS, D), F32),
                            pltpu.SemaphoreType.DMA(()),
                            pltpu.SemaphoreType.DMA(())]),
        out_shape=jax.ShapeDtypeStruct((n_rows, D), F32),
        compiler_params=pltpu.CompilerParams(dimension_semantics=("arbitrary",)),
        name="dispatch",
    )(dest_flat, pad_ends, h2)


def _expert_kernel(be_ref, nu_ref, xs_ref, wgu_ref, bgu_ref, wd_ref, bd_ref, ys_ref):
    b = pl.program_id(0)
    F = wd_ref.shape[1]

    @pl.when(b < nu_ref[0])
    def _():
        gu = _dot(xs_ref[...].astype(BF16), wgu_ref[0]) + bgu_ref[0]
        g = jnp.minimum(gu[:, :F], SWIGLU_LIMIT)
        lin = jnp.clip(gu[:, F:], -SWIGLU_LIMIT, SWIGLU_LIMIT)
        act = (lin + 1.0) * (g * _sigmoid(SWIGLU_ALPHA * g))
        ys_ref[...] = _dot(act.astype(BF16), wd_ref[0]) + bd_ref[0]

    @pl.when(b >= nu_ref[0])
    def _():
        ys_ref[...] = jnp.zeros_like(ys_ref)


def _experts(block_expert, n_used, xs, wgu, bgu, wd, bd):
    NR, D = xs.shape
    bm = EXPERT_ROWS
    E, _, F2 = wgu.shape
    F = wd.shape[1]
    row_blk = lambda b, be, nu: (jnp.minimum(b, nu[0] - 1), 0)
    per_e = lambda b, be, nu: (be[b], 0, 0)
    return pl.pallas_call(
        _expert_kernel,
        grid_spec=pltpu.PrefetchScalarGridSpec(
            num_scalar_prefetch=2,
            grid=(NR // bm,),
            in_specs=[pl.BlockSpec((bm, D), row_blk),
                      pl.BlockSpec((1, D, F2), per_e),
                      pl.BlockSpec((1, 1, F2), per_e),
                      pl.BlockSpec((1, F, D), per_e),
                      pl.BlockSpec((1, 1, D), per_e)],
            out_specs=pl.BlockSpec((bm, D), lambda b, be, nu: (b, 0))),
        out_shape=jax.ShapeDtypeStruct((NR, D), F32),
        compiler_params=pltpu.CompilerParams(
            dimension_semantics=("arbitrary",), vmem_limit_bytes=VMEM_LIMIT),
        name="experts",
    )(block_expert, n_used, xs, wgu, bgu, wd, bd)


def _combine_kernel(T, dest_ref, x1_ref, wt_ref, g_ref, ys_ref, o_ref, buf_ref, sem):
    i = pl.program_id(0)
    n = pl.num_programs(0)
    tm = x1_ref.shape[0]

    def row_copy(step, slot, kk, t):
        d = dest_ref[kk * T + step * tm + t]
        return pltpu.make_async_copy(ys_ref.at[pl.ds(d, 1)], buf_ref.at[slot, kk, pl.ds(t, 1)],
                                     sem.at[slot])

    def issue(step, slot):
        def f(t, c):
            for kk in range(TOP_K):
                row_copy(step, slot, kk, t).start()
            return c
        lax.fori_loop(0, tm, f, 0)

    @pl.when(i == 0)
    def _():
        issue(0, 0)

    @pl.when(i + 1 < n)
    def _():
        issue(i + 1, (i + 1) % 2)

    slot = i % 2

    def drain(t, c):
        for kk in range(TOP_K):
            row_copy(i, slot, kk, t).wait()
        return c

    lax.fori_loop(0, tm, drain, 0)

    wt = wt_ref[...]
    y = x1_ref[...]
    for kk in range(TOP_K):
        y = y + wt[:, kk:kk + 1] * buf_ref[slot, kk]
    o_ref[...] = _rms(y, g_ref[...])


def _combine(dest_flat, x1, wt, g, ys):
    T, D = x1.shape
    tm = TOKEN_TILE
    return pl.pallas_call(
        functools.partial(_combine_kernel, T),
        grid_spec=pltpu.PrefetchScalarGridSpec(
            num_scalar_prefetch=1,
            grid=(T // tm,),
            in_specs=[pl.BlockSpec((tm, D), lambda i, d: (i, 0)),
                      pl.BlockSpec((tm, LANES), lambda i, d: (i, 0)),
                      pl.BlockSpec((1, D), lambda i, d: (0, 0)),
                      pl.BlockSpec(memory_space=pl.ANY)],
            out_specs=pl.BlockSpec((tm, D), lambda i, d: (i, 0)),
            scratch_shapes=[pltpu.VMEM((2, TOP_K, tm, D), F32),
                            pltpu.SemaphoreType.DMA((2,))]),
        out_shape=jax.ShapeDtypeStruct((T, D), F32),
        compiler_params=pltpu.CompilerParams(
            dimension_semantics=("arbitrary",), vmem_limit_bytes=VMEM_LIMIT),
        name="combine",
    )(dest_flat, x1, wt, g, ys)


def _t5_bucket(dist):
    n = jnp.maximum(dist, 0)
    max_exact = REL_BUCKETS // 2
    nf = jnp.maximum(n, 1).astype(F32)
    large = max_exact + (jnp.log(nf / max_exact) / math.log(REL_MAX_DIST / max_exact)
                         * (REL_BUCKETS - max_exact)).astype(jnp.int32)
    large = jnp.minimum(large, REL_BUCKETS - 1)
    return jnp.where(n < max_exact, n, large)


def _bias_tiles(rel_bias):
    L = MOBA_BLOCK
    assert REL_MAX_DIST <= L
    d = np.arange(-L, 3 * L)
    onehot = (_t5_bucket(jnp.asarray(d))[:, None] == jnp.arange(REL_BUCKETS)).astype(F32)
    by_dist = jnp.dot(onehot, rel_bias.astype(F32), precision=HIGHEST)
    by_dist = jnp.where((d >= 0)[:, None], by_dist, NEG).T
    H = by_dist.shape[0]
    tiles = []
    for t in range(3):
        v = jnp.concatenate([by_dist[:, (t + 1) * L:(t + 2) * L], by_dist[:, t * L:(t + 1) * L]],
                            axis=1)
        flat = jnp.tile(v, (1, L))[:, :L * (2 * L - 1)]
        tiles.append(flat.reshape(H, L, 2 * L - 1)[:, :, :L])
    return jnp.stack(tiles) * LOG2E


def kernel(x, mem, rel_bias, norm_mix_g, w_in, b_gate, conv_w, norm_mem_g, w_mem_kv, w_br_att,
           w_br_conv, w_br_xatt, w_out, norm_ffn_g, w_router, b_router, w_gu, b_gu, w_down,
           b_down, norm_final_g):
    B, S, D = x.shape
    T = B * S
    depth = w_in.shape[0]
    assert depth == 1, "the combine step applies the final norm: single-layer configuration only"
    W = ATT_WIDTH
    XW = XATT_HEADS * XATT_HEAD_DIM
    E = w_router.shape[2]
    bm = EXPERT_ROWS
    tb = _bias_tiles(rel_bias)

    xc = x
    for l in range(depth):
        w_l = w_in[l]
        wk = w_l[:, W:2 * W].astype(BF16)
        wqvT = jnp.concatenate([w_l[:, :W], w_l[:, 2 * W:3 * W]], axis=1).T.astype(BF16)
        w_rest = w_l[:, 3 * W:].astype(BF16)
        wm = w_mem_kv[l]

        mkT, mv = _mem_kv(mem, norm_mem_g[l][None], wm[:, :XW].T.astype(BF16),
                          wm[:, XW:].astype(BF16))
        qT, k, vT_even, vT_odd, selbT = _qkv_select(xc, norm_mix_g[l][None], wk, wqvT)
        y_att = _moba(qT, selbT, k, vT_even, vT_odd, tb)

        x1, h2, idx, rank, wt, cnt = _merge(
            xc.reshape(T, D), y_att.reshape(T, W), mkT, mv, norm_mix_g[l][None], w_rest,
            conv_w[l], b_gate[l], w_br_att[l].astype(BF16), w_br_conv[l].astype(BF16),
            w_br_xatt[l].astype(BF16), w_out[l].astype(BF16), norm_ffn_g[l][None],
            w_router[l].T, b_router[l][:, None], S)

        counts = cnt[:, 0].astype(jnp.int32)
        padded = (counts + bm - 1) // bm * bm
        pad_ends = jnp.cumsum(padded).astype(jnp.int32)
        pad_starts = pad_ends - padded
        onehot = idx[..., None] == jnp.arange(E, dtype=jnp.int32)
        dest = (jnp.sum(jnp.where(onehot, pad_starts, 0), axis=-1) + rank).reshape(-1)
        n_rows = T * TOP_K + E * bm
        n_blocks = n_rows // bm
        n_used = pad_ends[-1] // bm
        blk = jnp.minimum(jnp.arange(n_blocks, dtype=jnp.int32), n_used - 1) * bm
        block_expert = jnp.minimum(jnp.sum(blk[:, None] >= pad_ends[None, :], axis=1),
                                   E - 1).astype(jnp.int32)

        xs = _dispatch(dest, pad_ends, h2, n_rows)
        ys = _experts(block_expert, n_used[None], xs, w_gu[l].astype(BF16), b_gu[l][:, None],
                      w_down[l].astype(BF16), b_down[l][:, None])
        xc = _combine(dest, x1, wt, norm_final_g[None], ys).reshape(B, S, D)
    return xc
```

```python
import functools
import math

import jax
import jax.numpy as jnp
import numpy as np
from jax import lax
from jax.experimental import pallas as pl
from jax.experimental.pallas import tpu as pltpu

F32 = jnp.float32
BF16 = jnp.bfloat16
HIGHEST = lax.Precision.HIGHEST

ATT_HEADS = 8
ATT_HEAD_DIM = 64
ATT_WIDTH = ATT_HEADS * ATT_HEAD_DIM
MOBA_BLOCK = 256
MOBA_TOPK = 3
REL_BUCKETS = 32
REL_MAX_DIST = 128
CONV_K = 3
XATT_HEADS = 4
XATT_HEAD_DIM = 128
N_BRANCH = 3
N_EXPERTS = 32
TOP_K = 4
SWIGLU_LIMIT = 7.0
SWIGLU_ALPHA = 1.702
EPS = 1e-5
NEG = -1e30
LOG2E = math.log2(math.e)

LANES = 128
SEL_SLOTS = 16
TOKEN_TILE = 256
EXPERT_ROWS = 256
VMEM_LIMIT = 56 * 1024 * 1024


def _rms(x, g):
    return x * lax.rsqrt(jnp.mean(x * x, axis=-1, keepdims=True) + EPS) * g


def _dot(a, b):
    return jnp.dot(a, b, preferred_element_type=F32)


def _dot_nt(a, b, precision=None):
    return lax.dot_general(a, b, (((1,), (1,)), ((), ())), precision=precision,
                           preferred_element_type=F32)


def _sigmoid(x):
    return 1.0 / (1.0 + jnp.exp(-x))


def _mem_kv_kernel(mem_ref, g_ref, wkT_ref, wv_ref, mkT_ref, mv_ref):
    mn = _rms(mem_ref[0], g_ref[...]).astype(BF16)
    mkT_ref[0] = _dot_nt(wkT_ref[...], mn).astype(BF16)
    mv_ref[0] = _dot(mn, wv_ref[...]).astype(BF16)


def _mem_kv(mem, g, wkT, wv):
    B, M, D = mem.shape
    XW = wv.shape[1]
    return pl.pallas_call(
        _mem_kv_kernel,
        grid=(B,),
        in_specs=[pl.BlockSpec((1, M, D), lambda b: (b, 0, 0)),
                  pl.BlockSpec((1, D), lambda b: (0, 0)),
                  pl.BlockSpec((XW, D), lambda b: (0, 0)),
                  pl.BlockSpec((D, XW), lambda b: (0, 0))],
        out_specs=[pl.BlockSpec((1, XW, M), lambda b: (b, 0, 0)),
                   pl.BlockSpec((1, M, XW), lambda b: (b, 0, 0))],
        out_shape=[jax.ShapeDtypeStruct((B, XW, M), BF16),
                   jax.ShapeDtypeStruct((B, M, XW), BF16)],
        name="mem_kv",
    )(mem, g, wkT, wv)


def _qkv_select_kernel(x_ref, g_ref, wk_ref, wqvT_ref, qT_ref, k_ref, vTe_ref, vTo_ref, selbT_ref,
                       km_ref):
    i = pl.program_id(1)
    tq = x_ref.shape[1]
    W = ATT_WIDTH

    @pl.when(i == 0)
    def _():
        km_ref[...] = jnp.zeros_like(km_ref)

    h = _rms(x_ref[0], g_ref[...]).astype(BF16)
    k = _dot(h, wk_ref[...])
    k_ref[0] = k.astype(BF16)
    qvT = _dot_nt(wqvT_ref[...], h)
    qT = qvT[:W] * (1.0 / math.sqrt(ATT_HEAD_DIM))
    qT_ref[0] = (qT * LOG2E).astype(BF16)
    vT = qvT[W:]
    even_head = (lax.broadcasted_iota(jnp.int32, (W, tq), 0) // ATT_HEAD_DIM) % 2 == 0
    vTe_ref[0] = jnp.where(even_head, vT, 1.0).astype(BF16)
    vTo_ref[0] = jnp.where(even_head, 1.0, vT).astype(BF16)

    gate = jnp.dot(km_ref[...], qT, precision=HIGHEST, preferred_element_type=F32)
    g3 = gate.reshape(ATT_HEADS, SEL_SLOTS, tq)
    jj = lax.broadcasted_iota(jnp.int32, g3.shape, 1).astype(F32)
    valid = jj < i.astype(F32)
    g3 = jnp.where(valid, g3, -jnp.inf)
    sel = jnp.zeros(g3.shape, F32)
    for _ in range(MOBA_TOPK):
        m = jnp.max(g3, axis=1, keepdims=True)
        first = jnp.min(jnp.where(g3 == m, jj, float(SEL_SLOTS)), axis=1, keepdims=True)
        pick = jj == first
        sel = jnp.where(pick, 1.0, sel)
        g3 = jnp.where(pick, -jnp.inf, g3)
    keep = ((sel > 0.5) & valid) | (jj == i.astype(F32))
    selbT_ref[0] = jnp.where(keep, 0.0, NEG).reshape(ATT_HEADS * SEL_SLOTS, tq).astype(BF16)

    kmean = jnp.sum(k, axis=0, keepdims=True) * (1.0 / tq)
    lane_head = lax.broadcasted_iota(jnp.int32, (1, W), 1) // ATT_HEAD_DIM
    for hh in range(ATT_HEADS):
        km_ref[pl.ds(hh * SEL_SLOTS + i, 1), :] = jnp.where(lane_head == hh, kmean, 0.0)


def _qkv_select(x, g, wk, wqvT):
    B, S, D = x.shape
    W = ATT_WIDTH
    tq = MOBA_BLOCK
    nb = S // tq
    assert nb <= SEL_SLOTS and ATT_HEADS * SEL_SLOTS == LANES
    by_row = pl.BlockSpec((1, tq, W), lambda b, i: (b, i, 0))
    by_col = pl.BlockSpec((1, W, tq), lambda b, i: (b, 0, i))
    return pl.pallas_call(
        _qkv_select_kernel,
        grid=(B, nb),
        in_specs=[pl.BlockSpec((1, tq, D), lambda b, i: (b, i, 0)),
                  pl.BlockSpec((1, D), lambda b, i: (0, 0)),
                  pl.BlockSpec((D, W), lambda b, i: (0, 0)),
                  pl.BlockSpec((2 * W, D), lambda b, i: (0, 0))],
        out_specs=[by_col, by_row, by_col, by_col,
                   pl.BlockSpec((1, LANES, tq), lambda b, i: (b, 0, i))],
        out_shape=[jax.ShapeDtypeStruct((B, W, S), BF16),
                   jax.ShapeDtypeStruct((B, S, W), BF16),
                   jax.ShapeDtypeStruct((B, W, S), BF16),
                   jax.ShapeDtypeStruct((B, W, S), BF16),
                   jax.ShapeDtypeStruct((B, LANES, S), BF16)],
        scratch_shapes=[pltpu.VMEM((LANES, W), F32)],
        compiler_params=pltpu.CompilerParams(
            dimension_semantics=("arbitrary", "arbitrary"), vmem_limit_bytes=VMEM_LIMIT),
        name="qkv_select",
    )(x, g, wk, wqvT)


def _moba_kernel(qT_ref, selbT_ref, k_ref, vTe_ref, vTo_ref, tbT_ref, eT_ref, o_ref,
                 rhs_ref, m_ref, acc_ref):
    i = pl.program_id(1)
    tq = o_ref.shape[1]
    L = MOBA_BLOCK
    HD = ATT_HEAD_DIM
    H = ATT_HEADS
    selbT = selbT_ref[0].astype(F32)
    row = lax.broadcasted_iota(jnp.int32, (LANES, tq), 0)
    vT_refs = (vTe_ref, vTo_ref)
    slabs = [slice(p * LANES, (p + 1) * LANES) for p in range(H // 2)]

    for h in range(H):
        qTp = qT_ref[0, slabs[h // 2], :].astype(F32)
        own = (row >= HD) if h % 2 else (row < HD)
        rhs_ref[h, :LANES, :] = jnp.where(own, qTp, 0.0).astype(BF16)
        rhs_ref[h, LANES:, :] = jnp.where(row // SEL_SLOTS == h, selbT, 0.0).astype(BF16)
    m_ref[...] = jnp.full(m_ref.shape, -jnp.inf, F32)
    acc_ref[...] = jnp.zeros_like(acc_ref)

    def body(j, carry):
        koff = pl.multiple_of(j * L, L)
        t = jnp.minimum(i - j, 2)
        s = []
        for p in range(H // 2):
            lhs = jnp.concatenate([k_ref[0, pl.ds(koff, L), slabs[p]], eT_ref[j]], axis=1)
            s += [_dot(lhs, rhs_ref[2 * p + hh]) for hh in (0, 1)]
        pr, alpha = [], []
        for h in range(H):
            sh = s[h] + tbT_ref[t, h]
            cm = jnp.max(jnp.max(sh.reshape(8, L // 8, tq), axis=0), axis=0, keepdims=True)
            m_old = m_ref[h:h + 1, :]
            m_new = jnp.maximum(m_old, cm)
            m_ref[h:h + 1, :] = m_new
            pr.append(jnp.exp2(sh - m_new).astype(BF16))
            alpha.append(jnp.exp2(m_old - m_new))
        for h in range(H):
            pv = _dot(vT_refs[h % 2][0, slabs[h // 2], pl.ds(koff, L)], pr[h])
            acc_ref[h] = alpha[h] * acc_ref[h] + pv
        return carry

    lax.fori_loop(0, i + 1, body, 0)
    for p in range(H // 2):
        a0, a1 = acc_ref[2 * p], acc_ref[2 * p + 1]
        oT = jnp.concatenate([a0[:HD] / a0[HD:HD + 1], a1[HD:] / a1[0:1]], axis=0)
        o_ref[0, :, slabs[p]] = oT.T.astype(BF16)


def _moba(qT, selbT, k, vT_even, vT_odd, tbT):
    B, S, W = k.shape
    tq = MOBA_BLOCK
    nb = S // tq
    lane = np.arange(LANES)[None, None, :] % SEL_SLOTS
    eT_all = jnp.asarray(np.broadcast_to(lane == np.arange(SEL_SLOTS)[:, None, None],
                                         (SEL_SLOTS, MOBA_BLOCK, LANES)), BF16)
    whole_T = pl.BlockSpec((1, W, S), lambda b, i: (b, 0, 0))
    return pl.pallas_call(
        _moba_kernel,
        grid=(B, nb),
        in_specs=[pl.BlockSpec((1, W, tq), lambda b, i: (b, 0, i)),
                  pl.BlockSpec((1, LANES, tq), lambda b, i: (b, 0, i)),
                  pl.BlockSpec((1, S, W), lambda b, i: (b, 0, 0)),
                  whole_T, whole_T,
                  pl.BlockSpec(tbT.shape, lambda b, i: (0, 0, 0, 0)),
                  pl.BlockSpec(eT_all.shape, lambda b, i: (0, 0, 0))],
        out_specs=pl.BlockSpec((1, tq, W), lambda b, i: (b, i, 0)),
        out_shape=jax.ShapeDtypeStruct((B, S, W), BF16),
        scratch_shapes=[pltpu.VMEM((ATT_HEADS, 2 * LANES, tq), BF16),
                        pltpu.VMEM((ATT_HEADS, tq), F32),
                        pltpu.VMEM((ATT_HEADS, LANES, tq), F32)],
        compiler_params=pltpu.CompilerParams(
            dimension_semantics=("arbitrary", "arbitrary"), vmem_limit_bytes=VMEM_LIMIT),
        name="moba",
    )(qT, selbT, k, vT_even, vT_odd, tbT, eT_all)


def _merge_kernel(tiles_per_seq,
                  x_ref, ya_ref, mkT_ref, mv_ref, g1_ref, wr_ref, cw_ref, bg_ref,
                  wba_ref, wbc_ref, wbx_ref, wo_ref, g2_ref, wrt_ref, br_ref,
                  x1_ref, h2_ref, idx_ref, rank_ref, wt_ref, cnt_ref, zprev_ref):
    i = pl.program_id(0)
    tm, D = x_ref.shape
    CW = cw_ref.shape[1]
    XW = mv_ref.shape[2]

    @pl.when(i == 0)
    def _():
        cnt_ref[...] = jnp.zeros_like(cnt_ref)

    @pl.when(i % tiles_per_seq == 0)
    def _():
        zprev_ref[...] = jnp.zeros_like(zprev_ref)

    x = x_ref[...]
    h = _rms(x, g1_ref[...]).astype(BF16)
    pr = _dot(h, wr_ref[...])

    cb = pr[:, :CW]
    z = pr[:, CW:2 * CW] * pr[:, 2 * CW:3 * CW]
    row = lax.broadcasted_iota(jnp.int32, (tm, CW), 0)
    zp = zprev_ref[...]
    z1 = jnp.where(row == 0, zp[7:8], pltpu.roll(z, 1, 0))
    z2 = jnp.where(row == 0, zp[6:7], jnp.where(row == 1, zp[7:8], pltpu.roll(z, 2, 0)))
    zprev_ref[...] = z[tm - 8:]
    cw = cw_ref[...]
    y_conv = cb * (cw[0:1] * z2 + cw[1:2] * z1 + cw[2:3] * z)

    o0 = 3 * CW
    scale = 1.0 / math.sqrt(XATT_HEAD_DIM)
    ys = []
    for hh in range(XATT_HEADS):
        hs = slice(hh * XATT_HEAD_DIM, (hh + 1) * XATT_HEAD_DIM)
        qx = pr[:, o0 + hh * XATT_HEAD_DIM:o0 + (hh + 1) * XATT_HEAD_DIM].astype(BF16)
        s = _dot(qx, mkT_ref[0, hs, :]) * scale
        e = jnp.exp(s - jnp.max(s, axis=1, keepdims=True))
        l = jnp.sum(e, axis=1, keepdims=True)
        ys.append(_dot(e.astype(BF16), mv_ref[0, :, hs]) / l)
    y_x = jnp.concatenate(ys, axis=1)

    o1 = o0 + XW
    bg = bg_ref[...]
    merged = (_sigmoid(pr[:, o1:o1 + D] + bg[0:1]) * _dot(ya_ref[...], wba_ref[...])
              + _sigmoid(pr[:, o1 + D:o1 + 2 * D] + bg[1:2]) * _dot(y_conv.astype(BF16), wbc_ref[...])
              + _sigmoid(pr[:, o1 + 2 * D:o1 + 3 * D] + bg[2:3]) * _dot(y_x.astype(BF16), wbx_ref[...]))
    x1 = x + _dot(merged.astype(BF16), wo_ref[...])
    x1_ref[...] = x1
    h2 = _rms(x1, g2_ref[...])
    for c in range(D // LANES):
        h2_ref[pl.ds(c, tm, stride=D // LANES), :] = h2[:, c * LANES:(c + 1) * LANES]

    lg = _dot_nt(wrt_ref[...], h2, precision=HIGHEST) + br_ref[...]
    E = lg.shape[0]
    ee = lax.broadcasted_iota(jnp.int32, (E, tm), 0).astype(F32)
    work = lg
    member = jnp.zeros((E, tm), F32)
    picks, vals = [], []
    for _ in range(TOP_K):
        m = jnp.max(work, axis=0, keepdims=True)
        first = jnp.min(jnp.where(work == m, ee, float(E)), axis=0, keepdims=True)
        pick = ee == first
        work = jnp.where(pick, -jnp.inf, work)
        member = jnp.where(pick, 1.0, member)
        picks.append((pick, first))
        vals.append(m)
    exps = [jnp.exp(v - vals[0]) for v in vals]
    denom = exps[0] + exps[1] + exps[2] + exps[3]

    r_i = lax.broadcasted_iota(jnp.int32, (tm, tm), 0)
    c_i = lax.broadcasted_iota(jnp.int32, (tm, tm), 1)
    upper = jnp.where(r_i < c_i, 1.0, 0.0).astype(BF16)
    before = _dot(member.astype(BF16), upper) + cnt_ref[:, 0:1]
    cnt_ref[...] = cnt_ref[...] + jnp.sum(member, axis=1, keepdims=True)

    wrow = lax.broadcasted_iota(jnp.int32, (LANES, tm), 0)
    wpad = jnp.zeros((LANES, tm), F32)
    for kk in range(TOP_K):
        pick, first = picks[kk]
        idx_ref[kk:kk + 1, :] = first.astype(jnp.int32)
        rank_ref[kk:kk + 1, :] = jnp.sum(jnp.where(pick, before, 0.0), axis=0,
                                         keepdims=True).astype(jnp.int32)
        wpad = jnp.where(wrow == kk, exps[kk] / denom, wpad)
    wt_ref[...] = wpad.T


def _merge(x2, yatt2, mkT, mv, g1, w_rest, conv_w, b_gate, wba, wbc, wbx, wo, g2, wrt, br, S):
    T, D = x2.shape
    tm = TOKEN_TILE
    nt = T // tm
    tps = S // tm
    CW = conv_w.shape[1]
    XW, M = mkT.shape[1], mkT.shape[2]
    E = wrt.shape[0]
    const = lambda shape: pl.BlockSpec(shape, lambda i: (0,) * len(shape))
    return pl.pallas_call(
        functools.partial(_merge_kernel, tps),
        grid=(nt,),
        in_specs=[pl.BlockSpec((tm, D), lambda i: (i, 0)),
                  pl.BlockSpec((tm, yatt2.shape[1]), lambda i: (i, 0)),
                  pl.BlockSpec((1, XW, M), lambda i: (i // tps, 0, 0)),
                  pl.BlockSpec((1, M, XW), lambda i: (i // tps, 0, 0)),
                  const((1, D)), const(w_rest.shape), const(conv_w.shape), const(b_gate.shape),
                  const(wba.shape), const(wbc.shape), const(wbx.shape), const(wo.shape),
                  const((1, D)), const(wrt.shape), const(br.shape)],
        out_specs=[pl.BlockSpec((tm, D), lambda i: (i, 0)),
                   pl.BlockSpec((tm * (D // LANES), LANES), lambda i: (i, 0)),
                   pl.BlockSpec((TOP_K, tm), lambda i: (0, i)),
                   pl.BlockSpec((TOP_K, tm), lambda i: (0, i)),
                   pl.BlockSpec((tm, LANES), lambda i: (i, 0)),
                   pl.BlockSpec((E, LANES), lambda i: (0, 0))],
        out_shape=[jax.ShapeDtypeStruct((T, D), F32),
                   jax.ShapeDtypeStruct((T * (D // LANES), LANES), F32),
                   jax.ShapeDtypeStruct((TOP_K, T), jnp.int32),
                   jax.ShapeDtypeStruct((TOP_K, T), jnp.int32),
                   jax.ShapeDtypeStruct((T, LANES), F32),
                   jax.ShapeDtypeStruct((E, LANES), F32)],
        scratch_shapes=[pltpu.VMEM((8, CW), F32)],
        compiler_params=pltpu.CompilerParams(
            dimension_semantics=("arbitrary",), vmem_limit_bytes=VMEM_LIMIT),
        name="merge_route",
    )(x2, yatt2, mkT, mv, g1, w_rest, conv_w, b_gate, wba, wbc, wbx, wo, g2, wrt, br)


def _dispatch_kernel(T, dest_ref, pe_ref, h2_ref, xs_ref, zero_ref, sem, zsem):
    i = pl.program_id(0)
    tm = h2_ref.shape[0]
    bm = zero_ref.shape[0]

    @pl.when(i == 0)
    def _():
        zero_ref[...] = jnp.zeros_like(zero_ref)

        def last_block(e):
            end = pe_ref[e]
            start = pl.multiple_of(jnp.maximum(end - bm, 0), bm)
            has_rows = end > (pe_ref[e - 1] if e else 0)
            return has_rows, pltpu.make_async_copy(zero_ref, xs_ref.at[pl.ds(start, bm)], zsem)

        def tail_block(b):
            start = pe_ref[N_EXPERTS - 1] + b * bm
            in_range = start < xs_ref.shape[0]
            start = pl.multiple_of(jnp.minimum(start, xs_ref.shape[0] - bm), bm)
            return in_range, pltpu.make_async_copy(zero_ref, xs_ref.at[pl.ds(start, bm)], zsem)

        fills = [last_block(e) for e in range(N_EXPERTS)] + [tail_block(b) for b in range(N_EXPERTS)]
        for cond, cp in fills:
            pl.when(cond)(cp.start)
        for cond, cp in fills:
            pl.when(cond)(cp.wait)

    def row_copy(t, d):
        return pltpu.make_async_copy(h2_ref.at[pl.ds(t, 1)], xs_ref.at[pl.ds(d, 1)], sem)

    def issue(t, c):
        for kk in range(TOP_K):
            row_copy(t, dest_ref[kk * T + i * tm + t]).start()
        return c

    lax.fori_loop(0, tm, issue, 0)

    def drain(t, c):
        for kk in range(TOP_K):
            row_copy(t, dest_ref[kk * T + i * tm + t]).wait()
        return c

    lax.fori_loop(0, tm, drain, 0)


def _dispatch(dest_flat, pad_ends, h2, n_rows):
    T, D = h2.shape
    tm = TOKEN_TILE
    return pl.pallas_call(
        functools.partial(_dispatch_kernel, T),
        grid_spec=pltpu.PrefetchScalarGridSpec(
            num_scalar_prefetch=2,
            grid=(T // tm,),
            in_specs=[pl.BlockSpec((tm, D), lambda i, d, pe: (i, 0))],
            out_specs=pl.BlockSpec(memory_space=pl.ANY),
            scratch_shapes=[pltpu.VMEM((EXPERT_ROWS, D), F32),
                            pltpu.SemaphoreType.DMA(()),
                            pltpu.SemaphoreType.DMA(())]),
        out_shape=jax.ShapeDtypeStruct((n_rows, D), F32),
        compiler_params=pltpu.CompilerParams(dimension_semantics=("arbitrary",)),
        name="dispatch",
    )(dest_flat, pad_ends, h2)


def _expert_kernel(be_ref, nu_ref, xs_ref, wgu_ref, bgu_ref, wd_ref, bd_ref, ys_ref):
    b = pl.program_id(0)
    F = wd_ref.shape[1]

    @pl.when(b < nu_ref[0])
    def _():
        gu = _dot(xs_ref[...].astype(BF16), wgu_ref[0]) + bgu_ref[0]
        g = jnp.minimum(gu[:, :F], SWIGLU_LIMIT)
        lin = jnp.clip(gu[:, F:], -SWIGLU_LIMIT, SWIGLU_LIMIT)
        act = (lin + 1.0) * (g * _sigmoid(SWIGLU_ALPHA * g))
        ys_ref[...] = _dot(act.astype(BF16), wd_ref[0]) + bd_ref[0]

    @pl.when(b >= nu_ref[0])
    def _():
        ys_ref[...] = jnp.zeros_like(ys_ref)


def _experts(block_expert, n_used, xs, wgu, bgu, wd, bd):
    NR, D = xs.shape
    bm = EXPERT_ROWS
    E, _, F2 = wgu.shape
    F = wd.shape[1]
    row_blk = lambda b, be, nu: (jnp.minimum(b, nu[0] - 1), 0)
    per_e = lambda b, be, nu: (be[b], 0, 0)
    return pl.pallas_call(
        _expert_kernel,
        grid_spec=pltpu.PrefetchScalarGridSpec(
            num_scalar_prefetch=2,
            grid=(NR // bm,),
            in_specs=[pl.BlockSpec((bm, D), row_blk),
                      pl.BlockSpec((1, D, F2), per_e),
                      pl.BlockSpec((1, 1, F2), per_e),
                      pl.BlockSpec((1, F, D), per_e),
                      pl.BlockSpec((1, 1, D), per_e)],
            out_specs=pl.BlockSpec((bm, D), lambda b, be, nu: (b, 0))),
        out_shape=jax.ShapeDtypeStruct((NR, D), F32),
        compiler_params=pltpu.CompilerParams(
            dimension_semantics=("arbitrary",), vmem_limit_bytes=VMEM_LIMIT),
        name="experts",
    )(block_expert, n_used, xs, wgu, bgu, wd, bd)


def _combine_kernel(T, dest_ref, x1_ref, wt_ref, g_ref, ys_ref, o_ref, buf_ref, sem):
    i = pl.program_id(0)
    n = pl.num_programs(0)
    tm = x1_ref.shape[0]

    def row_copy(step, slot, kk, t):
        d = dest_ref[kk * T + step * tm + t]
        return pltpu.make_async_copy(ys_ref.at[pl.ds(d, 1)], buf_ref.at[slot, kk, pl.ds(t, 1)],
                                     sem.at[slot])

    def issue(step, slot):
        def f(t, c):
            for kk in range(TOP_K):
                row_copy(step, slot, kk, t).start()
            return c
        lax.fori_loop(0, tm, f, 0)

    @pl.when(i == 0)
    def _():
        issue(0, 0)

    @pl.when(i + 1 < n)
    def _():
        issue(i + 1, (i + 1) % 2)

    slot = i % 2

    def drain(t, c):
        for kk in range(TOP_K):
            row_copy(i, slot, kk, t).wait()
        return c

    lax.fori_loop(0, tm, drain, 0)

    wt = wt_ref[...]
    y = x1_ref[...]
    for kk in range(TOP_K):
        y = y + wt[:, kk:kk + 1] * buf_ref[slot, kk]
    o_ref[...] = _rms(y, g_ref[...])


def _combine(dest_flat, x1, wt, g, ys):
    T, D = x1.shape
    tm = TOKEN_TILE
    return pl.pallas_call(
        functools.partial(_combine_kernel, T),
        grid_spec=pltpu.PrefetchScalarGridSpec(
            num_scalar_prefetch=1,
            grid=(T // tm,),
            in_specs=[pl.BlockSpec((tm, D), lambda i, d: (i, 0)),
                      pl.BlockSpec((tm, LANES), lambda i, d: (i, 0)),
                      pl.BlockSpec((1, D), lambda i, d: (0, 0)),
                      pl.BlockSpec(memory_space=pl.ANY)],
            out_specs=pl.BlockSpec((tm, D), lambda i, d: (i, 0)),
            scratch_shapes=[pltpu.VMEM((2, TOP_K, tm, D), F32),
                            pltpu.SemaphoreType.DMA((2,))]),
        out_shape=jax.ShapeDtypeStruct((T, D), F32),
        compiler_params=pltpu.CompilerParams(
            dimension_semantics=("arbitrary",), vmem_limit_bytes=VMEM_LIMIT),
        name="combine",
    )(dest_flat, x1, wt, g, ys)


def _moe_kernel(T, be_ref, nu_ref, src_ref, h2t_ref, wgu_ref, bgu_ref, wd_ref, bd_ref, y4_ref,
                xbuf0, xbuf1, ybuf0, ybuf1, gsem, ssem):
    b = pl.program_id(0)
    n_used = nu_ref[0]
    bm = EXPERT_ROWS
    F = wd_ref.shape[1]
    CH = wgu_ref.shape[1] // LANES
    xbuf = (xbuf0, xbuf1)
    ybuf = (ybuf0, ybuf1)

    def gather_start(blk, slot):
        for r in range(bm):
            tok = src_ref[blk * bm + r] & (T - 1)
            pltpu.make_async_copy(h2t_ref.at[pl.ds(pl.multiple_of(tok * CH, CH), CH)],
                                  xbuf[slot].at[pl.ds(r * CH, CH)], gsem.at[slot]).start()

    def scatter_start(blk, slot):
        for r in range(bm):
            a = src_ref[blk * bm + r]
            pltpu.make_async_copy(ybuf[slot].at[pl.ds(r * CH, CH)],
                                  y4_ref.at[pl.ds(pl.multiple_of(a * CH, CH), CH)],
                                  ssem.at[slot]).start()

    def gather_wait(slot):
        pltpu.make_async_copy(h2t_ref.at[pl.ds(0, bm * CH)], xbuf[slot], gsem.at[slot]).wait()

    def scatter_wait(slot):
        pltpu.make_async_copy(ybuf[slot], y4_ref.at[pl.ds(0, bm * CH)], ssem.at[slot]).wait()

    def load_rows(slot):
        x = jnp.concatenate([xbuf[slot][pl.ds(c, bm, stride=CH), :] for c in range(CH)], axis=1)
        return x.astype(BF16)

    def ffn(slot, x):
        gu = _dot(x, wgu_ref[0]) + bgu_ref[0]
        g = jnp.minimum(gu[:, :F], SWIGLU_LIMIT)
        lin = jnp.clip(gu[:, F:], -SWIGLU_LIMIT, SWIGLU_LIMIT)
        act = (lin + 1.0) * (g * _sigmoid(SWIGLU_ALPHA * g))
        y = _dot(act.astype(BF16), wd_ref[0]) + bd_ref[0]
        for c in range(CH):
            ybuf[slot][pl.ds(c, bm, stride=CH), :] = y[:, c * LANES:(c + 1) * LANES]

    @pl.when(b == 0)
    def _():
        gather_start(0, 0)
        gather_wait(0)
        x = load_rows(0)
        gather_start(1, 1)
        ffn(0, x)

    for slot in (0, 1):
        @pl.when((b > 0) & (b < n_used) & (b % 2 == slot))
        def _():
            gather_wait(slot)

            @pl.when(b >= 2)
            def _():
                scatter_wait(slot)

            x = load_rows(slot)
            gather_start(b + 1, 1 - slot)
            scatter_start(b - 1, 1 - slot)
            ffn(slot, x)

        @pl.when((b == n_used - 1) & (b % 2 == slot))
        def _():
            scatter_start(b, slot)
            gather_wait(1 - slot)

            @pl.when(b >= 1)
            def _():
                scatter_wait(1 - slot)

            scatter_wait(slot)


def _moe_experts(T, block_expert, n_used, src, h2t, wgu, bgu, wd, bd):
    bm = EXPERT_ROWS
    E, D, F2 = wgu.shape
    F = wd.shape[1]
    CH = D // LANES
    n_blocks = block_expert.shape[0]
    assert T & (T - 1) == 0 and src.shape[0] == (n_blocks + 1) * bm
    per_e = lambda b, be, nu, s: (be[b], 0, 0)
    return pl.pallas_call(
        functools.partial(_moe_kernel, T),
        grid_spec=pltpu.PrefetchScalarGridSpec(
            num_scalar_prefetch=3,
            grid=(n_blocks,),
            in_specs=[pl.BlockSpec(memory_space=pl.ANY),
                      pl.BlockSpec((1, D, F2), per_e),
                      pl.BlockSpec((1, 1, F2), per_e),
                      pl.BlockSpec((1, F, D), per_e),
                      pl.BlockSpec((1, 1, D), per_e)],
            out_specs=pl.BlockSpec(memory_space=pl.ANY),
            scratch_shapes=[pltpu.VMEM((bm * CH, LANES), F32)] * 4 + [
                            pltpu.SemaphoreType.DMA((2,)),
                            pltpu.SemaphoreType.DMA((2,))]),
        out_shape=jax.ShapeDtypeStruct(((TOP_K * T + 2 * bm) * CH, LANES), F32),
        compiler_params=pltpu.CompilerParams(
            dimension_semantics=("arbitrary",), vmem_limit_bytes=VMEM_LIMIT),
        name="moe_experts",
    )(block_expert, n_used, src, h2t, wgu, bgu, wd, bd)


def _finish_kernel(x1_ref, wt_ref, g_ref, y0_ref, y1_ref, y2_ref, y3_ref, o_ref):
    tm, D = x1_ref.shape
    CH = D // LANES
    wt = wt_ref[...]
    y = x1_ref[...]
    for kk, yk_ref in enumerate((y0_ref, y1_ref, y2_ref, y3_ref)):
        yk = jnp.concatenate([yk_ref[pl.ds(c, tm, stride=CH), :] for c in range(CH)], axis=1)
        y = y + wt[:, kk:kk + 1] * yk
    o_ref[...] = _rms(y, g_ref[...])


def _finish(x1, wt, g, y4):
    T, D = x1.shape
    tm = TOKEN_TILE
    CH = D // LANES
    nt = T // tm
    choice = lambda kk: pl.BlockSpec((tm * CH, LANES), lambda i: (kk * nt + i, 0))
    return pl.pallas_call(
        _finish_kernel,
        grid=(nt,),
        in_specs=[pl.BlockSpec((tm, D), lambda i: (i, 0)),
                  pl.BlockSpec((tm, LANES), lambda i: (i, 0)),
                  pl.BlockSpec((1, D), lambda i: (0, 0)),
                  choice(0), choice(1), choice(2), choice(3)],
        out_specs=pl.BlockSpec((tm, D), lambda i: (i, 0)),
        out_shape=jax.ShapeDtypeStruct((T, D), F32),
        compiler_params=pltpu.CompilerParams(
            dimension_semantics=("arbitrary",), vmem_limit_bytes=VMEM_LIMIT),
        name="finish",
    )(x1, wt, g, y4, y4, y4, y4)


def _t5_bucket(dist):
    n = jnp.maximum(dist, 0)
    max_exact = REL_BUCKETS // 2
    nf = jnp.maximum(n, 1).astype(F32)
    large = max_exact + (jnp.log(nf / max_exact) / math.log(REL_MAX_DIST / max_exact)
                         * (REL_BUCKETS - max_exact)).astype(jnp.int32)
    large = jnp.minimum(large, REL_BUCKETS - 1)
    return jnp.where(n < max_exact, n, large)


def _bias_tiles(rel_bias):
    L = MOBA_BLOCK
    assert REL_MAX_DIST <= L
    d = np.arange(-L, 3 * L)
    onehot = (_t5_bucket(jnp.asarray(d))[:, None] == jnp.arange(REL_BUCKETS)).astype(F32)
    by_dist = jnp.dot(onehot, rel_bias.astype(F32), precision=HIGHEST)
    by_dist = jnp.where((d >= 0)[:, None], by_dist, NEG).T
    H = by_dist.shape[0]
    tiles = []
    for t in range(3):
        v = jnp.concatenate([by_dist[:, (t + 1) * L:(t + 2) * L], by_dist[:, t * L:(t + 1) * L]],
                            axis=1)
        flat = jnp.tile(v, (1, L))[:, :L * (2 * L - 1)]
        tiles.append(flat.reshape(H, L, 2 * L - 1)[:, :, :L])
    return jnp.stack(tiles) * LOG2E


def kernel(x, mem, rel_bias, norm_mix_g, w_in, b_gate, conv_w, norm_mem_g, w_mem_kv, w_br_att,
           w_br_conv, w_br_xatt, w_out, norm_ffn_g, w_router, b_router, w_gu, b_gu, w_down,
           b_down, norm_final_g):
    B, S, D = x.shape
    T = B * S
    depth = w_in.shape[0]
    assert depth == 1, "the combine step applies the final norm: single-layer configuration only"
    W = ATT_WIDTH
    XW = XATT_HEADS * XATT_HEAD_DIM
    E = w_router.shape[2]
    bm = EXPERT_ROWS
    tb = _bias_tiles(rel_bias)

    xc = x
    for l in range(depth):
        w_l = w_in[l]
        wk = w_l[:, W:2 * W].astype(BF16)
        wqvT = jnp.concatenate([w_l[:, :W], w_l[:, 2 * W:3 * W]], axis=1).T.astype(BF16)
        w_rest = w_l[:, 3 * W:].astype(BF16)
        wm = w_mem_kv[l]

        mkT, mv = _mem_kv(mem, norm_mem_g[l][None], wm[:, :XW].T.astype(BF16),
                          wm[:, XW:].astype(BF16))
        qT, k, vT_even, vT_odd, selbT = _qkv_select(xc, norm_mix_g[l][None], wk, wqvT)
        y_att = _moba(qT, selbT, k, vT_even, vT_odd, tb)

        x1, h2, idx, rank, wt, cnt = _merge(
            xc.reshape(T, D), y_att.reshape(T, W), mkT, mv, norm_mix_g[l][None], w_rest,
            conv_w[l], b_gate[l], w_br_att[l].astype(BF16), w_br_conv[l].astype(BF16),
            w_br_xatt[l].astype(BF16), w_out[l].astype(BF16), norm_ffn_g[l][None],
            w_router[l].T, b_router[l][:, None], S)

        counts = cnt[:, 0].astype(jnp.int32)
        padded = (counts + bm - 1) // bm * bm
        pad_ends = jnp.cumsum(padded).astype(jnp.int32)
        pad_starts = pad_ends - padded
        onehot = idx[..., None] == jnp.arange(E, dtype=jnp.int32)
        dest = (jnp.sum(jnp.where(onehot, pad_starts, 0), axis=-1) + rank).reshape(-1)
        n_rows = T * TOP_K + E * bm
        n_blocks = n_rows // bm
        n_used = pad_ends[-1] // bm
        blk = jnp.minimum(jnp.arange(n_blocks, dtype=jnp.int32), n_used - 1) * bm
        block_expert = jnp.minimum(jnp.sum(blk[:, None] >= pad_ends[None, :], axis=1),
                                   E - 1).astype(jnp.int32)

        r = jnp.arange(n_rows + bm, dtype=jnp.int32)
        src = TOP_K * T + (r // bm) % 2 * bm + r % bm
        src = src.at[dest].set(jnp.arange(TOP_K * T, dtype=jnp.int32), unique_indices=True)

        y4 = _moe_experts(T, block_expert, n_used[None], src, h2, w_gu[l].astype(BF16),
                          b_gu[l][:, None], w_down[l].astype(BF16), b_down[l][:, None])
        xc = _finish(x1, wt, norm_final_g[None], y4).reshape(B, S, D)
    return xc
```

```python
import functools
import math

import jax
import jax.numpy as jnp
import numpy as np
from jax import lax
from jax.experimental import pallas as pl
from jax.experimental.pallas import tpu as pltpu

F32 = jnp.float32
BF16 = jnp.bfloat16
HIGHEST = lax.Precision.HIGHEST

ATT_HEADS = 8
ATT_HEAD_DIM = 64
ATT_WIDTH = ATT_HEADS * ATT_HEAD_DIM
MOBA_BLOCK = 256
MOBA_TOPK = 3
REL_BUCKETS = 32
REL_MAX_DIST = 128
CONV_K = 3
XATT_HEADS = 4
XATT_HEAD_DIM = 128
N_BRANCH = 3
N_EXPERTS = 32
TOP_K = 4
SWIGLU_LIMIT = 7.0
SWIGLU_ALPHA = 1.702
EPS = 1e-5
NEG = -1e30
LOG2E = math.log2(math.e)

LANES = 128
SEL_SLOTS = 16
TOKEN_TILE = 256
EXPERT_ROWS = 256
VMEM_LIMIT = 56 * 1024 * 1024


def _rms(x, g):
    return x * lax.rsqrt(jnp.mean(x * x, axis=-1, keepdims=True) + EPS) * g


def _dot(a, b):
    return jnp.dot(a, b, preferred_element_type=F32)


def _dot_nt(a, b, precision=None):
    return lax.dot_general(a, b, (((1,), (1,)), ((), ())), precision=precision,
                           preferred_element_type=F32)


def _sigmoid(x):
    return 1.0 / (1.0 + jnp.exp(-x))


def _mem_kv_kernel(mem_ref, g_ref, wkT_ref, wv_ref, mkT_ref, mv_ref):
    mn = _rms(mem_ref[0], g_ref[...]).astype(BF16)
    mkT_ref[0] = _dot_nt(wkT_ref[...], mn).astype(BF16)
    mv_ref[0] = _dot(mn, wv_ref[...]).astype(BF16)


def _mem_kv(mem, g, wkT, wv):
    B, M, D = mem.shape
    XW = wv.shape[1]
    return pl.pallas_call(
        _mem_kv_kernel,
        grid=(B,),
        in_specs=[pl.BlockSpec((1, M, D), lambda b: (b, 0, 0)),
                  pl.BlockSpec((1, D), lambda b: (0, 0)),
                  pl.BlockSpec((XW, D), lambda b: (0, 0)),
                  pl.BlockSpec((D, XW), lambda b: (0, 0))],
        out_specs=[pl.BlockSpec((1, XW, M), lambda b: (b, 0, 0)),
                   pl.BlockSpec((1, M, XW), lambda b: (b, 0, 0))],
        out_shape=[jax.ShapeDtypeStruct((B, XW, M), BF16),
                   jax.ShapeDtypeStruct((B, M, XW), BF16)],
        name="mem_kv",
    )(mem, g, wkT, wv)


def _qkv_select_kernel(x_ref, g_ref, wk_ref, wqvT_ref, qT_ref, k_ref, vTe_ref, vTo_ref, selbT_ref,
                       km_ref):
    i = pl.program_id(1)
    tq = x_ref.shape[1]
    W = ATT_WIDTH

    @pl.when(i == 0)
    def _():
        km_ref[...] = jnp.zeros_like(km_ref)

    h = _rms(x_ref[0], g_ref[...]).astype(BF16)
    k = _dot(h, wk_ref[...])
    k_ref[0] = k.astype(BF16)
    qvT = _dot_nt(wqvT_ref[...], h)
    qT = qvT[:W] * (1.0 / math.sqrt(ATT_HEAD_DIM))
    qT_ref[0] = (qT * LOG2E).astype(BF16)
    vT = qvT[W:]
    even_head = (lax.broadcasted_iota(jnp.int32, (W, tq), 0) // ATT_HEAD_DIM) % 2 == 0
    vTe_ref[0] = jnp.where(even_head, vT, 1.0).astype(BF16)
    vTo_ref[0] = jnp.where(even_head, 1.0, vT).astype(BF16)

    gate = jnp.dot(km_ref[...], qT, precision=HIGHEST, preferred_element_type=F32)
    g3 = gate.reshape(ATT_HEADS, SEL_SLOTS, tq)
    jj = lax.broadcasted_iota(jnp.int32, g3.shape, 1).astype(F32)
    valid = jj < i.astype(F32)
    g3 = jnp.where(valid, g3, -jnp.inf)
    sel = jnp.zeros(g3.shape, F32)
    for _ in range(MOBA_TOPK):
        m = jnp.max(g3, axis=1, keepdims=True)
        first = jnp.min(jnp.where(g3 == m, jj, float(SEL_SLOTS)), axis=1, keepdims=True)
        pick = jj == first
        sel = jnp.where(pick, 1.0, sel)
        g3 = jnp.where(pick, -jnp.inf, g3)
    keep = ((sel > 0.5) & valid) | (jj == i.astype(F32))
    selbT_ref[0] = jnp.where(keep, 0.0, NEG).reshape(ATT_HEADS * SEL_SLOTS, tq).astype(BF16)

    kmean = jnp.sum(k, axis=0, keepdims=True) * (1.0 / tq)
    lane_head = lax.broadcasted_iota(jnp.int32, (1, W), 1) // ATT_HEAD_DIM
    for hh in range(ATT_HEADS):
        km_ref[pl.ds(hh * SEL_SLOTS + i, 1), :] = jnp.where(lane_head == hh, kmean, 0.0)


def _qkv_select(x, g, wk, wqvT):
    B, S, D = x.shape
    W = ATT_WIDTH
    tq = MOBA_BLOCK
    nb = S // tq
    assert nb <= SEL_SLOTS and ATT_HEADS * SEL_SLOTS == LANES
    by_row = pl.BlockSpec((1, tq, W), lambda b, i: (b, i, 0))
    by_col = pl.BlockSpec((1, W, tq), lambda b, i: (b, 0, i))
    return pl.pallas_call(
        _qkv_select_kernel,
        grid=(B, nb),
        in_specs=[pl.BlockSpec((1, tq, D), lambda b, i: (b, i, 0)),
                  pl.BlockSpec((1, D), lambda b, i: (0, 0)),
                  pl.BlockSpec((D, W), lambda b, i: (0, 0)),
                  pl.BlockSpec((2 * W, D), lambda b, i: (0, 0))],
        out_specs=[by_col, by_row, by_col, by_col,
                   pl.BlockSpec((1, LANES, tq), lambda b, i: (b, 0, i))],
        out_shape=[jax.ShapeDtypeStruct((B, W, S), BF16),
                   jax.ShapeDtypeStruct((B, S, W), BF16),
                   jax.ShapeDtypeStruct((B, W, S), BF16),
                   jax.ShapeDtypeStruct((B, W, S), BF16),
                   jax.ShapeDtypeStruct((B, LANES, S), BF16)],
        scratch_shapes=[pltpu.VMEM((LANES, W), F32)],
        compiler_params=pltpu.CompilerParams(
            dimension_semantics=("arbitrary", "arbitrary"), vmem_limit_bytes=VMEM_LIMIT),
        name="qkv_select",
    )(x, g, wk, wqvT)


def _moba_kernel(qT_ref, selbT_ref, k_ref, vTe_ref, vTo_ref, tbT_ref, eT_ref, o_ref,
                 rhs_ref, m_ref, acc_ref):
    i = pl.program_id(1)
    tq = o_ref.shape[1]
    L = MOBA_BLOCK
    HD = ATT_HEAD_DIM
    H = ATT_HEADS
    selbT = selbT_ref[0].astype(F32)
    row = lax.broadcasted_iota(jnp.int32, (LANES, tq), 0)
    vT_refs = (vTe_ref, vTo_ref)
    slabs = [slice(p * LANES, (p + 1) * LANES) for p in range(H // 2)]

    for h in range(H):
        qTp = qT_ref[0, slabs[h // 2], :].astype(F32)
        own = (row >= HD) if h % 2 else (row < HD)
        rhs_ref[h, :LANES, :] = jnp.where(own, qTp, 0.0).astype(BF16)
        rhs_ref[h, LANES:, :] = jnp.where(row // SEL_SLOTS == h, selbT, 0.0).astype(BF16)
    m_ref[...] = jnp.full(m_ref.shape, -jnp.inf, F32)
    acc_ref[...] = jnp.zeros_like(acc_ref)

    def body(j, carry):
        koff = pl.multiple_of(j * L, L)
        t = jnp.minimum(i - j, 2)
        s = []
        for p in range(H // 2):
            lhs = jnp.concatenate([k_ref[0, pl.ds(koff, L), slabs[p]], eT_ref[j]], axis=1)
            s += [_dot(lhs, rhs_ref[2 * p + hh]) for hh in (0, 1)]
        pr, alpha = [], []
        for h in range(H):
            sh = s[h] + tbT_ref[t, h]
            cm = jnp.max(jnp.max(sh.reshape(8, L // 8, tq), axis=0), axis=0, keepdims=True)
            m_old = m_ref[h:h + 1, :]
            m_new = jnp.maximum(m_old, cm)
            m_ref[h:h + 1, :] = m_new
            pr.append(jnp.exp2(sh - m_new).astype(BF16))
            alpha.append(jnp.exp2(m_old - m_new))
        for h in range(H):
            pv = _dot(vT_refs[h % 2][0, slabs[h // 2], pl.ds(koff, L)], pr[h])
            acc_ref[h] = alpha[h] * acc_ref[h] + pv
        return carry

    lax.fori_loop(0, i + 1, body, 0)
    for p in range(H // 2):
        a0, a1 = acc_ref[2 * p], acc_ref[2 * p + 1]
        oT = jnp.concatenate([a0[:HD] / a0[HD:HD + 1], a1[HD:] / a1[0:1]], axis=0)
        o_ref[0, :, slabs[p]] = oT.T.astype(BF16)


def _moba(qT, selbT, k, vT_even, vT_odd, tbT):
    B, S, W = k.shape
    tq = MOBA_BLOCK
    nb = S // tq
    lane = np.arange(LANES)[None, None, :] % SEL_SLOTS
    eT_all = jnp.asarray(np.broadcast_to(lane == np.arange(SEL_SLOTS)[:, None, None],
                                         (SEL_SLOTS, MOBA_BLOCK, LANES)), BF16)
    whole_T = pl.BlockSpec((1, W, S), lambda b, i: (b, 0, 0))
    return pl.pallas_call(
        _moba_kernel,
        grid=(B, nb),
        in_specs=[pl.BlockSpec((1, W, tq), lambda b, i: (b, 0, i)),
                  pl.BlockSpec((1, LANES, tq), lambda b, i: (b, 0, i)),
                  pl.BlockSpec((1, S, W), lambda b, i: (b, 0, 0)),
                  whole_T, whole_T,
                  pl.BlockSpec(tbT.shape, lambda b, i: (0, 0, 0, 0)),
                  pl.BlockSpec(eT_all.shape, lambda b, i: (0, 0, 0))],
        out_specs=pl.BlockSpec((1, tq, W), lambda b, i: (b, i, 0)),
        out_shape=jax.ShapeDtypeStruct((B, S, W), BF16),
        scratch_shapes=[pltpu.VMEM((ATT_HEADS, 2 * LANES, tq), BF16),
                        pltpu.VMEM((ATT_HEADS, tq), F32),
                        pltpu.VMEM((ATT_HEADS, LANES, tq), F32)],
        compiler_params=pltpu.CompilerParams(
            dimension_semantics=("arbitrary", "arbitrary"), vmem_limit_bytes=VMEM_LIMIT),
        name="moba",
    )(qT, selbT, k, vT_even, vT_odd, tbT, eT_all)


def _merge_kernel(tiles_per_seq,
                  x_ref, ya_ref, mkT_ref, mv_ref, g1_ref, wr_ref, cw_ref, bg_ref,
                  wba_ref, wbc_ref, wbx_ref, wo_ref, g2_ref, wrt_ref, br_ref,
                  x1_ref, h2_ref, idx_ref, rank_ref, wt_ref, cnt_ref, zprev_ref):
    i = pl.program_id(0)
    tm, D = x_ref.shape
    CW = cw_ref.shape[1]
    XW = mv_ref.shape[2]

    @pl.when(i == 0)
    def _():
        cnt_ref[...] = jnp.zeros_like(cnt_ref)

    @pl.when(i % tiles_per_seq == 0)
    def _():
        zprev_ref[...] = jnp.zeros_like(zprev_ref)

    x = x_ref[...]
    h = _rms(x, g1_ref[...]).astype(BF16)
    pr = _dot(h, wr_ref[...])

    cb = pr[:, :CW]
    z = pr[:, CW:2 * CW] * pr[:, 2 * CW:3 * CW]
    row = lax.broadcasted_iota(jnp.int32, (tm, CW), 0)
    zp = zprev_ref[...]
    z1 = jnp.where(row == 0, zp[7:8], pltpu.roll(z, 1, 0))
    z2 = jnp.where(row == 0, zp[6:7], jnp.where(row == 1, zp[7:8], pltpu.roll(z, 2, 0)))
    zprev_ref[...] = z[tm - 8:]
    cw = cw_ref[...]
    y_conv = cb * (cw[0:1] * z2 + cw[1:2] * z1 + cw[2:3] * z)

    o0 = 3 * CW
    scale = 1.0 / math.sqrt(XATT_HEAD_DIM)
    ys = []
    for hh in range(XATT_HEADS):
        hs = slice(hh * XATT_HEAD_DIM, (hh + 1) * XATT_HEAD_DIM)
        qx = pr[:, o0 + hh * XATT_HEAD_DIM:o0 + (hh + 1) * XATT_HEAD_DIM].astype(BF16)
        s = _dot(qx, mkT_ref[0, hs, :]) * scale
        e = jnp.exp(s - jnp.max(s, axis=1, keepdims=True))
        l = jnp.sum(e, axis=1, keepdims=True)
        ys.append(_dot(e.astype(BF16), mv_ref[0, :, hs]) / l)
    y_x = jnp.concatenate(ys, axis=1)

    o1 = o0 + XW
    bg = bg_ref[...]
    merged = (_sigmoid(pr[:, o1:o1 + D] + bg[0:1]) * _dot(ya_ref[...], wba_ref[...])
              + _sigmoid(pr[:, o1 + D:o1 + 2 * D] + bg[1:2]) * _dot(y_conv.astype(BF16), wbc_ref[...])
              + _sigmoid(pr[:, o1 + 2 * D:o1 + 3 * D] + bg[2:3]) * _dot(y_x.astype(BF16), wbx_ref[...]))
    x1 = x + _dot(merged.astype(BF16), wo_ref[...])
    x1_ref[...] = x1
    h2 = _rms(x1, g2_ref[...])
    for c in range(D // LANES):
        h2_ref[pl.ds(c, tm, stride=D // LANES), :] = h2[:, c * LANES:(c + 1) * LANES]

    lg = _dot_nt(wrt_ref[...], h2, precision=HIGHEST) + br_ref[...]
    E = lg.shape[0]
    ee = lax.broadcasted_iota(jnp.int32, (E, tm), 0).astype(F32)
    work = lg
    member = jnp.zeros((E, tm), F32)
    picks, vals = [], []
    for _ in range(TOP_K):
        m = jnp.max(work, axis=0, keepdims=True)
        first = jnp.min(jnp.where(work == m, ee, float(E)), axis=0, keepdims=True)
        pick = ee == first
        work = jnp.where(pick, -jnp.inf, work)
        member = jnp.where(pick, 1.0, member)
        picks.append((pick, first))
        vals.append(m)
    exps = [jnp.exp(v - vals[0]) for v in vals]
    denom = exps[0] + exps[1] + exps[2] + exps[3]

    r_i = lax.broadcasted_iota(jnp.int32, (tm, tm), 0)
    c_i = lax.broadcasted_iota(jnp.int32, (tm, tm), 1)
    upper = jnp.where(r_i < c_i, 1.0, 0.0).astype(BF16)
    before = _dot(member.astype(BF16), upper) + cnt_ref[:, 0:1]
    cnt_ref[...] = cnt_ref[...] + jnp.sum(member, axis=1, keepdims=True)

    wrow = lax.broadcasted_iota(jnp.int32, (LANES, tm), 0)
    wpad = jnp.zeros((LANES, tm), F32)
    for kk in range(TOP_K):
        pick, first = picks[kk]
        idx_ref[kk:kk + 1, :] = first.astype(jnp.int32)
        rank_ref[kk:kk + 1, :] = jnp.sum(jnp.where(pick, before, 0.0), axis=0,
                                         keepdims=True).astype(jnp.int32)
        wpad = jnp.where(wrow == kk, exps[kk] / denom, wpad)
    wt_ref[...] = wpad.T


def _merge(x2, yatt2, mkT, mv, g1, w_rest, conv_w, b_gate, wba, wbc, wbx, wo, g2, wrt, br, S):
    T, D = x2.shape
    tm = TOKEN_TILE
    nt = T // tm
    tps = S // tm
    CW = conv_w.shape[1]
    XW, M = mkT.shape[1], mkT.shape[2]
    E = wrt.shape[0]
    const = lambda shape: pl.BlockSpec(shape, lambda i: (0,) * len(shape))
    return pl.pallas_call(
        functools.partial(_merge_kernel, tps),
        grid=(nt,),
        in_specs=[pl.BlockSpec((tm, D), lambda i: (i, 0)),
                  pl.BlockSpec((tm, yatt2.shape[1]), lambda i: (i, 0)),
                  pl.BlockSpec((1, XW, M), lambda i: (i // tps, 0, 0)),
                  pl.BlockSpec((1, M, XW), lambda i: (i // tps, 0, 0)),
                  const((1, D)), const(w_rest.shape), const(conv_w.shape), const(b_gate.shape),
                  const(wba.shape), const(wbc.shape), const(wbx.shape), const(wo.shape),
                  const((1, D)), const(wrt.shape), const(br.shape)],
        out_specs=[pl.BlockSpec((tm, D), lambda i: (i, 0)),
                   pl.BlockSpec((tm * (D // LANES), LANES), lambda i: (i, 0)),
                   pl.BlockSpec((TOP_K, tm), lambda i: (0, i)),
                   pl.BlockSpec((TOP_K, tm), lambda i: (0, i)),
                   pl.BlockSpec((tm, LANES), lambda i: (i, 0)),
                   pl.BlockSpec((E, LANES), lambda i: (0, 0))],
        out_shape=[jax.ShapeDtypeStruct((T, D), F32),
                   jax.ShapeDtypeStruct((T * (D // LANES), LANES), F32),
                   jax.ShapeDtypeStruct((TOP_K, T), jnp.int32),
                   jax.ShapeDtypeStruct((TOP_K, T), jnp.int32),
                   jax.ShapeDtypeStruct((T, LANES), F32),
                   jax.ShapeDtypeStruct((E, LANES), F32)],
        scratch_shapes=[pltpu.VMEM((8, CW), F32)],
        compiler_params=pltpu.CompilerParams(
            dimension_semantics=("arbitrary",), vmem_limit_bytes=VMEM_LIMIT),
        name="merge_route",
    )(x2, yatt2, mkT, mv, g1, w_rest, conv_w, b_gate, wba, wbc, wbx, wo, g2, wrt, br)


def _dispatch_kernel(T, dest_ref, pe_ref, h2_ref, xs_ref, zero_ref, sem, zsem):
    i = pl.program_id(0)
    tm = h2_ref.shape[0]
    bm = zero_ref.shape[0]

    @pl.when(i == 0)
    def _():
        zero_ref[...] = jnp.zeros_like(zero_ref)

        def last_block(e):
            end = pe_ref[e]
            start = pl.multiple_of(jnp.maximum(end - bm, 0), bm)
            has_rows = end > (pe_ref[e - 1] if e else 0)
            return has_rows, pltpu.make_async_copy(zero_ref, xs_ref.at[pl.ds(start, bm)], zsem)

        def tail_block(b):
            start = pe_ref[N_EXPERTS - 1] + b * bm
            in_range = start < xs_ref.shape[0]
            start = pl.multiple_of(jnp.minimum(start, xs_ref.shape[0] - bm), bm)
            return in_range, pltpu.make_async_copy(zero_ref, xs_ref.at[pl.ds(start, bm)], zsem)

        fills = [last_block(e) for e in range(N_EXPERTS)] + [tail_block(b) for b in range(N_EXPERTS)]
        for cond, cp in fills:
            pl.when(cond)(cp.start)
        for cond, cp in fills:
            pl.when(cond)(cp.wait)

    def row_copy(t, d):
        return pltpu.make_async_copy(h2_ref.at[pl.ds(t, 1)], xs_ref.at[pl.ds(d, 1)], sem)

    def issue(t, c):
        for kk in range(TOP_K):
            row_copy(t, dest_ref[kk * T + i * tm + t]).start()
        return c

    lax.fori_loop(0, tm, issue, 0)

    def drain(t, c):
        for kk in range(TOP_K):
            row_copy(t, dest_ref[kk * T + i * tm + t]).wait()
        return c

    lax.fori_loop(0, tm, drain, 0)


def _dispatch(dest_flat, pad_ends, h2, n_rows):
    T, D = h2.shape
    tm = TOKEN_TILE
    return pl.pallas_call(
        functools.partial(_dispatch_kernel, T),
        grid_spec=pltpu.PrefetchScalarGridSpec(
            num_scalar_prefetch=2,
            grid=(T // tm,),
            in_specs=[pl.BlockSpec((tm, D), lambda i, d, pe: (i, 0))],
            out_specs=pl.BlockSpec(memory_space=pl.ANY),
            scratch_shapes=[pltpu.VMEM((EXPERT_ROWS, D), F32),
                            pltpu.SemaphoreType.DMA(()),
                            pltpu.SemaphoreType.DMA(())]),
        out_shape=jax.ShapeDtypeStruct((n_rows, D), F32),
        compiler_params=pltpu.CompilerParams(dimension_semantics=("arbitrary",)),
        name="dispatch",
    )(dest_flat, pad_ends, h2)


def _expert_kernel(be_ref, nu_ref, xs_ref, wgu_ref, bgu_ref, wd_ref, bd_ref, ys_ref):
    b = pl.program_id(0)
    F = wd_ref.shape[1]

    @pl.when(b < nu_ref[0])
    def _():
        gu = _dot(xs_ref[...].astype(BF16), wgu_ref[0]) + bgu_ref[0]
        g = jnp.minimum(gu[:, :F], SWIGLU_LIMIT)
        lin = jnp.clip(gu[:, F:], -SWIGLU_LIMIT, SWIGLU_LIMIT)
        act = (lin + 1.0) * (g * _sigmoid(SWIGLU_ALPHA * g))
        ys_ref[...] = _dot(act.astype(BF16), wd_ref[0]) + bd_ref[0]

    @pl.when(b >= nu_ref[0])
    def _():
        ys_ref[...] = jnp.zeros_like(ys_ref)


def _experts(block_expert, n_used, xs, wgu, bgu, wd, bd):
    NR, D = xs.shape
    bm = EXPERT_ROWS
    E, _, F2 = wgu.shape
    F = wd.shape[1]
    row_blk = lambda b, be, nu: (jnp.minimum(b, nu[0] - 1), 0)
    per_e = lambda b, be, nu: (be[b], 0, 0)
    return pl.pallas_call(
        _expert_kernel,
        grid_spec=pltpu.PrefetchScalarGridSpec(
            num_scalar_prefetch=2,
            grid=(NR // bm,),
            in_specs=[pl.BlockSpec((bm, D), row_blk),
                      pl.BlockSpec((1, D, F2), per_e),
                      pl.BlockSpec((1, 1, F2), per_e),
                      pl.BlockSpec((1, F, D), per_e),
                      pl.BlockSpec((1, 1, D), per_e)],
            out_specs=pl.BlockSpec((bm, D), lambda b, be, nu: (b, 0))),
        out_shape=jax.ShapeDtypeStruct((NR, D), F32),
        compiler_params=pltpu.CompilerParams(
            dimension_semantics=("arbitrary",), vmem_limit_bytes=VMEM_LIMIT),
        name="experts",
    )(block_expert, n_used, xs, wgu, bgu, wd, bd)


def _combine_kernel(T, dest_ref, x1_ref, wt_ref, g_ref, ys_ref, o_ref, buf_ref, sem):
    i = pl.program_id(0)
    n = pl.num_programs(0)
    tm = x1_ref.shape[0]

    def row_copy(step, slot, kk, t):
        d = dest_ref[kk * T + step * tm + t]
        return pltpu.make_async_copy(ys_ref.at[pl.ds(d, 1)], buf_ref.at[slot, kk, pl.ds(t, 1)],
                                     sem.at[slot])

    def issue(step, slot):
        def f(t, c):
            for kk in range(TOP_K):
                row_copy(step, slot, kk, t).start()
            return c
        lax.fori_loop(0, tm, f, 0)

    @pl.when(i == 0)
    def _():
        issue(0, 0)

    @pl.when(i + 1 < n)
    def _():
        issue(i + 1, (i + 1) % 2)

    slot = i % 2

    def drain(t, c):
        for kk in range(TOP_K):
            row_copy(i, slot, kk, t).wait()
        return c

    lax.fori_loop(0, tm, drain, 0)

    wt = wt_ref[...]
    y = x1_ref[...]
    for kk in range(TOP_K):
        y = y + wt[:, kk:kk + 1] * buf_ref[slot, kk]
    o_ref[...] = _rms(y, g_ref[...])


def _combine(dest_flat, x1, wt, g, ys):
    T, D = x1.shape
    tm = TOKEN_TILE
    return pl.pallas_call(
        functools.partial(_combine_kernel, T),
        grid_spec=pltpu.PrefetchScalarGridSpec(
            num_scalar_prefetch=1,
            grid=(T // tm,),
            in_specs=[pl.BlockSpec((tm, D), lambda i, d: (i, 0)),
                      pl.BlockSpec((tm, LANES), lambda i, d: (i, 0)),
                      pl.BlockSpec((1, D), lambda i, d: (0, 0)),
                      pl.BlockSpec(memory_space=pl.ANY)],
            out_specs=pl.BlockSpec((tm, D), lambda i, d: (i, 0)),
            scratch_shapes=[pltpu.VMEM((2, TOP_K, tm, D), F32),
                            pltpu.SemaphoreType.DMA((2,))]),
        out_shape=jax.ShapeDtypeStruct((T, D), F32),
        compiler_params=pltpu.CompilerParams(
            dimension_semantics=("arbitrary",), vmem_limit_bytes=VMEM_LIMIT),
        name="combine",
    )(dest_flat, x1, wt, g, ys)


def _invert_kernel(dest_ref, cnt_ref, pe_ref, src_ref):
    bm = EXPERT_ROWS
    n_assign = dest_ref.shape[0]

    assert bm & (bm - 1) == 0
    shift = bm.bit_length() - 1

    def pad_id(r):
        return n_assign + ((r >> shift) & 1) * bm + (r & (bm - 1))

    def put_pad(r, c):
        src_ref[r] = pad_id(r)
        return c

    for e in range(N_EXPERTS):
        first_row = (pe_ref[e - 1] if e else 0) + cnt_ref[e]
        lax.fori_loop(first_row, pe_ref[e], put_pad, 0)

    def put_pad_block(blk, c):
        for r in range(bm):
            src_ref[blk * bm + r] = n_assign + (blk % 2) * bm + r
        return c

    lax.fori_loop(pe_ref[N_EXPERTS - 1] // bm, src_ref.shape[0] // bm, put_pad_block, 0)

    def put(g, c):
        for u in range(8):
            a = g * 8 + u
            src_ref[dest_ref[a]] = a
        return c

    lax.fori_loop(0, n_assign // 8, put, 0)


def _invert(dest_flat, counts, pad_ends, n_src):
    return pl.pallas_call(
        _invert_kernel,
        grid_spec=pltpu.PrefetchScalarGridSpec(
            num_scalar_prefetch=3,
            grid=(1,),
            in_specs=[],
            out_specs=pl.BlockSpec(memory_space=pltpu.SMEM)),
        out_shape=jax.ShapeDtypeStruct((n_src,), jnp.int32),
        name="invert_routing",
    )(dest_flat, counts, pad_ends)


def _moe_kernel(T, be_ref, nu_ref, src_ref, h2t_ref, wgu_ref, bgu_ref, wd_ref, bd_ref, y4_ref,
                xbuf0, xbuf1, ybuf0, ybuf1, gsem, ssem):
    b = pl.program_id(0)
    n_used = nu_ref[0]
    bm = EXPERT_ROWS
    F = wd_ref.shape[1]
    CH = wgu_ref.shape[1] // LANES
    xbuf = (xbuf0, xbuf1)
    ybuf = (ybuf0, ybuf1)

    def gather_start(blk, slot):
        for r in range(bm):
            tok = src_ref[blk * bm + r] & (T - 1)
            pltpu.make_async_copy(h2t_ref.at[pl.ds(pl.multiple_of(tok * CH, CH), CH)],
                                  xbuf[slot].at[pl.ds(r * CH, CH)], gsem.at[slot]).start()

    def scatter_start(blk, slot):
        for r in range(bm):
            a = src_ref[blk * bm + r]
            pltpu.make_async_copy(ybuf[slot].at[pl.ds(r * CH, CH)],
                                  y4_ref.at[pl.ds(pl.multiple_of(a * CH, CH), CH)],
                                  ssem.at[slot]).start()

    def gather_wait(slot):
        pltpu.make_async_copy(h2t_ref.at[pl.ds(0, bm * CH)], xbuf[slot], gsem.at[slot]).wait()

    def scatter_wait(slot):
        pltpu.make_async_copy(ybuf[slot], y4_ref.at[pl.ds(0, bm * CH)], ssem.at[slot]).wait()

    def load_rows(slot):
        x = jnp.concatenate([xbuf[slot][pl.ds(c, bm, stride=CH), :] for c in range(CH)], axis=1)
        return x.astype(BF16)

    def ffn(slot, x):
        gu = _dot(x, wgu_ref[0]) + bgu_ref[0]
        g = jnp.minimum(gu[:, :F], SWIGLU_LIMIT)
        lin = jnp.clip(gu[:, F:], -SWIGLU_LIMIT, SWIGLU_LIMIT)
        act = (lin + 1.0) * (g * _sigmoid(SWIGLU_ALPHA * g))
        y = _dot(act.astype(BF16), wd_ref[0]) + bd_ref[0]
        for c in range(CH):
            ybuf[slot][pl.ds(c, bm, stride=CH), :] = y[:, c * LANES:(c + 1) * LANES]

    @pl.when(b == 0)
    def _():
        gather_start(0, 0)
        gather_wait(0)
        gather_start(1, 1)
        ffn(0, load_rows(0))

    for slot in (0, 1):
        @pl.when((b > 0) & (b < n_used) & (b % 2 == slot))
        def _():
            gather_start(b + 1, 1 - slot)
            scatter_start(b - 1, 1 - slot)
            gather_wait(slot)

            @pl.when(b >= 2)
            def _():
                scatter_wait(slot)

            ffn(slot, load_rows(slot))

        @pl.when((b == n_used - 1) & (b % 2 == slot))
        def _():
            scatter_start(b, slot)
            gather_wait(1 - slot)

            @pl.when(b >= 1)
            def _():
                scatter_wait(1 - slot)

            scatter_wait(slot)


def _moe_experts(T, block_expert, n_used, src, h2t, wgu, bgu, wd, bd):
    bm = EXPERT_ROWS
    E, D, F2 = wgu.shape
    F = wd.shape[1]
    CH = D // LANES
    n_blocks = block_expert.shape[0]
    assert T & (T - 1) == 0 and src.shape[0] == (n_blocks + 1) * bm
    per_e = lambda b, be, nu, s: (be[b], 0, 0)
    return pl.pallas_call(
        functools.partial(_moe_kernel, T),
        grid_spec=pltpu.PrefetchScalarGridSpec(
            num_scalar_prefetch=3,
            grid=(n_blocks,),
            in_specs=[pl.BlockSpec(memory_space=pl.ANY),
                      pl.BlockSpec((1, D, F2), per_e),
                      pl.BlockSpec((1, 1, F2), per_e),
                      pl.BlockSpec((1, F, D), per_e),
                      pl.BlockSpec((1, 1, D), per_e)],
            out_specs=pl.BlockSpec(memory_space=pl.ANY),
            scratch_shapes=[pltpu.VMEM((bm * CH, LANES), F32)] * 4 + [
                            pltpu.SemaphoreType.DMA((2,)),
                            pltpu.SemaphoreType.DMA((2,))]),
        out_shape=jax.ShapeDtypeStruct(((TOP_K * T + 2 * bm) * CH, LANES), F32),
        compiler_params=pltpu.CompilerParams(
            dimension_semantics=("arbitrary",), vmem_limit_bytes=VMEM_LIMIT),
        name="moe_experts",
    )(block_expert, n_used, src, h2t, wgu, bgu, wd, bd)


def _finish_kernel(x1_ref, wt_ref, g_ref, y0_ref, y1_ref, y2_ref, y3_ref, o_ref):
    tm, D = x1_ref.shape
    CH = D // LANES
    wt = wt_ref[...]
    y = x1_ref[...]
    for kk, yk_ref in enumerate((y0_ref, y1_ref, y2_ref, y3_ref)):
        yk = jnp.concatenate([yk_ref[pl.ds(c, tm, stride=CH), :] for c in range(CH)], axis=1)
        y = y + wt[:, kk:kk + 1] * yk
    o_ref[...] = _rms(y, g_ref[...])


def _finish(x1, wt, g, y4):
    T, D = x1.shape
    tm = TOKEN_TILE
    CH = D // LANES
    nt = T // tm
    choice = lambda kk: pl.BlockSpec((tm * CH, LANES), lambda i: (kk * nt + i, 0))
    return pl.pallas_call(
        _finish_kernel,
        grid=(nt,),
        in_specs=[pl.BlockSpec((tm, D), lambda i: (i, 0)),
                  pl.BlockSpec((tm, LANES), lambda i: (i, 0)),
                  pl.BlockSpec((1, D), lambda i: (0, 0)),
                  choice(0), choice(1), choice(2), choice(3)],
        out_specs=pl.BlockSpec((tm, D), lambda i: (i, 0)),
        out_shape=jax.ShapeDtypeStruct((T, D), F32),
        compiler_params=pltpu.CompilerParams(
            dimension_semantics=("arbitrary",), vmem_limit_bytes=VMEM_LIMIT),
        name="finish",
    )(x1, wt, g, y4, y4, y4, y4)


def _t5_bucket(dist):
    n = jnp.maximum(dist, 0)
    max_exact = REL_BUCKETS // 2
    nf = jnp.maximum(n, 1).astype(F32)
    large = max_exact + (jnp.log(nf / max_exact) / math.log(REL_MAX_DIST / max_exact)
                         * (REL_BUCKETS - max_exact)).astype(jnp.int32)
    large = jnp.minimum(large, REL_BUCKETS - 1)
    return jnp.where(n < max_exact, n, large)


def _bias_tiles(rel_bias):
    L = MOBA_BLOCK
    assert REL_MAX_DIST <= L
    d = np.arange(-L, 3 * L)
    onehot = (_t5_bucket(jnp.asarray(d))[:, None] == jnp.arange(REL_BUCKETS)).astype(F32)
    by_dist = jnp.dot(onehot, rel_bias.astype(F32), precision=HIGHEST)
    by_dist = jnp.where((d >= 0)[:, None], by_dist, NEG).T
    H = by_dist.shape[0]
    tiles = []
    for t in range(3):
        v = jnp.concatenate([by_dist[:, (t + 1) * L:(t + 2) * L], by_dist[:, t * L:(t + 1) * L]],
                            axis=1)
        flat = jnp.tile(v, (1, L))[:, :L * (2 * L - 1)]
        tiles.append(flat.reshape(H, L, 2 * L - 1)[:, :, :L])
    return jnp.stack(tiles) * LOG2E


def kernel(x, mem, rel_bias, norm_mix_g, w_in, b_gate, conv_w, norm_mem_g, w_mem_kv, w_br_att,
           w_br_conv, w_br_xatt, w_out, norm_ffn_g, w_router, b_router, w_gu, b_gu, w_down,
           b_down, norm_final_g):
    B, S, D = x.shape
    T = B * S
    depth = w_in.shape[0]
    assert depth == 1, "the combine step applies the final norm: single-layer configuration only"
    W = ATT_WIDTH
    XW = XATT_HEADS * XATT_HEAD_DIM
    E = w_router.shape[2]
    bm = EXPERT_ROWS
    tb = _bias_tiles(rel_bias)

    xc = x
    for l in range(depth):
        w_l = w_in[l]
        wk = w_l[:, W:2 * W].astype(BF16)
        wqvT = jnp.concatenate([w_l[:, :W], w_l[:, 2 * W:3 * W]], axis=1).T.astype(BF16)
        w_rest = w_l[:, 3 * W:].astype(BF16)
        wm = w_mem_kv[l]

        mkT, mv = _mem_kv(mem, norm_mem_g[l][None], wm[:, :XW].T.astype(BF16),
                          wm[:, XW:].astype(BF16))
        qT, k, vT_even, vT_odd, selbT = _qkv_select(xc, norm_mix_g[l][None], wk, wqvT)
        y_att = _moba(qT, selbT, k, vT_even, vT_odd, tb)

        x1, h2, idx, rank, wt, cnt = _merge(
            xc.reshape(T, D), y_att.reshape(T, W), mkT, mv, norm_mix_g[l][None], w_rest,
            conv_w[l], b_gate[l], w_br_att[l].astype(BF16), w_br_conv[l].astype(BF16),
            w_br_xatt[l].astype(BF16), w_out[l].astype(BF16), norm_ffn_g[l][None],
            w_router[l].T, b_router[l][:, None], S)

        counts = cnt[:, 0].astype(jnp.int32)
        padded = (counts + bm - 1) // bm * bm
        pad_ends = jnp.cumsum(padded).astype(jnp.int32)
        pad_starts = pad_ends - padded
        onehot = idx[..., None] == jnp.arange(E, dtype=jnp.int32)
        dest = (jnp.sum(jnp.where(onehot, pad_starts, 0), axis=-1) + rank).reshape(-1)
        n_rows = T * TOP_K + E * bm
        n_blocks = n_rows // bm
        n_used = pad_ends[-1] // bm
        blk = jnp.minimum(jnp.arange(n_blocks, dtype=jnp.int32), n_used - 1) * bm
        block_expert = jnp.minimum(jnp.sum(blk[:, None] >= pad_ends[None, :], axis=1),
                                   E - 1).astype(jnp.int32)

        src = _invert(dest, counts, pad_ends, n_rows + bm)

        y4 = _moe_experts(T, block_expert, n_used[None], src, h2, w_gu[l].astype(BF16),
                          b_gu[l][:, None], w_down[l].astype(BF16), b_down[l][:, None])
        xc = _finish(x1, wt, norm_final_g[None], y4).reshape(B, S, D)
    return xc
```

```python
import functools
import math

import jax
import jax.numpy as jnp
import numpy as np
from jax import lax
from jax.experimental import pallas as pl
from jax.experimental.pallas import tpu as pltpu

F32 = jnp.float32
BF16 = jnp.bfloat16
HIGHEST = lax.Precision.HIGHEST

ATT_HEADS = 8
ATT_HEAD_DIM = 64
ATT_WIDTH = ATT_HEADS * ATT_HEAD_DIM
MOBA_BLOCK = 256
MOBA_TOPK = 3
REL_BUCKETS = 32
REL_MAX_DIST = 128
CONV_K = 3
XATT_HEADS = 4
XATT_HEAD_DIM = 128
N_BRANCH = 3
N_EXPERTS = 32
TOP_K = 4
SWIGLU_LIMIT = 7.0
SWIGLU_ALPHA = 1.702
EPS = 1e-5
NEG = -1e30
LOG2E = math.log2(math.e)

LANES = 128
SEL_SLOTS = 16
TOKEN_TILE = 256
EXPERT_ROWS = 256
VMEM_LIMIT = 56 * 1024 * 1024


def _rms(x, g):
    return x * lax.rsqrt(jnp.mean(x * x, axis=-1, keepdims=True) + EPS) * g


def _dot(a, b):
    return jnp.dot(a, b, preferred_element_type=F32)


def _dot_nt(a, b, precision=None):
    return lax.dot_general(a, b, (((1,), (1,)), ((), ())), precision=precision,
                           preferred_element_type=F32)


def _sigmoid(x):
    return 1.0 / (1.0 + jnp.exp(-x))


def _split_bf16(x):
    hi = x.astype(BF16)
    return hi, (x - hi.astype(F32)).astype(BF16)


def _dot3(a, b, dot):
    m = a.shape[0]
    a_hi, a_lo = _split_bf16(a)
    b_hi, b_lo = _split_bf16(b)
    both = dot(jnp.concatenate([a_hi, a_lo], axis=0), b_hi)
    return both[:m] + both[m:] + dot(a_hi, b_lo)


def _mem_kv_kernel(mem_ref, g_ref, wkT_ref, wv_ref, mkT_ref, mv_ref):
    mn = _rms(mem_ref[0], g_ref[...]).astype(BF16)
    mkT_ref[0] = _dot_nt(wkT_ref[...], mn).astype(BF16)
    mv_ref[0] = _dot(mn, wv_ref[...]).astype(BF16)


def _mem_kv(mem, g, wkT, wv):
    B, M, D = mem.shape
    XW = wv.shape[1]
    return pl.pallas_call(
        _mem_kv_kernel,
        grid=(B,),
        in_specs=[pl.BlockSpec((1, M, D), lambda b: (b, 0, 0)),
                  pl.BlockSpec((1, D), lambda b: (0, 0)),
                  pl.BlockSpec((XW, D), lambda b: (0, 0)),
                  pl.BlockSpec((D, XW), lambda b: (0, 0))],
        out_specs=[pl.BlockSpec((1, XW, M), lambda b: (b, 0, 0)),
                   pl.BlockSpec((1, M, XW), lambda b: (b, 0, 0))],
        out_shape=[jax.ShapeDtypeStruct((B, XW, M), BF16),
                   jax.ShapeDtypeStruct((B, M, XW), BF16)],
        name="mem_kv",
    )(mem, g, wkT, wv)


def _qkv_select_kernel(x_ref, g_ref, wk_ref, wqvT_ref, qT_ref, k_ref, vTe_ref, vTo_ref, selbT_ref,
                       km_ref):
    i = pl.program_id(1)
    tq = x_ref.shape[1]
    W = ATT_WIDTH

    @pl.when(i == 0)
    def _():
        km_ref[...] = jnp.zeros_like(km_ref)

    h = _rms(x_ref[0], g_ref[...]).astype(BF16)
    k = _dot(h, wk_ref[...])
    k_ref[0] = k.astype(BF16)
    qvT = _dot_nt(wqvT_ref[...], h)
    qT = qvT[:W] * (1.0 / math.sqrt(ATT_HEAD_DIM))
    qT_ref[0] = (qT * LOG2E).astype(BF16)
    vT = qvT[W:]
    even_head = (lax.broadcasted_iota(jnp.int32, (W, tq), 0) // ATT_HEAD_DIM) % 2 == 0
    vTe_ref[0] = jnp.where(even_head, vT, 1.0).astype(BF16)
    vTo_ref[0] = jnp.where(even_head, 1.0, vT).astype(BF16)

    gate = _dot3(km_ref[...], qT, _dot)
    g3 = gate.reshape(ATT_HEADS, SEL_SLOTS, tq)
    jj = lax.broadcasted_iota(jnp.int32, g3.shape, 1).astype(F32)
    valid = jj < i.astype(F32)
    g3 = jnp.where(valid, g3, -jnp.inf)
    sel = jnp.zeros(g3.shape, F32)
    for _ in range(MOBA_TOPK):
        m = jnp.max(g3, axis=1, keepdims=True)
        first = jnp.min(jnp.where(g3 == m, jj, float(SEL_SLOTS)), axis=1, keepdims=True)
        pick = jj == first
        sel = jnp.where(pick, 1.0, sel)
        g3 = jnp.where(pick, -jnp.inf, g3)
    keep = ((sel > 0.5) & valid) | (jj == i.astype(F32))
    selbT_ref[0] = jnp.where(keep, 0.0, NEG).reshape(ATT_HEADS * SEL_SLOTS, tq).astype(BF16)

    kmean = jnp.sum(k, axis=0, keepdims=True) * (1.0 / tq)
    lane_head = lax.broadcasted_iota(jnp.int32, (1, W), 1) // ATT_HEAD_DIM
    for hh in range(ATT_HEADS):
        km_ref[pl.ds(hh * SEL_SLOTS + i, 1), :] = jnp.where(lane_head == hh, kmean, 0.0)


def _qkv_select(x, g, wk, wqvT):
    B, S, D = x.shape
    W = ATT_WIDTH
    tq = MOBA_BLOCK
    nb = S // tq
    assert nb <= SEL_SLOTS and ATT_HEADS * SEL_SLOTS == LANES
    by_row = pl.BlockSpec((1, tq, W), lambda b, i: (b, i, 0))
    by_col = pl.BlockSpec((1, W, tq), lambda b, i: (b, 0, i))
    return pl.pallas_call(
        _qkv_select_kernel,
        grid=(B, nb),
        in_specs=[pl.BlockSpec((1, tq, D), lambda b, i: (b, i, 0)),
                  pl.BlockSpec((1, D), lambda b, i: (0, 0)),
                  pl.BlockSpec((D, W), lambda b, i: (0, 0)),
                  pl.BlockSpec((2 * W, D), lambda b, i: (0, 0))],
        out_specs=[by_col, by_row, by_col, by_col,
                   pl.BlockSpec((1, LANES, tq), lambda b, i: (b, 0, i))],
        out_shape=[jax.ShapeDtypeStruct((B, W, S), BF16),
                   jax.ShapeDtypeStruct((B, S, W), BF16),
                   jax.ShapeDtypeStruct((B, W, S), BF16),
                   jax.ShapeDtypeStruct((B, W, S), BF16),
                   jax.ShapeDtypeStruct((B, LANES, S), BF16)],
        scratch_shapes=[pltpu.VMEM((LANES, W), F32)],
        compiler_params=pltpu.CompilerParams(
            dimension_semantics=("arbitrary", "arbitrary"), vmem_limit_bytes=VMEM_LIMIT),
        name="qkv_select",
    )(x, g, wk, wqvT)


def _moba_kernel(qT_ref, selbT_ref, k_ref, vTe_ref, vTo_ref, tbT_ref, cfar_ref, eT_ref, o_ref,
                 rhs_ref, m_ref, acc_ref):
    i = pl.program_id(1)
    tq = o_ref.shape[1]
    L = MOBA_BLOCK
    HD = ATT_HEAD_DIM
    H = ATT_HEADS
    selbT = selbT_ref[0].astype(F32)
    row = lax.broadcasted_iota(jnp.int32, (LANES, tq), 0)
    vT_refs = (vTe_ref, vTo_ref)
    slabs = [slice(p * LANES, (p + 1) * LANES) for p in range(H // 2)]

    for h in range(H):
        qTp = qT_ref[0, slabs[h // 2], :].astype(F32)
        own = (row >= HD) if h % 2 else (row < HD)
        rhs_ref[h, :LANES, :] = jnp.where(own, qTp, 0.0).astype(BF16)
        rhs_ref[h, LANES:, :] = jnp.where(row // SEL_SLOTS == h, selbT, 0.0).astype(BF16)
    m_ref[...] = jnp.full(m_ref.shape, -jnp.inf, F32)
    acc_ref[...] = jnp.zeros_like(acc_ref)

    def attend(js, tiles):
        koffs = [pl.multiple_of(j * L, L) for j in js]
        s = []
        for j, koff in zip(js, koffs):
            sj = []
            for p in range(H // 2):
                lhs = jnp.concatenate([k_ref[0, pl.ds(koff, L), slabs[p]], eT_ref[j]], axis=1)
                sj += [_dot(lhs, rhs_ref[2 * p + hh]) for hh in (0, 1)]
            s.append(sj)
        for sj, koff, tile in zip(s, koffs, tiles):
            pr, alpha = [], []
            for h in range(H):
                sh = sj[h] if tile is None else sj[h] + tbT_ref[tile, h]
                cm = jnp.max(jnp.max(sh.reshape(8, L // 8, tq), axis=0), axis=0, keepdims=True)
                m_old = m_ref[h:h + 1, :]
                if tile is None:
                    m_new = jnp.maximum(m_old, cm + cfar_ref[h])
                    pr.append(jnp.exp2(sh - (m_new - cfar_ref[h])).astype(BF16))
                else:
                    m_new = jnp.maximum(m_old, cm)
                    pr.append(jnp.exp2(sh - m_new).astype(BF16))
                m_ref[h:h + 1, :] = m_new
                alpha.append(jnp.exp2(m_old - m_new))
            for h in range(H):
                pv = _dot(vT_refs[h % 2][0, slabs[h // 2], pl.ds(koff, L)], pr[h])
                acc_ref[h] = alpha[h] * acc_ref[h] + pv

    n_far = jnp.maximum(i - 1, 0)

    def far_pair(jj, carry):
        attend([2 * jj, 2 * jj + 1], [None, None])
        return carry

    lax.fori_loop(0, n_far // 2, far_pair, 0)

    @pl.when(n_far % 2 == 1)
    def _():
        attend([n_far - 1], [None])

    @pl.when(i >= 1)
    def _():
        attend([i - 1, i], [1, 0])

    @pl.when(i == 0)
    def _():
        attend([i], [0])
    for p in range(H // 2):
        a0, a1 = acc_ref[2 * p], acc_ref[2 * p + 1]
        oT = jnp.concatenate([a0[:HD] / a0[HD:HD + 1], a1[HD:] / a1[0:1]], axis=0)
        o_ref[0, :, slabs[p]] = oT.T.astype(BF16)


def _moba(qT, selbT, k, vT_even, vT_odd, tbT, cfar):
    B, S, W = k.shape
    tq = MOBA_BLOCK
    nb = S // tq
    lane = np.arange(LANES)[None, None, :] % SEL_SLOTS
    eT_all = jnp.asarray(np.broadcast_to(lane == np.arange(SEL_SLOTS)[:, None, None],
                                         (SEL_SLOTS, MOBA_BLOCK, LANES)), BF16)
    whole_T = pl.BlockSpec((1, W, S), lambda b, i: (b, 0, 0))
    return pl.pallas_call(
        _moba_kernel,
        grid=(B, nb),
        in_specs=[pl.BlockSpec((1, W, tq), lambda b, i: (b, 0, i)),
                  pl.BlockSpec((1, LANES, tq), lambda b, i: (b, 0, i)),
                  pl.BlockSpec((1, S, W), lambda b, i: (b, 0, 0)),
                  whole_T, whole_T,
                  pl.BlockSpec(tbT.shape, lambda b, i: (0, 0, 0, 0)),
                  pl.BlockSpec(memory_space=pltpu.SMEM),
                  pl.BlockSpec(eT_all.shape, lambda b, i: (0, 0, 0))],
        out_specs=pl.BlockSpec((1, tq, W), lambda b, i: (b, i, 0)),
        out_shape=jax.ShapeDtypeStruct((B, S, W), BF16),
        scratch_shapes=[pltpu.VMEM((ATT_HEADS, 2 * LANES, tq), BF16),
                        pltpu.VMEM((ATT_HEADS, tq), F32),
                        pltpu.VMEM((ATT_HEADS, LANES, tq), F32)],
        compiler_params=pltpu.CompilerParams(
            dimension_semantics=("arbitrary", "arbitrary"), vmem_limit_bytes=VMEM_LIMIT),
        name="moba",
    )(qT, selbT, k, vT_even, vT_odd, tbT, cfar, eT_all)


def _merge_kernel(tiles_per_seq,
                  x_ref, ya_ref, mkT_ref, mv_ref, g1_ref, wr_ref, cw_ref, bg_ref,
                  wba_ref, wbc_ref, wbx_ref, wo_ref, g2_ref, wrt_ref, br_ref,
                  x1_ref, h2_ref, idx_ref, rank_ref, wt_ref, cnt_ref, zprev_ref):
    i = pl.program_id(0)
    tm, D = x_ref.shape
    CW = cw_ref.shape[1]
    XW = mv_ref.shape[2]

    @pl.when(i == 0)
    def _():
        cnt_ref[...] = jnp.zeros_like(cnt_ref)

    @pl.when(i % tiles_per_seq == 0)
    def _():
        zprev_ref[...] = jnp.zeros_like(zprev_ref)

    x = x_ref[...]
    h = _rms(x, g1_ref[...]).astype(BF16)
    pr = _dot(h, wr_ref[...])

    cb = pr[:, :CW]
    z = pr[:, CW:2 * CW] * pr[:, 2 * CW:3 * CW]
    row = lax.broadcasted_iota(jnp.int32, (tm, CW), 0)
    zp = zprev_ref[...]
    z1 = jnp.where(row == 0, zp[7:8], pltpu.roll(z, 1, 0))
    z2 = jnp.where(row == 0, zp[6:7], jnp.where(row == 1, zp[7:8], pltpu.roll(z, 2, 0)))
    zprev_ref[...] = z[tm - 8:]
    cw = cw_ref[...]
    y_conv = cb * (cw[0:1] * z2 + cw[1:2] * z1 + cw[2:3] * z)

    o0 = 3 * CW
    scale = 1.0 / math.sqrt(XATT_HEAD_DIM)
    ys = []
    for hh in range(XATT_HEADS):
        hs = slice(hh * XATT_HEAD_DIM, (hh + 1) * XATT_HEAD_DIM)
        qx = pr[:, o0 + hh * XATT_HEAD_DIM:o0 + (hh + 1) * XATT_HEAD_DIM].astype(BF16)
        s = _dot(qx, mkT_ref[0, hs, :]) * scale
        e = jnp.exp(s - jnp.max(s, axis=1, keepdims=True))
        l = jnp.sum(e, axis=1, keepdims=True)
        ys.append(_dot(e.astype(BF16), mv_ref[0, :, hs]) / l)
    y_x = jnp.concatenate(ys, axis=1)

    o1 = o0 + XW
    bg = bg_ref[...]
    merged = (_sigmoid(pr[:, o1:o1 + D] + bg[0:1]) * _dot(ya_ref[...], wba_ref[...])
              + _sigmoid(pr[:, o1 + D:o1 + 2 * D] + bg[1:2]) * _dot(y_conv.astype(BF16), wbc_ref[...])
              + _sigmoid(pr[:, o1 + 2 * D:o1 + 3 * D] + bg[2:3]) * _dot(y_x.astype(BF16), wbx_ref[...]))
    x1 = x + _dot(merged.astype(BF16), wo_ref[...])
    x1_ref[...] = x1
    h2 = _rms(x1, g2_ref[...])
    for c in range(D // LANES):
        h2_ref[pl.ds(c, tm, stride=D // LANES), :] = h2[:, c * LANES:(c + 1) * LANES]

    lg = _dot3(wrt_ref[...], h2, _dot_nt) + br_ref[...]
    E = lg.shape[0]
    ee = lax.broadcasted_iota(jnp.int32, (E, tm), 0).astype(F32)
    work = lg
    member = jnp.zeros((E, tm), F32)
    picks, vals = [], []
    for _ in range(TOP_K):
        m = jnp.max(work, axis=0, keepdims=True)
        first = jnp.min(jnp.where(work == m, ee, float(E)), axis=0, keepdims=True)
        pick = ee == first
        work = jnp.where(pick, -jnp.inf, work)
        member = jnp.where(pick, 1.0, member)
        picks.append((pick, first))
        vals.append(m)
    exps = [jnp.exp(v - vals[0]) for v in vals]
    denom = exps[0] + exps[1] + exps[2] + exps[3]

    r_i = lax.broadcasted_iota(jnp.int32, (tm, tm), 0)
    c_i = lax.broadcasted_iota(jnp.int32, (tm, tm), 1)
    upper = jnp.where(r_i < c_i, 1.0, 0.0).astype(BF16)
    before = _dot(member.astype(BF16), upper) + cnt_ref[:, 0:1]
    cnt_ref[...] = cnt_ref[...] + jnp.sum(member, axis=1, keepdims=True)

    wrow = lax.broadcasted_iota(jnp.int32, (LANES, tm), 0)
    wpad = jnp.zeros((LANES, tm), F32)
    for kk in range(TOP_K):
        pick, first = picks[kk]
        idx_ref[kk:kk + 1, :] = first.astype(jnp.int32)
        rank_ref[kk:kk + 1, :] = jnp.sum(jnp.where(pick, before, 0.0), axis=0,
                                         keepdims=True).astype(jnp.int32)
        wpad = jnp.where(wrow == kk, exps[kk] / denom, wpad)
    wt_ref[...] = wpad.T


def _merge(x2, yatt2, mkT, mv, g1, w_rest, conv_w, b_gate, wba, wbc, wbx, wo, g2, wrt, br, S):
    T, D = x2.shape
    tm = TOKEN_TILE
    nt = T // tm
    tps = S // tm
    CW = conv_w.shape[1]
    XW, M = mkT.shape[1], mkT.shape[2]
    E = wrt.shape[0]
    const = lambda shape: pl.BlockSpec(shape, lambda i: (0,) * len(shape))
    return pl.pallas_call(
        functools.partial(_merge_kernel, tps),
        grid=(nt,),
        in_specs=[pl.BlockSpec((tm, D), lambda i: (i, 0)),
                  pl.BlockSpec((tm, yatt2.shape[1]), lambda i: (i, 0)),
                  pl.BlockSpec((1, XW, M), lambda i: (i // tps, 0, 0)),
                  pl.BlockSpec((1, M, XW), lambda i: (i // tps, 0, 0)),
                  const((1, D)), const(w_rest.shape), const(conv_w.shape), const(b_gate.shape),
                  const(wba.shape), const(wbc.shape), const(wbx.shape), const(wo.shape),
                  const((1, D)), const(wrt.shape), const(br.shape)],
        out_specs=[pl.BlockSpec((tm, D), lambda i: (i, 0)),
                   pl.BlockSpec((tm * (D // LANES), LANES), lambda i: (i, 0)),
                   pl.BlockSpec((TOP_K, tm), lambda i: (0, i)),
                   pl.BlockSpec((TOP_K, tm), lambda i: (0, i)),
                   pl.BlockSpec((tm, LANES), lambda i: (i, 0)),
                   pl.BlockSpec((E, LANES), lambda i: (0, 0))],
        out_shape=[jax.ShapeDtypeStruct((T, D), F32),
                   jax.ShapeDtypeStruct((T * (D // LANES), LANES), F32),
                   jax.ShapeDtypeStruct((TOP_K, T), jnp.int32),
                   jax.ShapeDtypeStruct((TOP_K, T), jnp.int32),
                   jax.ShapeDtypeStruct((T, LANES), F32),
                   jax.ShapeDtypeStruct((E, LANES), F32)],
        scratch_shapes=[pltpu.VMEM((8, CW), F32)],
        compiler_params=pltpu.CompilerParams(
            dimension_semantics=("arbitrary",), vmem_limit_bytes=VMEM_LIMIT),
        name="merge_route",
    )(x2, yatt2, mkT, mv, g1, w_rest, conv_w, b_gate, wba, wbc, wbx, wo, g2, wrt, br)


def _dispatch_kernel(T, dest_ref, pe_ref, h2_ref, xs_ref, zero_ref, sem, zsem):
    i = pl.program_id(0)
    tm = h2_ref.shape[0]
    bm = zero_ref.shape[0]

    @pl.when(i == 0)
    def _():
        zero_ref[...] = jnp.zeros_like(zero_ref)

        def last_block(e):
            end = pe_ref[e]
            start = pl.multiple_of(jnp.maximum(end - bm, 0), bm)
            has_rows = end > (pe_ref[e - 1] if e else 0)
            return has_rows, pltpu.make_async_copy(zero_ref, xs_ref.at[pl.ds(start, bm)], zsem)

        def tail_block(b):
            start = pe_ref[N_EXPERTS - 1] + b * bm
            in_range = start < xs_ref.shape[0]
            start = pl.multiple_of(jnp.minimum(start, xs_ref.shape[0] - bm), bm)
            return in_range, pltpu.make_async_copy(zero_ref, xs_ref.at[pl.ds(start, bm)], zsem)

        fills = [last_block(e) for e in range(N_EXPERTS)] + [tail_block(b) for b in range(N_EXPERTS)]
        for cond, cp in fills:
            pl.when(cond)(cp.start)
        for cond, cp in fills:
            pl.when(cond)(cp.wait)

    def row_copy(t, d):
        return pltpu.make_async_copy(h2_ref.at[pl.ds(t, 1)], xs_ref.at[pl.ds(d, 1)], sem)

    def issue(t, c):
        for kk in range(TOP_K):
            row_copy(t, dest_ref[kk * T + i * tm + t]).start()
        return c

    lax.fori_loop(0, tm, issue, 0)

    def drain(t, c):
        for kk in range(TOP_K):
            row_copy(t, dest_ref[kk * T + i * tm + t]).wait()
        return c

    lax.fori_loop(0, tm, drain, 0)


def _dispatch(dest_flat, pad_ends, h2, n_rows):
    T, D = h2.shape
    tm = TOKEN_TILE
    return pl.pallas_call(
        functools.partial(_dispatch_kernel, T),
        grid_spec=pltpu.PrefetchScalarGridSpec(
            num_scalar_prefetch=2,
            grid=(T // tm,),
            in_specs=[pl.BlockSpec((tm, D), lambda i, d, pe: (i, 0))],
            out_specs=pl.BlockSpec(memory_space=pl.ANY),
            scratch_shapes=[pltpu.VMEM((EXPERT_ROWS, D), F32),
                            pltpu.SemaphoreType.DMA(()),
                            pltpu.SemaphoreType.DMA(())]),
        out_shape=jax.ShapeDtypeStruct((n_rows, D), F32),
        compiler_params=pltpu.CompilerParams(dimension_semantics=("arbitrary",)),
        name="dispatch",
    )(dest_flat, pad_ends, h2)


def _expert_kernel(be_ref, nu_ref, xs_ref, wgu_ref, bgu_ref, wd_ref, bd_ref, ys_ref):
    b = pl.program_id(0)
    F = wd_ref.shape[1]

    @pl.when(b < nu_ref[0])
    def _():
        gu = _dot(xs_ref[...].astype(BF16), wgu_ref[0]) + bgu_ref[0]
        g = jnp.minimum(gu[:, :F], SWIGLU_LIMIT)
        lin = jnp.clip(gu[:, F:], -SWIGLU_LIMIT, SWIGLU_LIMIT)
        act = (lin + 1.0) * (g * _sigmoid(SWIGLU_ALPHA * g))
        ys_ref[...] = _dot(act.astype(BF16), wd_ref[0]) + bd_ref[0]

    @pl.when(b >= nu_ref[0])
    def _():
        ys_ref[...] = jnp.zeros_like(ys_ref)


def _experts(block_expert, n_used, xs, wgu, bgu, wd, bd):
    NR, D = xs.shape
    bm = EXPERT_ROWS
    E, _, F2 = wgu.shape
    F = wd.shape[1]
    row_blk = lambda b, be, nu: (jnp.minimum(b, nu[0] - 1), 0)
    per_e = lambda b, be, nu: (be[b], 0, 0)
    return pl.pallas_call(
        _expert_kernel,
        grid_spec=pltpu.PrefetchScalarGridSpec(
            num_scalar_prefetch=2,
            grid=(NR // bm,),
            in_specs=[pl.BlockSpec((bm, D), row_blk),
                      pl.BlockSpec((1, D, F2), per_e),
                      pl.BlockSpec((1, 1, F2), per_e),
                      pl.BlockSpec((1, F, D), per_e),
                      pl.BlockSpec((1, 1, D), per_e)],
            out_specs=pl.BlockSpec((bm, D), lambda b, be, nu: (b, 0))),
        out_shape=jax.ShapeDtypeStruct((NR, D), F32),
        compiler_params=pltpu.CompilerParams(
            dimension_semantics=("arbitrary",), vmem_limit_bytes=VMEM_LIMIT),
        name="experts",
    )(block_expert, n_used, xs, wgu, bgu, wd, bd)


def _combine_kernel(T, dest_ref, x1_ref, wt_ref, g_ref, ys_ref, o_ref, buf_ref, sem):
    i = pl.program_id(0)
    n = pl.num_programs(0)
    tm = x1_ref.shape[0]

    def row_copy(step, slot, kk, t):
        d = dest_ref[kk * T + step * tm + t]
        return pltpu.make_async_copy(ys_ref.at[pl.ds(d, 1)], buf_ref.at[slot, kk, pl.ds(t, 1)],
                                     sem.at[slot])

    def issue(step, slot):
        def f(t, c):
            for kk in range(TOP_K):
                row_copy(step, slot, kk, t).start()
            return c
        lax.fori_loop(0, tm, f, 0)

    @pl.when(i == 0)
    def _():
        issue(0, 0)

    @pl.when(i + 1 < n)
    def _():
        issue(i + 1, (i + 1) % 2)

    slot = i % 2

    def drain(t, c):
        for kk in range(TOP_K):
            row_copy(i, slot, kk, t).wait()
        return c

    lax.fori_loop(0, tm, drain, 0)

    wt = wt_ref[...]
    y = x1_ref[...]
    for kk in range(TOP_K):
        y = y + wt[:, kk:kk + 1] * buf_ref[slot, kk]
    o_ref[...] = _rms(y, g_ref[...])


def _combine(dest_flat, x1, wt, g, ys):
    T, D = x1.shape
    tm = TOKEN_TILE
    return pl.pallas_call(
        functools.partial(_combine_kernel, T),
        grid_spec=pltpu.PrefetchScalarGridSpec(
            num_scalar_prefetch=1,
            grid=(T // tm,),
            in_specs=[pl.BlockSpec((tm, D), lambda i, d: (i, 0)),
                      pl.BlockSpec((tm, LANES), lambda i, d: (i, 0)),
                      pl.BlockSpec((1, D), lambda i, d: (0, 0)),
                      pl.BlockSpec(memory_space=pl.ANY)],
            out_specs=pl.BlockSpec((tm, D), lambda i, d: (i, 0)),
            scratch_shapes=[pltpu.VMEM((2, TOP_K, tm, D), F32),
                            pltpu.SemaphoreType.DMA((2,))]),
        out_shape=jax.ShapeDtypeStruct((T, D), F32),
        compiler_params=pltpu.CompilerParams(
            dimension_semantics=("arbitrary",), vmem_limit_bytes=VMEM_LIMIT),
        name="combine",
    )(dest_flat, x1, wt, g, ys)


def _invert_kernel(dest_ref, cnt_ref, pe_ref, src_ref):
    bm = EXPERT_ROWS
    n_assign = dest_ref.shape[0]

    assert bm & (bm - 1) == 0
    shift = bm.bit_length() - 1

    def pad_id(r):
        return n_assign + ((r >> shift) & 1) * bm + (r & (bm - 1))

    def put_pad(r, c):
        src_ref[r] = pad_id(r)
        return c

    for e in range(N_EXPERTS):
        first_row = (pe_ref[e - 1] if e else 0) + cnt_ref[e]
        lax.fori_loop(first_row, pe_ref[e], put_pad, 0)

    def put_pad_block(blk, c):
        for r in range(bm):
            src_ref[blk * bm + r] = n_assign + (blk % 2) * bm + r
        return c

    lax.fori_loop(pe_ref[N_EXPERTS - 1] // bm, src_ref.shape[0] // bm, put_pad_block, 0)

    def put(g, c):
        for u in range(8):
            a = g * 8 + u
            src_ref[dest_ref[a]] = a
        return c

    lax.fori_loop(0, n_assign // 8, put, 0)


def _invert(dest_flat, counts, pad_ends, n_src):
    return pl.pallas_call(
        _invert_kernel,
        grid_spec=pltpu.PrefetchScalarGridSpec(
            num_scalar_prefetch=3,
            grid=(1,),
            in_specs=[],
            out_specs=pl.BlockSpec(memory_space=pltpu.SMEM)),
        out_shape=jax.ShapeDtypeStruct((n_src,), jnp.int32),
        name="invert_routing",
    )(dest_flat, counts, pad_ends)


def _moe_kernel(T, be_ref, nu_ref, src_ref, h2t_ref, wgu_ref, bgu_ref, wd_ref, bd_ref, y4_ref,
                xbuf0, xbuf1, ybuf0, ybuf1, gsem, ssem):
    b = pl.program_id(0)
    n_used = nu_ref[0]
    bm = EXPERT_ROWS
    F = wd_ref.shape[1]
    CH = wgu_ref.shape[1] // LANES
    xbuf = (xbuf0, xbuf1)
    ybuf = (ybuf0, ybuf1)

    def gather_start(blk, slot):
        for r in range(bm):
            tok = src_ref[blk * bm + r] & (T - 1)
            pltpu.make_async_copy(h2t_ref.at[pl.ds(pl.multiple_of(tok * CH, CH), CH)],
                                  xbuf[slot].at[pl.ds(r * CH, CH)], gsem.at[slot]).start()

    def scatter_start(blk, slot):
        for r in range(bm):
            a = src_ref[blk * bm + r]
            pltpu.make_async_copy(ybuf[slot].at[pl.ds(r * CH, CH)],
                                  y4_ref.at[pl.ds(pl.multiple_of(a * CH, CH), CH)],
                                  ssem.at[slot]).start()

    def gather_wait(slot):
        pltpu.make_async_copy(h2t_ref.at[pl.ds(0, bm * CH)], xbuf[slot], gsem.at[slot]).wait()

    def scatter_wait(slot):
        pltpu.make_async_copy(ybuf[slot], y4_ref.at[pl.ds(0, bm * CH)], ssem.at[slot]).wait()

    def load_rows(slot):
        x = jnp.concatenate([xbuf[slot][pl.ds(c, bm, stride=CH), :] for c in range(CH)], axis=1)
        return x.astype(BF16)

    def ffn(slot, x):
        gu = _dot(x, wgu_ref[0]) + bgu_ref[0]
        g = jnp.minimum(gu[:, :F], SWIGLU_LIMIT)
        lin = jnp.clip(gu[:, F:], -SWIGLU_LIMIT, SWIGLU_LIMIT)
        act = (lin + 1.0) * (g * _sigmoid(SWIGLU_ALPHA * g))
        y = _dot(act.astype(BF16), wd_ref[0]) + bd_ref[0]
        for c in range(CH):
            ybuf[slot][pl.ds(c, bm, stride=CH), :] = y[:, c * LANES:(c + 1) * LANES]

    @pl.when(b == 0)
    def _():
        gather_start(0, 0)
        gather_wait(0)
        gather_start(1, 1)
        ffn(0, load_rows(0))

    for slot in (0, 1):
        @pl.when((b > 0) & (b < n_used) & (b % 2 == slot))
        def _():
            gather_start(b + 1, 1 - slot)
            scatter_start(b - 1, 1 - slot)
            gather_wait(slot)

            @pl.when(b >= 2)
            def _():
                scatter_wait(slot)

            ffn(slot, load_rows(slot))

        @pl.when((b == n_used - 1) & (b % 2 == slot))
        def _():
            scatter_start(b, slot)
            gather_wait(1 - slot)

            @pl.when(b >= 1)
            def _():
                scatter_wait(1 - slot)

            scatter_wait(slot)


def _moe_experts(T, block_expert, n_used, src, h2t, wgu, bgu, wd, bd):
    bm = EXPERT_ROWS
    E, D, F2 = wgu.shape
    F = wd.shape[1]
    CH = D // LANES
    n_blocks = block_expert.shape[0]
    assert T & (T - 1) == 0 and src.shape[0] == (n_blocks + 1) * bm
    per_e = lambda b, be, nu, s: (be[b], 0, 0)
    return pl.pallas_call(
        functools.partial(_moe_kernel, T),
        grid_spec=pltpu.PrefetchScalarGridSpec(
            num_scalar_prefetch=3,
            grid=(n_blocks,),
            in_specs=[pl.BlockSpec(memory_space=pl.ANY),
                      pl.BlockSpec((1, D, F2), per_e),
                      pl.BlockSpec((1, 1, F2), per_e),
                      pl.BlockSpec((1, F, D), per_e),
                      pl.BlockSpec((1, 1, D), per_e)],
            out_specs=pl.BlockSpec(memory_space=pl.ANY),
            scratch_shapes=[pltpu.VMEM((bm * CH, LANES), F32)] * 4 + [
                            pltpu.SemaphoreType.DMA((2,)),
                            pltpu.SemaphoreType.DMA((2,))]),
        out_shape=jax.ShapeDtypeStruct(((TOP_K * T + 2 * bm) * CH, LANES), F32),
        compiler_params=pltpu.CompilerParams(
            dimension_semantics=("arbitrary",), vmem_limit_bytes=VMEM_LIMIT),
        name="moe_experts",
    )(block_expert, n_used, src, h2t, wgu, bgu, wd, bd)


def _finish_kernel(x1_ref, wt_ref, g_ref, y0_ref, y1_ref, y2_ref, y3_ref, o_ref):
    tm, D = x1_ref.shape
    CH = D // LANES
    wt = wt_ref[...]
    y = x1_ref[...]
    for kk, yk_ref in enumerate((y0_ref, y1_ref, y2_ref, y3_ref)):
        yk = jnp.concatenate([yk_ref[pl.ds(c, tm, stride=CH), :] for c in range(CH)], axis=1)
        y = y + wt[:, kk:kk + 1] * yk
    o_ref[...] = _rms(y, g_ref[...])


def _finish(x1, wt, g, y4):
    T, D = x1.shape
    tm = TOKEN_TILE
    CH = D // LANES
    nt = T // tm
    choice = lambda kk: pl.BlockSpec((tm * CH, LANES), lambda i: (kk * nt + i, 0))
    return pl.pallas_call(
        _finish_kernel,
        grid=(nt,),
        in_specs=[pl.BlockSpec((tm, D), lambda i: (i, 0)),
                  pl.BlockSpec((tm, LANES), lambda i: (i, 0)),
                  pl.BlockSpec((1, D), lambda i: (0, 0)),
                  choice(0), choice(1), choice(2), choice(3)],
        out_specs=pl.BlockSpec((tm, D), lambda i: (i, 0)),
        out_shape=jax.ShapeDtypeStruct((T, D), F32),
        compiler_params=pltpu.CompilerParams(
            dimension_semantics=("arbitrary",), vmem_limit_bytes=VMEM_LIMIT),
        name="finish",
    )(x1, wt, g, y4, y4, y4, y4)


def _t5_bucket(dist):
    n = jnp.maximum(dist, 0)
    max_exact = REL_BUCKETS // 2
    nf = jnp.maximum(n, 1).astype(F32)
    large = max_exact + (jnp.log(nf / max_exact) / math.log(REL_MAX_DIST / max_exact)
                         * (REL_BUCKETS - max_exact)).astype(jnp.int32)
    large = jnp.minimum(large, REL_BUCKETS - 1)
    return jnp.where(n < max_exact, n, large)


def _bias_tiles(rel_bias):
    L = MOBA_BLOCK
    assert REL_MAX_DIST <= L
    d = np.arange(-L, 3 * L)
    onehot = (_t5_bucket(jnp.asarray(d))[:, None] == jnp.arange(REL_BUCKETS)).astype(F32)
    by_dist = jnp.dot(onehot, rel_bias.astype(F32), precision=HIGHEST)
    by_dist = jnp.where((d >= 0)[:, None], by_dist, NEG).T
    H = by_dist.shape[0]
    tiles = []
    for t in range(3):
        v = jnp.concatenate([by_dist[:, (t + 1) * L:(t + 2) * L], by_dist[:, t * L:(t + 1) * L]],
                            axis=1)
        flat = jnp.tile(v, (1, L))[:, :L * (2 * L - 1)]
        tiles.append(flat.reshape(H, L, 2 * L - 1)[:, :, :L])
    return jnp.stack(tiles) * LOG2E


def kernel(x, mem, rel_bias, norm_mix_g, w_in, b_gate, conv_w, norm_mem_g, w_mem_kv, w_br_att,
           w_br_conv, w_br_xatt, w_out, norm_ffn_g, w_router, b_router, w_gu, b_gu, w_down,
           b_down, norm_final_g):
    B, S, D = x.shape
    T = B * S
    depth = w_in.shape[0]
    assert depth == 1, "the combine step applies the final norm: single-layer configuration only"
    W = ATT_WIDTH
    XW = XATT_HEADS * XATT_HEAD_DIM
    E = w_router.shape[2]
    bm = EXPERT_ROWS
    tb = _bias_tiles(rel_bias)

    xc = x
    for l in range(depth):
        w_l = w_in[l]
        wk = w_l[:, W:2 * W].astype(BF16)
        wqvT = jnp.concatenate([w_l[:, :W], w_l[:, 2 * W:3 * W]], axis=1).T.astype(BF16)
        w_rest = w_l[:, 3 * W:].astype(BF16)
        wm = w_mem_kv[l]

        mkT, mv = _mem_kv(mem, norm_mem_g[l][None], wm[:, :XW].T.astype(BF16),
                          wm[:, XW:].astype(BF16))
        qT, k, vT_even, vT_odd, selbT = _qkv_select(xc, norm_mix_g[l][None], wk, wqvT)
        y_att = _moba(qT, selbT, k, vT_even, vT_odd, tb[:2], tb[2, :, 0, 0])

        x1, h2, idx, rank, wt, cnt = _merge(
            xc.reshape(T, D), y_att.reshape(T, W), mkT, mv, norm_mix_g[l][None], w_rest,
            conv_w[l], b_gate[l], w_br_att[l].astype(BF16), w_br_conv[l].astype(BF16),
            w_br_xatt[l].astype(BF16), w_out[l].astype(BF16), norm_ffn_g[l][None],
            w_router[l].T, b_router[l][:, None], S)

        counts = cnt[:, 0].astype(jnp.int32)
        padded = (counts + bm - 1) // bm * bm
        pad_ends = jnp.cumsum(padded).astype(jnp.int32)
        pad_starts = pad_ends - padded
        onehot = idx[..., None] == jnp.arange(E, dtype=jnp.int32)
        dest = (jnp.sum(jnp.where(onehot, pad_starts, 0), axis=-1) + rank).reshape(-1)
        n_rows = T * TOP_K + E * bm
        n_blocks = n_rows // bm
        n_used = pad_ends[-1] // bm
        blk = jnp.minimum(jnp.arange(n_blocks, dtype=jnp.int32), n_used - 1) * bm
        block_expert = jnp.minimum(jnp.sum(blk[:, None] >= pad_ends[None, :], axis=1),
                                   E - 1).astype(jnp.int32)

        src = _invert(dest, counts, pad_ends, n_rows + bm)

        y4 = _moe_experts(T, block_expert, n_used[None], src, h2, w_gu[l].astype(BF16),
                          b_gu[l][:, None], w_down[l].astype(BF16), b_down[l][:, None])
        xc = _finish(x1, wt, norm_final_g[None], y4).reshape(B, S, D)
    return xc
```

```python
import functools
import math

import jax
import jax.numpy as jnp
import numpy as np
from jax import lax
from jax.experimental import pallas as pl
from jax.experimental.pallas import tpu as pltpu

F32 = jnp.float32
BF16 = jnp.bfloat16
HIGHEST = lax.Precision.HIGHEST

ATT_HEADS = 8
ATT_HEAD_DIM = 64
ATT_WIDTH = ATT_HEADS * ATT_HEAD_DIM
MOBA_BLOCK = 256
MOBA_TOPK = 3
REL_BUCKETS = 32
REL_MAX_DIST = 128
CONV_K = 3
XATT_HEADS = 4
XATT_HEAD_DIM = 128
N_BRANCH = 3
N_EXPERTS = 32
TOP_K = 4
SWIGLU_LIMIT = 7.0
SWIGLU_ALPHA = 1.702
EPS = 1e-5
NEG = -1e30
LOG2E = math.log2(math.e)

LANES = 128
SEL_SLOTS = 16
TOKEN_TILE = 256
EXPERT_ROWS = 256
VMEM_LIMIT = 56 * 1024 * 1024


def _rms(x, g):
    return x * lax.rsqrt(jnp.mean(x * x, axis=-1, keepdims=True) + EPS) * g


def _dot(a, b):
    return jnp.dot(a, b, preferred_element_type=F32)


def _dot_nt(a, b, precision=None):
    return lax.dot_general(a, b, (((1,), (1,)), ((), ())), precision=precision,
                           preferred_element_type=F32)


def _sigmoid(x):
    return 1.0 / (1.0 + jnp.exp(-x))


def _split_bf16(x):
    hi = x.astype(BF16)
    return hi, (x - hi.astype(F32)).astype(BF16)


def _dot3(a, b, dot):
    m = a.shape[0]
    a_hi, a_lo = _split_bf16(a)
    b_hi, b_lo = _split_bf16(b)
    both = dot(jnp.concatenate([a_hi, a_lo], axis=0), b_hi)
    return both[:m] + both[m:] + dot(a_hi, b_lo)


def _mem_kv_kernel(mem_ref, g_ref, wkT_ref, wv_ref, mkT_ref, mv_ref):
    mn = _rms(mem_ref[0], g_ref[...]).astype(BF16)
    mkT_ref[0] = _dot_nt(wkT_ref[...], mn).astype(BF16)
    mv_ref[0] = _dot(mn, wv_ref[...]).astype(BF16)


def _mem_kv(mem, g, wkT, wv):
    B, M, D = mem.shape
    XW = wv.shape[1]
    return pl.pallas_call(
        _mem_kv_kernel,
        grid=(B,),
        in_specs=[pl.BlockSpec((1, M, D), lambda b: (b, 0, 0)),
                  pl.BlockSpec((1, D), lambda b: (0, 0)),
                  pl.BlockSpec((XW, D), lambda b: (0, 0)),
                  pl.BlockSpec((D, XW), lambda b: (0, 0))],
        out_specs=[pl.BlockSpec((1, XW, M), lambda b: (b, 0, 0)),
                   pl.BlockSpec((1, M, XW), lambda b: (b, 0, 0))],
        out_shape=[jax.ShapeDtypeStruct((B, XW, M), BF16),
                   jax.ShapeDtypeStruct((B, M, XW), BF16)],
        name="mem_kv",
    )(mem, g, wkT, wv)


def _qkv_select_kernel(x_ref, g_ref, wk_ref, wqvT_ref, qT_ref, k_ref, vTe_ref, vTo_ref, selbT_ref,
                       km_ref):
    i = pl.program_id(1)
    tq = x_ref.shape[1]
    W = ATT_WIDTH

    @pl.when(i == 0)
    def _():
        km_ref[...] = jnp.zeros_like(km_ref)

    h = _rms(x_ref[0], g_ref[...]).astype(BF16)
    k = _dot(h, wk_ref[...])
    k_ref[0] = k.astype(BF16)
    qvT = _dot_nt(wqvT_ref[...], h)
    qT = qvT[:W] * (1.0 / math.sqrt(ATT_HEAD_DIM))
    qT_ref[0] = (qT * LOG2E).astype(BF16)
    vT = qvT[W:]
    even_head = (lax.broadcasted_iota(jnp.int32, (W, tq), 0) // ATT_HEAD_DIM) % 2 == 0
    vTe_ref[0] = jnp.where(even_head, vT, 1.0).astype(BF16)
    vTo_ref[0] = jnp.where(even_head, 1.0, vT).astype(BF16)

    gate = _dot3(km_ref[...], qT, _dot)
    g3 = gate.reshape(ATT_HEADS, SEL_SLOTS, tq)
    jj = lax.broadcasted_iota(jnp.int32, g3.shape, 1).astype(F32)
    valid = jj < i.astype(F32)
    g3 = jnp.where(valid, g3, -jnp.inf)
    sel = jnp.zeros(g3.shape, F32)
    for _ in range(MOBA_TOPK):
        m = jnp.max(g3, axis=1, keepdims=True)
        first = jnp.min(jnp.where(g3 == m, jj, float(SEL_SLOTS)), axis=1, keepdims=True)
        pick = jj == first
        sel = jnp.where(pick, 1.0, sel)
        g3 = jnp.where(pick, -jnp.inf, g3)
    keep = ((sel > 0.5) & valid) | (jj == i.astype(F32))
    selbT_ref[0] = jnp.where(keep, 0.0, NEG).reshape(ATT_HEADS * SEL_SLOTS, tq).astype(BF16)

    kmean = jnp.sum(k, axis=0, keepdims=True) * (1.0 / tq)
    lane_head = lax.broadcasted_iota(jnp.int32, (1, W), 1) // ATT_HEAD_DIM
    for hh in range(ATT_HEADS):
        km_ref[pl.ds(hh * SEL_SLOTS + i, 1), :] = jnp.where(lane_head == hh, kmean, 0.0)


def _qkv_select(x, g, wk, wqvT):
    B, S, D = x.shape
    W = ATT_WIDTH
    tq = MOBA_BLOCK
    nb = S // tq
    assert nb <= SEL_SLOTS and ATT_HEADS * SEL_SLOTS == LANES
    by_row = pl.BlockSpec((1, tq, W), lambda b, i: (b, i, 0))
    by_col = pl.BlockSpec((1, W, tq), lambda b, i: (b, 0, i))
    return pl.pallas_call(
        _qkv_select_kernel,
        grid=(B, nb),
        in_specs=[pl.BlockSpec((1, tq, D), lambda b, i: (b, i, 0)),
                  pl.BlockSpec((1, D), lambda b, i: (0, 0)),
                  pl.BlockSpec((D, W), lambda b, i: (0, 0)),
                  pl.BlockSpec((2 * W, D), lambda b, i: (0, 0))],
        out_specs=[by_col, by_row, by_col, by_col,
                   pl.BlockSpec((1, LANES, tq), lambda b, i: (b, 0, i))],
        out_shape=[jax.ShapeDtypeStruct((B, W, S), BF16),
                   jax.ShapeDtypeStruct((B, S, W), BF16),
                   jax.ShapeDtypeStruct((B, W, S), BF16),
                   jax.ShapeDtypeStruct((B, W, S), BF16),
                   jax.ShapeDtypeStruct((B, LANES, S), BF16)],
        scratch_shapes=[pltpu.VMEM((LANES, W), F32)],
        compiler_params=pltpu.CompilerParams(
            dimension_semantics=("arbitrary", "arbitrary"), vmem_limit_bytes=VMEM_LIMIT),
        name="qkv_select",
    )(x, g, wk, wqvT)


def _moba_kernel(qT_ref, selbT_ref, k_ref, vTe_ref, vTo_ref, tbT_ref, cfar_ref, eT_ref, o_ref,
                 rhs_ref, m_ref, acc_ref):
    i = pl.program_id(1)
    tq = o_ref.shape[1]
    L = MOBA_BLOCK
    HD = ATT_HEAD_DIM
    H = ATT_HEADS
    selbT = selbT_ref[0].astype(F32)
    row = lax.broadcasted_iota(jnp.int32, (LANES, tq), 0)
    vT_refs = (vTe_ref, vTo_ref)
    slabs = [slice(p * LANES, (p + 1) * LANES) for p in range(H // 2)]

    for h in range(H):
        qTp = qT_ref[0, slabs[h // 2], :].astype(F32)
        own = (row >= HD) if h % 2 else (row < HD)
        rhs_ref[h, :LANES, :] = jnp.where(own, qTp, 0.0).astype(BF16)
        rhs_ref[h, LANES:, :] = jnp.where(row // SEL_SLOTS == h, selbT, 0.0).astype(BF16)
    m_ref[...] = jnp.full(m_ref.shape, -jnp.inf, F32)
    acc_ref[...] = jnp.zeros_like(acc_ref)

    def attend(js, tiles):
        koffs = [pl.multiple_of(j * L, L) for j in js]
        s = []
        for j, koff in zip(js, koffs):
            sj = []
            for p in range(H // 2):
                lhs = jnp.concatenate([k_ref[0, pl.ds(koff, L), slabs[p]], eT_ref[j]], axis=1)
                sj += [_dot(lhs, rhs_ref[2 * p + hh]) for hh in (0, 1)]
            s.append(sj)
        for sj, koff, tile in zip(s, koffs, tiles):
            pr, alpha = [], []
            for h in range(H):
                sh = sj[h] if tile is None else sj[h] + tbT_ref[tile, h]
                cm = jnp.max(jnp.max(sh.reshape(8, L // 8, tq), axis=0), axis=0, keepdims=True)
                m_old = m_ref[h:h + 1, :]
                if tile is None:
                    m_new = jnp.maximum(m_old, cm + cfar_ref[h])
                    pr.append(jnp.exp2(sh - (m_new - cfar_ref[h])).astype(BF16))
                else:
                    m_new = jnp.maximum(m_old, cm)
                    pr.append(jnp.exp2(sh - m_new).astype(BF16))
                m_ref[h:h + 1, :] = m_new
                alpha.append(jnp.exp2(m_old - m_new))
            for h in range(H):
                pv = _dot(vT_refs[h % 2][0, slabs[h // 2], pl.ds(koff, L)], pr[h])
                acc_ref[h] = alpha[h] * acc_ref[h] + pv

    n_far = jnp.maximum(i - 1, 0)

    def far_pair(jj, carry):
        attend([2 * jj, 2 * jj + 1], [None, None])
        return carry

    lax.fori_loop(0, n_far // 2, far_pair, 0)

    @pl.when(n_far % 2 == 1)
    def _():
        attend([n_far - 1], [None])

    @pl.when(i >= 1)
    def _():
        attend([i - 1, i], [1, 0])

    @pl.when(i == 0)
    def _():
        attend([i], [0])
    for p in range(H // 2):
        a0, a1 = acc_ref[2 * p], acc_ref[2 * p + 1]
        oT = jnp.concatenate([a0[:HD] / a0[HD:HD + 1], a1[HD:] / a1[0:1]], axis=0)
        o_ref[0, :, slabs[p]] = oT.T.astype(BF16)


def _moba(qT, selbT, k, vT_even, vT_odd, tbT, cfar):
    B, S, W = k.shape
    tq = MOBA_BLOCK
    nb = S // tq
    lane = np.arange(LANES)[None, None, :] % SEL_SLOTS
    eT_all = jnp.asarray(np.broadcast_to(lane == np.arange(SEL_SLOTS)[:, None, None],
                                         (SEL_SLOTS, MOBA_BLOCK, LANES)), BF16)
    whole_T = pl.BlockSpec((1, W, S), lambda b, i: (b, 0, 0))
    return pl.pallas_call(
        _moba_kernel,
        grid=(B, nb),
        in_specs=[pl.BlockSpec((1, W, tq), lambda b, i: (b, 0, i)),
                  pl.BlockSpec((1, LANES, tq), lambda b, i: (b, 0, i)),
                  pl.BlockSpec((1, S, W), lambda b, i: (b, 0, 0)),
                  whole_T, whole_T,
                  pl.BlockSpec(tbT.shape, lambda b, i: (0, 0, 0, 0)),
                  pl.BlockSpec(memory_space=pltpu.SMEM),
                  pl.BlockSpec(eT_all.shape, lambda b, i: (0, 0, 0))],
        out_specs=pl.BlockSpec((1, tq, W), lambda b, i: (b, i, 0)),
        out_shape=jax.ShapeDtypeStruct((B, S, W), BF16),
        scratch_shapes=[pltpu.VMEM((ATT_HEADS, 2 * LANES, tq), BF16),
                        pltpu.VMEM((ATT_HEADS, tq), F32),
                        pltpu.VMEM((ATT_HEADS, LANES, tq), F32)],
        compiler_params=pltpu.CompilerParams(
            dimension_semantics=("arbitrary", "arbitrary"), vmem_limit_bytes=VMEM_LIMIT),
        name="moba",
    )(qT, selbT, k, vT_even, vT_odd, tbT, cfar, eT_all)


def _merge_kernel(tiles_per_seq,
                  x_ref, ya_ref, mkT_ref, mv_ref, g1_ref, wr_ref, cw_ref, bg_ref,
                  wba_ref, wbc_ref, wbx_ref, wo_ref, g2_ref, wrt_ref, br_ref,
                  x1_ref, h2_ref, idx_ref, rank_ref, wt_ref, cnt_ref, zprev_ref):
    i = pl.program_id(0)
    tm, D = x_ref.shape
    CW = cw_ref.shape[1]
    XW = mv_ref.shape[2]

    @pl.when(i == 0)
    def _():
        cnt_ref[...] = jnp.zeros_like(cnt_ref)

    @pl.when(i % tiles_per_seq == 0)
    def _():
        zprev_ref[...] = jnp.zeros_like(zprev_ref)

    x = x_ref[...]
    h = _rms(x, g1_ref[...]).astype(BF16)
    pr = _dot(h, wr_ref[...])

    cb = pr[:, :CW]
    z = pr[:, CW:2 * CW] * pr[:, 2 * CW:3 * CW]
    row = lax.broadcasted_iota(jnp.int32, (tm, CW), 0)
    zp = zprev_ref[...]
    z1 = jnp.where(row == 0, zp[7:8], pltpu.roll(z, 1, 0))
    z2 = jnp.where(row == 0, zp[6:7], jnp.where(row == 1, zp[7:8], pltpu.roll(z, 2, 0)))
    zprev_ref[...] = z[tm - 8:]
    cw = cw_ref[...]
    y_conv = cb * (cw[0:1] * z2 + cw[1:2] * z1 + cw[2:3] * z)

    o0 = 3 * CW
    scale = 1.0 / math.sqrt(XATT_HEAD_DIM)
    ys = []
    for hh in range(XATT_HEADS):
        hs = slice(hh * XATT_HEAD_DIM, (hh + 1) * XATT_HEAD_DIM)
        qx = pr[:, o0 + hh * XATT_HEAD_DIM:o0 + (hh + 1) * XATT_HEAD_DIM].astype(BF16)
        s = _dot(qx, mkT_ref[0, hs, :]) * scale
        e = jnp.exp(s - jnp.max(s, axis=1, keepdims=True))
        l = jnp.sum(e, axis=1, keepdims=True)
        ys.append(_dot(e.astype(BF16), mv_ref[0, :, hs]) / l)
    y_x = jnp.concatenate(ys, axis=1)

    o1 = o0 + XW
    bg = bg_ref[...]
    merged = (_sigmoid(pr[:, o1:o1 + D] + bg[0:1]) * _dot(ya_ref[...], wba_ref[...])
              + _sigmoid(pr[:, o1 + D:o1 + 2 * D] + bg[1:2]) * _dot(y_conv.astype(BF16), wbc_ref[...])
              + _sigmoid(pr[:, o1 + 2 * D:o1 + 3 * D] + bg[2:3]) * _dot(y_x.astype(BF16), wbx_ref[...]))
    x1 = x + _dot(merged.astype(BF16), wo_ref[...])
    x1_ref[...] = x1
    h2 = _rms(x1, g2_ref[...])
    for c in range(D // LANES):
        h2_ref[pl.ds(c, tm, stride=D // LANES), :] = h2[:, c * LANES:(c + 1) * LANES]

    lg = _dot3(wrt_ref[...], h2, _dot_nt) + br_ref[...]
    E = lg.shape[0]
    ee = lax.broadcasted_iota(jnp.int32, (E, tm), 0).astype(F32)
    work = lg
    member = jnp.zeros((E, tm), F32)
    picks, vals = [], []
    for _ in range(TOP_K):
        m = jnp.max(work, axis=0, keepdims=True)
        first = jnp.min(jnp.where(work == m, ee, float(E)), axis=0, keepdims=True)
        pick = ee == first
        work = jnp.where(pick, -jnp.inf, work)
        member = jnp.where(pick, 1.0, member)
        picks.append((pick, first))
        vals.append(m)
    exps = [jnp.exp(v - vals[0]) for v in vals]
    denom = exps[0] + exps[1] + exps[2] + exps[3]

    r_i = lax.broadcasted_iota(jnp.int32, (tm, tm), 0)
    c_i = lax.broadcasted_iota(jnp.int32, (tm, tm), 1)
    upper = jnp.where(r_i < c_i, 1.0, 0.0).astype(BF16)
    before = _dot(member.astype(BF16), upper) + cnt_ref[:, 0:1]
    cnt_ref[...] = cnt_ref[...] + jnp.sum(member, axis=1, keepdims=True)

    wrow = lax.broadcasted_iota(jnp.int32, (LANES, tm), 0)
    wpad = jnp.zeros((LANES, tm), F32)
    for kk in range(TOP_K):
        pick, first = picks[kk]
        idx_ref[kk:kk + 1, :] = first.astype(jnp.int32)
        rank_ref[kk:kk + 1, :] = jnp.sum(jnp.where(pick, before, 0.0), axis=0,
                                         keepdims=True).astype(jnp.int32)
        wpad = jnp.where(wrow == kk, exps[kk] / denom, wpad)
    wt_ref[...] = wpad.T


def _merge(x2, yatt2, mkT, mv, g1, w_rest, conv_w, b_gate, wba, wbc, wbx, wo, g2, wrt, br, S):
    T, D = x2.shape
    tm = TOKEN_TILE
    nt = T // tm
    tps = S // tm
    CW = conv_w.shape[1]
    XW, M = mkT.shape[1], mkT.shape[2]
    E = wrt.shape[0]
    const = lambda shape: pl.BlockSpec(shape, lambda i: (0,) * len(shape))
    return pl.pallas_call(
        functools.partial(_merge_kernel, tps),
        grid=(nt,),
        in_specs=[pl.BlockSpec((tm, D), lambda i: (i, 0)),
                  pl.BlockSpec((tm, yatt2.shape[1]), lambda i: (i, 0)),
                  pl.BlockSpec((1, XW, M), lambda i: (i // tps, 0, 0)),
                  pl.BlockSpec((1, M, XW), lambda i: (i // tps, 0, 0)),
                  const((1, D)), const(w_rest.shape), const(conv_w.shape), const(b_gate.shape),
                  const(wba.shape), const(wbc.shape), const(wbx.shape), const(wo.shape),
                  const((1, D)), const(wrt.shape), const(br.shape)],
        out_specs=[pl.BlockSpec((tm, D), lambda i: (i, 0)),
                   pl.BlockSpec((tm * (D // LANES), LANES), lambda i: (i, 0)),
                   pl.BlockSpec((TOP_K, tm), lambda i: (0, i)),
                   pl.BlockSpec((TOP_K, tm), lambda i: (0, i)),
                   pl.BlockSpec((tm, LANES), lambda i: (i, 0)),
                   pl.BlockSpec((E, LANES), lambda i: (0, 0))],
        out_shape=[jax.ShapeDtypeStruct((T, D), F32),
                   jax.ShapeDtypeStruct((T * (D // LANES), LANES), F32),
                   jax.ShapeDtypeStruct((TOP_K, T), jnp.int32),
                   jax.ShapeDtypeStruct((TOP_K, T), jnp.int32),
                   jax.ShapeDtypeStruct((T, LANES), F32),
                   jax.ShapeDtypeStruct((E, LANES), F32)],
        scratch_shapes=[pltpu.VMEM((8, CW), F32)],
        compiler_params=pltpu.CompilerParams(
            dimension_semantics=("arbitrary",), vmem_limit_bytes=VMEM_LIMIT),
        name="merge_route",
    )(x2, yatt2, mkT, mv, g1, w_rest, conv_w, b_gate, wba, wbc, wbx, wo, g2, wrt, br)


def _dispatch_kernel(T, dest_ref, pe_ref, h2_ref, xs_ref, zero_ref, sem, zsem):
    i = pl.program_id(0)
    tm = h2_ref.shape[0]
    bm = zero_ref.shape[0]

    @pl.when(i == 0)
    def _():
        zero_ref[...] = jnp.zeros_like(zero_ref)

        def last_block(e):
            end = pe_ref[e]
            start = pl.multiple_of(jnp.maximum(end - bm, 0), bm)
            has_rows = end > (pe_ref[e - 1] if e else 0)
            return has_rows, pltpu.make_async_copy(zero_ref, xs_ref.at[pl.ds(start, bm)], zsem)

        def tail_block(b):
            start = pe_ref[N_EXPERTS - 1] + b * bm
            in_range = start < xs_ref.shape[0]
            start = pl.multiple_of(jnp.minimum(start, xs_ref.shape[0] - bm), bm)
            return in_range, pltpu.make_async_copy(zero_ref, xs_ref.at[pl.ds(start, bm)], zsem)

        fills = [last_block(e) for e in range(N_EXPERTS)] + [tail_block(b) for b in range(N_EXPERTS)]
        for cond, cp in fills:
            pl.when(cond)(cp.start)
        for cond, cp in fills:
            pl.when(cond)(cp.wait)

    def row_copy(t, d):
        return pltpu.make_async_copy(h2_ref.at[pl.ds(t, 1)], xs_ref.at[pl.ds(d, 1)], sem)

    def issue(t, c):
        for kk in range(TOP_K):
            row_copy(t, dest_ref[kk * T + i * tm + t]).start()
        return c

    lax.fori_loop(0, tm, issue, 0)

    def drain(t, c):
        for kk in range(TOP_K):
            row_copy(t, dest_ref[kk * T + i * tm + t]).wait()
        return c

    lax.fori_loop(0, tm, drain, 0)


def _dispatch(dest_flat, pad_ends, h2, n_rows):
    T, D = h2.shape
    tm = TOKEN_TILE
    return pl.pallas_call(
        functools.partial(_dispatch_kernel, T),
        grid_spec=pltpu.PrefetchScalarGridSpec(
            num_scalar_prefetch=2,
            grid=(T // tm,),
            in_specs=[pl.BlockSpec((tm, D), lambda i, d, pe: (i, 0))],
            out_specs=pl.BlockSpec(memory_space=pl.ANY),
            scratch_shapes=[pltpu.VMEM((EXPERT_ROWS, D), F32),
                            pltpu.SemaphoreType.DMA(()),
                            pltpu.SemaphoreType.DMA(())]),
        out_shape=jax.ShapeDtypeStruct((n_rows, D), F32),
        compiler_params=pltpu.CompilerParams(dimension_semantics=("arbitrary",)),
        name="dispatch",
    )(dest_flat, pad_ends, h2)


def _expert_kernel(be_ref, nu_ref, xs_ref, wgu_ref, bgu_ref, wd_ref, bd_ref, ys_ref):
    b = pl.program_id(0)
    F = wd_ref.shape[1]

    @pl.when(b < nu_ref[0])
    def _():
        gu = _dot(xs_ref[...].astype(BF16), wgu_ref[0]) + bgu_ref[0]
        g = jnp.minimum(gu[:, :F], SWIGLU_LIMIT)
        lin = jnp.clip(gu[:, F:], -SWIGLU_LIMIT, SWIGLU_LIMIT)
        act = (lin + 1.0) * (g * _sigmoid(SWIGLU_ALPHA * g))
        ys_ref[...] = _dot(act.astype(BF16), wd_ref[0]) + bd_ref[0]

    @pl.when(b >= nu_ref[0])
    def _():
        ys_ref[...] = jnp.zeros_like(ys_ref)


def _experts(block_expert, n_used, xs, wgu, bgu, wd, bd):
    NR, D = xs.shape
    bm = EXPERT_ROWS
    E, _, F2 = wgu.shape
    F = wd.shape[1]
    row_blk = lambda b, be, nu: (jnp.minimum(b, nu[0] - 1), 0)
    per_e = lambda b, be, nu: (be[b], 0, 0)
    return pl.pallas_call(
        _expert_kernel,
        grid_spec=pltpu.PrefetchScalarGridSpec(
            num_scalar_prefetch=2,
            grid=(NR // bm,),
            in_specs=[pl.BlockSpec((bm, D), row_blk),
                      pl.BlockSpec((1, D, F2), per_e),
                      pl.BlockSpec((1, 1, F2), per_e),
                      pl.BlockSpec((1, F, D), per_e),
                      pl.BlockSpec((1, 1, D), per_e)],
            out_specs=pl.BlockSpec((bm, D), lambda b, be, nu: (b, 0))),
        out_shape=jax.ShapeDtypeStruct((NR, D), F32),
        compiler_params=pltpu.CompilerParams(
            dimension_semantics=("arbitrary",), vmem_limit_bytes=VMEM_LIMIT),
        name="experts",
    )(block_expert, n_used, xs, wgu, bgu, wd, bd)


def _combine_kernel(T, dest_ref, x1_ref, wt_ref, g_ref, ys_ref, o_ref, buf_ref, sem):
    i = pl.program_id(0)
    n = pl.num_programs(0)
    tm = x1_ref.shape[0]

    def row_copy(step, slot, kk, t):
        d = dest_ref[kk * T + step * tm + t]
        return pltpu.make_async_copy(ys_ref.at[pl.ds(d, 1)], buf_ref.at[slot, kk, pl.ds(t, 1)],
                                     sem.at[slot])

    def issue(step, slot):
        def f(t, c):
            for kk in range(TOP_K):
                row_copy(step, slot, kk, t).start()
            return c
        lax.fori_loop(0, tm, f, 0)

    @pl.when(i == 0)
    def _():
        issue(0, 0)

    @pl.when(i + 1 < n)
    def _():
        issue(i + 1, (i + 1) % 2)

    slot = i % 2

    def drain(t, c):
        for kk in range(TOP_K):
            row_copy(i, slot, kk, t).wait()
        return c

    lax.fori_loop(0, tm, drain, 0)

    wt = wt_ref[...]
    y = x1_ref[...]
    for kk in range(TOP_K):
        y = y + wt[:, kk:kk + 1] * buf_ref[slot, kk]
    o_ref[...] = _rms(y, g_ref[...])


def _combine(dest_flat, x1, wt, g, ys):
    T, D = x1.shape
    tm = TOKEN_TILE
    return pl.pallas_call(
        functools.partial(_combine_kernel, T),
        grid_spec=pltpu.PrefetchScalarGridSpec(
            num_scalar_prefetch=1,
            grid=(T // tm,),
            in_specs=[pl.BlockSpec((tm, D), lambda i, d: (i, 0)),
                      pl.BlockSpec((tm, LANES), lambda i, d: (i, 0)),
                      pl.BlockSpec((1, D), lambda i, d: (0, 0)),
                      pl.BlockSpec(memory_space=pl.ANY)],
            out_specs=pl.BlockSpec((tm, D), lambda i, d: (i, 0)),
            scratch_shapes=[pltpu.VMEM((2, TOP_K, tm, D), F32),
                            pltpu.SemaphoreType.DMA((2,))]),
        out_shape=jax.ShapeDtypeStruct((T, D), F32),
        compiler_params=pltpu.CompilerParams(
            dimension_semantics=("arbitrary",), vmem_limit_bytes=VMEM_LIMIT),
        name="combine",
    )(dest_flat, x1, wt, g, ys)


def _invert_kernel(dest_ref, cnt_ref, pe_ref, src_ref):
    bm = EXPERT_ROWS
    n_assign = dest_ref.shape[0]

    assert bm & (bm - 1) == 0
    shift = bm.bit_length() - 1

    def pad_id(r):
        return n_assign + ((r >> shift) & 1) * bm + (r & (bm - 1))

    def put_pad(r, c):
        src_ref[r] = pad_id(r)
        return c

    for e in range(N_EXPERTS):
        first_row = (pe_ref[e - 1] if e else 0) + cnt_ref[e]
        lax.fori_loop(first_row, pe_ref[e], put_pad, 0)

    def put_pad_block(blk, c):
        for r in range(bm):
            src_ref[blk * bm + r] = n_assign + (blk % 2) * bm + r
        return c

    lax.fori_loop(pe_ref[N_EXPERTS - 1] // bm, src_ref.shape[0] // bm, put_pad_block, 0)

    def put(g, c):
        for u in range(8):
            a = g * 8 + u
            src_ref[dest_ref[a]] = a
        return c

    lax.fori_loop(0, n_assign // 8, put, 0)


def _invert(dest_flat, counts, pad_ends, n_src):
    return pl.pallas_call(
        _invert_kernel,
        grid_spec=pltpu.PrefetchScalarGridSpec(
            num_scalar_prefetch=3,
            grid=(1,),
            in_specs=[],
            out_specs=pl.BlockSpec(memory_space=pltpu.SMEM)),
        out_shape=jax.ShapeDtypeStruct((n_src,), jnp.int32),
        name="invert_routing",
    )(dest_flat, counts, pad_ends)


def _moe_kernel(T, be_ref, nu_ref, src_ref, h2t_ref, wgu_ref, bgu_ref, wd_ref, bd_ref, y4_ref,
                xbuf0, xbuf1, ybuf0, ybuf1, wgu_bf, wd_bf, gsem, ssem):
    b = pl.program_id(0)
    n_used = nu_ref[0]
    bm = EXPERT_ROWS
    F = wd_ref.shape[1]
    CH = wgu_ref.shape[1] // LANES
    xbuf = (xbuf0, xbuf1)
    ybuf = (ybuf0, ybuf1)

    def gather_start(blk, slot):
        for r in range(bm):
            tok = src_ref[blk * bm + r] & (T - 1)
            pltpu.make_async_copy(h2t_ref.at[pl.ds(pl.multiple_of(tok * CH, CH), CH)],
                                  xbuf[slot].at[pl.ds(r * CH, CH)], gsem.at[slot]).start()

    def scatter_start(blk, slot):
        for r in range(bm):
            a = src_ref[blk * bm + r]
            pltpu.make_async_copy(ybuf[slot].at[pl.ds(r * CH, CH)],
                                  y4_ref.at[pl.ds(pl.multiple_of(a * CH, CH), CH)],
                                  ssem.at[slot]).start()

    def gather_wait(slot):
        pltpu.make_async_copy(h2t_ref.at[pl.ds(0, bm * CH)], xbuf[slot], gsem.at[slot]).wait()

    def scatter_wait(slot):
        pltpu.make_async_copy(ybuf[slot], y4_ref.at[pl.ds(0, bm * CH)], ssem.at[slot]).wait()

    def load_rows(slot):
        x = jnp.concatenate([xbuf[slot][pl.ds(c, bm, stride=CH), :] for c in range(CH)], axis=1)
        return x.astype(BF16)

    def ffn(slot, x):
        gu = _dot(x, wgu_bf[...]) + bgu_ref[0]
        g = jnp.minimum(gu[:, :F], SWIGLU_LIMIT)
        lin = jnp.clip(gu[:, F:], -SWIGLU_LIMIT, SWIGLU_LIMIT)
        act = (lin + 1.0) * (g * _sigmoid(SWIGLU_ALPHA * g))
        y = _dot(act.astype(BF16), wd_bf[...]) + bd_ref[0]
        for c in range(CH):
            ybuf[slot][pl.ds(c, bm, stride=CH), :] = y[:, c * LANES:(c + 1) * LANES]

    @pl.when((b < n_used) & ((b == 0) | (be_ref[b] != be_ref[jnp.maximum(b - 1, 0)])))
    def _():
        wgu_bf[...] = wgu_ref[0].astype(BF16)
        wd_bf[...] = wd_ref[0].astype(BF16)

    @pl.when(b == 0)
    def _():
        gather_start(0, 0)
        gather_wait(0)
        gather_start(1, 1)
        ffn(0, load_rows(0))

    for slot in (0, 1):
        @pl.when((b > 0) & (b < n_used) & (b % 2 == slot))
        def _():
            gather_start(b + 1, 1 - slot)
            gather_wait(slot)

            @pl.when(b >= 2)
            def _():
                scatter_wait(slot)

            scatter_start(b - 1, 1 - slot)
            ffn(slot, load_rows(slot))

        @pl.when((b == n_used - 1) & (b % 2 == slot))
        def _():
            scatter_start(b, slot)
            gather_wait(1 - slot)

            @pl.when(b >= 1)
            def _():
                scatter_wait(1 - slot)

            scatter_wait(slot)


def _moe_experts(T, block_expert, n_used, src, h2t, wgu, bgu, wd, bd):
    bm = EXPERT_ROWS
    E, D, F2 = wgu.shape
    F = wd.shape[1]
    CH = D // LANES
    n_blocks = block_expert.shape[0]
    assert T & (T - 1) == 0 and src.shape[0] == (n_blocks + 1) * bm
    per_e = lambda b, be, nu, s: (be[b], 0, 0)
    return pl.pallas_call(
        functools.partial(_moe_kernel, T),
        grid_spec=pltpu.PrefetchScalarGridSpec(
            num_scalar_prefetch=3,
            grid=(n_blocks,),
            in_specs=[pl.BlockSpec(memory_space=pl.ANY),
                      pl.BlockSpec((1, D, F2), per_e),
                      pl.BlockSpec((1, 1, F2), per_e),
                      pl.BlockSpec((1, F, D), per_e),
                      pl.BlockSpec((1, 1, D), per_e)],
            out_specs=pl.BlockSpec(memory_space=pl.ANY),
            scratch_shapes=[pltpu.VMEM((bm * CH, LANES), F32)] * 4 + [
                            pltpu.VMEM((D, F2), BF16),
                            pltpu.VMEM((F, D), BF16),
                            pltpu.SemaphoreType.DMA((2,)),
                            pltpu.SemaphoreType.DMA((2,))]),
        out_shape=jax.ShapeDtypeStruct(((TOP_K * T + 2 * bm) * CH, LANES), F32),
        compiler_params=pltpu.CompilerParams(
            dimension_semantics=("arbitrary",), vmem_limit_bytes=VMEM_LIMIT),
        name="moe_experts",
    )(block_expert, n_used, src, h2t, wgu, bgu, wd, bd)


def _finish_kernel(x1_ref, wt_ref, g_ref, y0_ref, y1_ref, y2_ref, y3_ref, o_ref):
    tm, D = x1_ref.shape
    CH = D // LANES
    wt = wt_ref[...]
    y = x1_ref[...]
    for kk, yk_ref in enumerate((y0_ref, y1_ref, y2_ref, y3_ref)):
        yk = jnp.concatenate([yk_ref[pl.ds(c, tm, stride=CH), :] for c in range(CH)], axis=1)
        y = y + wt[:, kk:kk + 1] * yk
    o_ref[...] = _rms(y, g_ref[...])


def _finish(x1, wt, g, y4):
    T, D = x1.shape
    tm = TOKEN_TILE
    CH = D // LANES
    nt = T // tm
    choice = lambda kk: pl.BlockSpec((tm * CH, LANES), lambda i: (kk * nt + i, 0))
    return pl.pallas_call(
        _finish_kernel,
        grid=(nt,),
        in_specs=[pl.BlockSpec((tm, D), lambda i: (i, 0)),
                  pl.BlockSpec((tm, LANES), lambda i: (i, 0)),
                  pl.BlockSpec((1, D), lambda i: (0, 0)),
                  choice(0), choice(1), choice(2), choice(3)],
        out_specs=pl.BlockSpec((tm, D), lambda i: (i, 0)),
        out_shape=jax.ShapeDtypeStruct((T, D), F32),
        compiler_params=pltpu.CompilerParams(
            dimension_semantics=("arbitrary",), vmem_limit_bytes=VMEM_LIMIT),
        name="finish",
    )(x1, wt, g, y4, y4, y4, y4)


def _t5_bucket(dist):
    n = jnp.maximum(dist, 0)
    max_exact = REL_BUCKETS // 2
    nf = jnp.maximum(n, 1).astype(F32)
    large = max_exact + (jnp.log(nf / max_exact) / math.log(REL_MAX_DIST / max_exact)
                         * (REL_BUCKETS - max_exact)).astype(jnp.int32)
    large = jnp.minimum(large, REL_BUCKETS - 1)
    return jnp.where(n < max_exact, n, large)


def _bias_tiles(rel_bias):
    L = MOBA_BLOCK
    assert REL_MAX_DIST <= L
    d = np.arange(-L, 3 * L)
    onehot = (_t5_bucket(jnp.asarray(d))[:, None] == jnp.arange(REL_BUCKETS)).astype(F32)
    by_dist = jnp.dot(onehot, rel_bias.astype(F32), precision=HIGHEST)
    by_dist = jnp.where((d >= 0)[:, None], by_dist, NEG).T
    H = by_dist.shape[0]
    tiles = []
    for t in range(3):
        v = jnp.concatenate([by_dist[:, (t + 1) * L:(t + 2) * L], by_dist[:, t * L:(t + 1) * L]],
                            axis=1)
        flat = jnp.tile(v, (1, L))[:, :L * (2 * L - 1)]
        tiles.append(flat.reshape(H, L, 2 * L - 1)[:, :, :L])
    return jnp.stack(tiles) * LOG2E


def kernel(x, mem, rel_bias, norm_mix_g, w_in, b_gate, conv_w, norm_mem_g, w_mem_kv, w_br_att,
           w_br_conv, w_br_xatt, w_out, norm_ffn_g, w_router, b_router, w_gu, b_gu, w_down,
           b_down, norm_final_g):
    B, S, D = x.shape
    T = B * S
    depth = w_in.shape[0]
    assert depth == 1, "the combine step applies the final norm: single-layer configuration only"
    W = ATT_WIDTH
    XW = XATT_HEADS * XATT_HEAD_DIM
    E = w_router.shape[2]
    bm = EXPERT_ROWS
    tb = _bias_tiles(rel_bias)

    xc = x
    for l in range(depth):
        w_l = w_in[l]
        wk = w_l[:, W:2 * W].astype(BF16)
        wqvT = jnp.concatenate([w_l[:, :W], w_l[:, 2 * W:3 * W]], axis=1).T.astype(BF16)
        w_rest = w_l[:, 3 * W:].astype(BF16)
        wm = w_mem_kv[l]

        mkT, mv = _mem_kv(mem, norm_mem_g[l][None], wm[:, :XW].T.astype(BF16),
                          wm[:, XW:].astype(BF16))
        qT, k, vT_even, vT_odd, selbT = _qkv_select(xc, norm_mix_g[l][None], wk, wqvT)
        y_att = _moba(qT, selbT, k, vT_even, vT_odd, tb[:2], tb[2, :, 0, 0])

        x1, h2, idx, rank, wt, cnt = _merge(
            xc.reshape(T, D), y_att.reshape(T, W), mkT, mv, norm_mix_g[l][None], w_rest,
            conv_w[l], b_gate[l], w_br_att[l].astype(BF16), w_br_conv[l].astype(BF16),
            w_br_xatt[l].astype(BF16), w_out[l].astype(BF16), norm_ffn_g[l][None],
            w_router[l].T, b_router[l][:, None], S)

        counts = cnt[:, 0].astype(jnp.int32)
        padded = (counts + bm - 1) // bm * bm
        pad_ends = jnp.cumsum(padded).astype(jnp.int32)
        pad_starts = pad_ends - padded
        onehot = idx[..., None] == jnp.arange(E, dtype=jnp.int32)
        dest = (jnp.sum(jnp.where(onehot, pad_starts, 0), axis=-1) + rank).reshape(-1)
        n_rows = T * TOP_K + E * bm
        n_blocks = n_rows // bm
        n_used = pad_ends[-1] // bm
        blk = jnp.minimum(jnp.arange(n_blocks, dtype=jnp.int32), n_used - 1) * bm
        block_expert = jnp.minimum(jnp.sum(blk[:, None] >= pad_ends[None, :], axis=1),
                                   E - 1).astype(jnp.int32)

        src = _invert(dest, counts, pad_ends, n_rows + bm)

        y4 = _moe_experts(T, block_expert, n_used[None], src, h2, w_gu[l], b_gu[l][:, None],
                          w_down[l], b_down[l][:, None])
        xc = _finish(x1, wt, norm_final_g[None], y4).reshape(B, S, D)
    return xc
```

```python
import functools
import math

import jax
import jax.numpy as jnp
import numpy as np
from jax import lax
from jax.experimental import pallas as pl
from jax.experimental.pallas import tpu as pltpu

F32 = jnp.float32
BF16 = jnp.bfloat16
HIGHEST = lax.Precision.HIGHEST

ATT_HEADS = 8
ATT_HEAD_DIM = 64
ATT_WIDTH = ATT_HEADS * ATT_HEAD_DIM
MOBA_BLOCK = 256
MOBA_TOPK = 3
REL_BUCKETS = 32
REL_MAX_DIST = 128
XATT_HEADS = 4
XATT_HEAD_DIM = 128
N_EXPERTS = 32
TOP_K = 4
SWIGLU_LIMIT = 7.0
SWIGLU_ALPHA = 1.702
EPS = 1e-5
NEG = -1e30
LOG2E = math.log2(math.e)

LANES = 128
SEL_SLOTS = 16
TOKEN_TILE = 256
EXPERT_ROWS = 256
VMEM_LIMIT = 56 * 1024 * 1024


def _rms(x, g):
    return x * lax.rsqrt(jnp.mean(x * x, axis=-1, keepdims=True) + EPS) * g


def _dot(a, b):
    return jnp.dot(a, b, preferred_element_type=F32)


def _dot_nt(a, b):
    return lax.dot_general(a, b, (((1,), (1,)), ((), ())), preferred_element_type=F32)


def _sigmoid(x):
    return 1.0 / (1.0 + jnp.exp(-x))


def _split_bf16(x):
    hi = x.astype(BF16)
    return hi, (x - hi.astype(F32)).astype(BF16)


def _dot3(a, b, dot):
    m = a.shape[0]
    a_hi, a_lo = _split_bf16(a)
    b_hi, b_lo = _split_bf16(b)
    both = dot(jnp.concatenate([a_hi, a_lo], axis=0), b_hi)
    return both[:m] + both[m:] + dot(a_hi, b_lo)


def _mem_kv_kernel(mem_ref, g_ref, wkT_ref, wv_ref, mkT_ref, mv_ref):
    mn = _rms(mem_ref[0], g_ref[...]).astype(BF16)
    mkT_ref[0] = _dot_nt(wkT_ref[...], mn).astype(BF16)
    mv_ref[0] = _dot(mn, wv_ref[...]).astype(BF16)


def _mem_kv(mem, g, wkT, wv):
    B, M, D = mem.shape
    XW = wv.shape[1]
    return pl.pallas_call(
        _mem_kv_kernel,
        grid=(B,),
        in_specs=[pl.BlockSpec((1, M, D), lambda b: (b, 0, 0)),
                  pl.BlockSpec((1, D), lambda b: (0, 0)),
                  pl.BlockSpec((XW, D), lambda b: (0, 0)),
                  pl.BlockSpec((D, XW), lambda b: (0, 0))],
        out_specs=[pl.BlockSpec((1, XW, M), lambda b: (b, 0, 0)),
                   pl.BlockSpec((1, M, XW), lambda b: (b, 0, 0))],
        out_shape=[jax.ShapeDtypeStruct((B, XW, M), BF16),
                   jax.ShapeDtypeStruct((B, M, XW), BF16)],
        name="mem_kv",
    )(mem, g, wkT, wv)


def _qkv_select_kernel(x_ref, g_ref, wk_ref, wqvT_ref, qT_ref, k_ref, vTe_ref, vTo_ref, selbT_ref,
                       km_ref):
    i = pl.program_id(1)
    tq = x_ref.shape[1]
    W = ATT_WIDTH

    @pl.when(i == 0)
    def _():
        km_ref[...] = jnp.zeros_like(km_ref)

    h = _rms(x_ref[0], g_ref[...]).astype(BF16)
    k = _dot(h, wk_ref[...])
    k_ref[0] = k.astype(BF16)
    qvT = _dot_nt(wqvT_ref[...], h)
    qT = qvT[:W] * (1.0 / math.sqrt(ATT_HEAD_DIM))
    qT_ref[0] = (qT * LOG2E).astype(BF16)
    vT = qvT[W:]
    even_head = (lax.broadcasted_iota(jnp.int32, (W, tq), 0) // ATT_HEAD_DIM) % 2 == 0
    vTe_ref[0] = jnp.where(even_head, vT, 1.0).astype(BF16)
    vTo_ref[0] = jnp.where(even_head, 1.0, vT).astype(BF16)

    gate = _dot3(km_ref[...], qT, _dot)
    g3 = gate.reshape(ATT_HEADS, SEL_SLOTS, tq)
    jj = lax.broadcasted_iota(jnp.int32, g3.shape, 1).astype(F32)
    valid = jj < i.astype(F32)
    g3 = jnp.where(valid, g3, -jnp.inf)
    sel = jnp.zeros(g3.shape, F32)
    for _ in range(MOBA_TOPK):
        m = jnp.max(g3, axis=1, keepdims=True)
        first = jnp.min(jnp.where(g3 == m, jj, float(SEL_SLOTS)), axis=1, keepdims=True)
        pick = jj == first
        sel = jnp.where(pick, 1.0, sel)
        g3 = jnp.where(pick, -jnp.inf, g3)
    keep = ((sel > 0.5) & valid) | (jj == i.astype(F32))
    selbT_ref[0] = jnp.where(keep, 0.0, NEG).reshape(ATT_HEADS * SEL_SLOTS, tq).astype(BF16)

    kmean = jnp.sum(k, axis=0, keepdims=True) * (1.0 / tq)
    lane_head = lax.broadcasted_iota(jnp.int32, (1, W), 1) // ATT_HEAD_DIM
    for hh in range(ATT_HEADS):
        km_ref[pl.ds(hh * SEL_SLOTS + i, 1), :] = jnp.where(lane_head == hh, kmean, 0.0)


def _qkv_select(x, g, wk, wqvT):
    B, S, D = x.shape
    W = ATT_WIDTH
    tq = MOBA_BLOCK
    nb = S // tq
    assert nb <= SEL_SLOTS and ATT_HEADS * SEL_SLOTS == LANES
    by_row = pl.BlockSpec((1, tq, W), lambda b, i: (b, i, 0))
    by_col = pl.BlockSpec((1, W, tq), lambda b, i: (b, 0, i))
    return pl.pallas_call(
        _qkv_select_kernel,
        grid=(B, nb),
        in_specs=[pl.BlockSpec((1, tq, D), lambda b, i: (b, i, 0)),
                  pl.BlockSpec((1, D), lambda b, i: (0, 0)),
                  pl.BlockSpec((D, W), lambda b, i: (0, 0)),
                  pl.BlockSpec((2 * W, D), lambda b, i: (0, 0))],
        out_specs=[by_col, by_row, by_col, by_col,
                   pl.BlockSpec((1, LANES, tq), lambda b, i: (b, 0, i))],
        out_shape=[jax.ShapeDtypeStruct((B, W, S), BF16),
                   jax.ShapeDtypeStruct((B, S, W), BF16),
                   jax.ShapeDtypeStruct((B, W, S), BF16),
                   jax.ShapeDtypeStruct((B, W, S), BF16),
                   jax.ShapeDtypeStruct((B, LANES, S), BF16)],
        scratch_shapes=[pltpu.VMEM((LANES, W), F32)],
        compiler_params=pltpu.CompilerParams(
            dimension_semantics=("arbitrary", "arbitrary"), vmem_limit_bytes=VMEM_LIMIT),
        name="qkv_select",
    )(x, g, wk, wqvT)


def _moba_kernel(qT_ref, selbT_ref, k_ref, vTe_ref, vTo_ref, tbT_ref, cfar_ref, eT_ref, o_ref,
                 rhs_ref, m_ref, acc_ref):
    i = pl.program_id(1)
    tq = o_ref.shape[1]
    L = MOBA_BLOCK
    HD = ATT_HEAD_DIM
    H = ATT_HEADS
    selbT = selbT_ref[0].astype(F32)
    row = lax.broadcasted_iota(jnp.int32, (LANES, tq), 0)
    vT_refs = (vTe_ref, vTo_ref)
    slabs = [slice(p * LANES, (p + 1) * LANES) for p in range(H // 2)]

    for h in range(H):
        qTp = qT_ref[0, slabs[h // 2], :].astype(F32)
        own = (row >= HD) if h % 2 else (row < HD)
        rhs_ref[h, :LANES, :] = jnp.where(own, qTp, 0.0).astype(BF16)
        rhs_ref[h, LANES:, :] = jnp.where(row // SEL_SLOTS == h, selbT, 0.0).astype(BF16)
    m_ref[...] = jnp.full(m_ref.shape, -jnp.inf, F32)
    acc_ref[...] = jnp.zeros_like(acc_ref)

    def attend(js, tiles):
        koffs = [pl.multiple_of(j * L, L) for j in js]
        s = []
        for j, koff in zip(js, koffs):
            sj = []
            for p in range(H // 2):
                lhs = jnp.concatenate([k_ref[0, pl.ds(koff, L), slabs[p]], eT_ref[j]], axis=1)
                sj += [_dot(lhs, rhs_ref[2 * p + hh]) for hh in (0, 1)]
            s.append(sj)
        for sj, koff, tile in zip(s, koffs, tiles):
            pr, alpha = [], []
            for h in range(H):
                sh = sj[h] if tile is None else sj[h] + tbT_ref[tile, h]
                cm = jnp.max(jnp.max(sh.reshape(8, L // 8, tq), axis=0), axis=0, keepdims=True)
                m_old = m_ref[h:h + 1, :]
                if tile is None:
                    m_new = jnp.maximum(m_old, cm + cfar_ref[h])
                    pr.append(jnp.exp2(sh - (m_new - cfar_ref[h])).astype(BF16))
                else:
                    m_new = jnp.maximum(m_old, cm)
                    pr.append(jnp.exp2(sh - m_new).astype(BF16))
                m_ref[h:h + 1, :] = m_new
                alpha.append(jnp.exp2(m_old - m_new))
            for h in range(H):
                pv = _dot(vT_refs[h % 2][0, slabs[h // 2], pl.ds(koff, L)], pr[h])
                acc_ref[h] = alpha[h] * acc_ref[h] + pv

    n_far = jnp.maximum(i - 1, 0)

    def far_pair(jj, carry):
        attend([2 * jj, 2 * jj + 1], [None, None])
        return carry

    lax.fori_loop(0, n_far // 2, far_pair, 0)

    @pl.when(n_far % 2 == 1)
    def _():
        attend([n_far - 1], [None])

    @pl.when(i >= 1)
    def _():
        attend([i - 1, i], [1, 0])

    @pl.when(i == 0)
    def _():
        attend([i], [0])

    for p in range(H // 2):
        a0, a1 = acc_ref[2 * p], acc_ref[2 * p + 1]
        oT = jnp.concatenate([a0[:HD] / a0[HD:HD + 1], a1[HD:] / a1[0:1]], axis=0)
        o_ref[0, :, slabs[p]] = oT.T.astype(BF16)


def _moba(qT, selbT, k, vT_even, vT_odd, tbT, cfar):
    B, S, W = k.shape
    tq = MOBA_BLOCK
    nb = S // tq
    lane = np.arange(LANES)[None, None, :] % SEL_SLOTS
    eT_all = jnp.asarray(np.broadcast_to(lane == np.arange(SEL_SLOTS)[:, None, None],
                                         (SEL_SLOTS, MOBA_BLOCK, LANES)), BF16)
    whole_T = pl.BlockSpec((1, W, S), lambda b, i: (b, 0, 0))
    return pl.pallas_call(
        _moba_kernel,
        grid=(B, nb),
        in_specs=[pl.BlockSpec((1, W, tq), lambda b, i: (b, 0, i)),
                  pl.BlockSpec((1, LANES, tq), lambda b, i: (b, 0, i)),
                  pl.BlockSpec((1, S, W), lambda b, i: (b, 0, 0)),
                  whole_T, whole_T,
                  pl.BlockSpec(tbT.shape, lambda b, i: (0, 0, 0, 0)),
                  pl.BlockSpec(memory_space=pltpu.SMEM),
                  pl.BlockSpec(eT_all.shape, lambda b, i: (0, 0, 0))],
        out_specs=pl.BlockSpec((1, tq, W), lambda b, i: (b, i, 0)),
        out_shape=jax.ShapeDtypeStruct((B, S, W), BF16),
        scratch_shapes=[pltpu.VMEM((ATT_HEADS, 2 * LANES, tq), BF16),
                        pltpu.VMEM((ATT_HEADS, tq), F32),
                        pltpu.VMEM((ATT_HEADS, LANES, tq), F32)],
        compiler_params=pltpu.CompilerParams(
            dimension_semantics=("arbitrary", "arbitrary"), vmem_limit_bytes=VMEM_LIMIT),
        name="moba",
    )(qT, selbT, k, vT_even, vT_odd, tbT, cfar, eT_all)


def _merge_kernel(tiles_per_seq,
                  x_ref, ya_ref, mkT_ref, mv_ref, g1_ref, wr_ref, cw_ref, bg_ref,
                  wba_ref, wbc_ref, wbx_ref, wo_ref, g2_ref, wrt_ref, br_ref,
                  x1_ref, h2_ref, idx_ref, rank_ref, wt_ref, cnt_ref, zprev_ref):
    i = pl.program_id(0)
    tm, D = x_ref.shape
    CW = cw_ref.shape[1]
    XW = mv_ref.shape[2]

    @pl.when(i == 0)
    def _():
        cnt_ref[...] = jnp.zeros_like(cnt_ref)

    @pl.when(i % tiles_per_seq == 0)
    def _():
        zprev_ref[...] = jnp.zeros_like(zprev_ref)

    x = x_ref[...]
    h = _rms(x, g1_ref[...]).astype(BF16)
    pr = _dot(h, wr_ref[...])

    cb = pr[:, :CW]
    z = pr[:, CW:2 * CW] * pr[:, 2 * CW:3 * CW]
    row = lax.broadcasted_iota(jnp.int32, (tm, CW), 0)
    zp = zprev_ref[...]
    z1 = jnp.where(row == 0, zp[7:8], pltpu.roll(z, 1, 0))
    z2 = jnp.where(row == 0, zp[6:7], jnp.where(row == 1, zp[7:8], pltpu.roll(z, 2, 0)))
    zprev_ref[...] = z[tm - 8:]
    cw = cw_ref[...]
    y_conv = cb * (cw[0:1] * z2 + cw[1:2] * z1 + cw[2:3] * z)

    o0 = 3 * CW
    scale = 1.0 / math.sqrt(XATT_HEAD_DIM)
    ys = []
    for hh in range(XATT_HEADS):
        hs = slice(hh * XATT_HEAD_DIM, (hh + 1) * XATT_HEAD_DIM)
        qx = pr[:, o0 + hh * XATT_HEAD_DIM:o0 + (hh + 1) * XATT_HEAD_DIM].astype(BF16)
        s = _dot(qx, mkT_ref[0, hs, :]) * scale
        e = jnp.exp(s - jnp.max(s, axis=1, keepdims=True))
        l = jnp.sum(e, axis=1, keepdims=True)
        ys.append(_dot(e.astype(BF16), mv_ref[0, :, hs]) / l)
    y_x = jnp.concatenate(ys, axis=1)

    o1 = o0 + XW
    bg = bg_ref[...]
    merged = (_sigmoid(pr[:, o1:o1 + D] + bg[0:1]) * _dot(ya_ref[...], wba_ref[...])
              + _sigmoid(pr[:, o1 + D:o1 + 2 * D] + bg[1:2]) * _dot(y_conv.astype(BF16), wbc_ref[...])
              + _sigmoid(pr[:, o1 + 2 * D:o1 + 3 * D] + bg[2:3]) * _dot(y_x.astype(BF16), wbx_ref[...]))
    x1 = x + _dot(merged.astype(BF16), wo_ref[...])
    x1_ref[...] = x1
    h2 = _rms(x1, g2_ref[...])
    for c in range(D // LANES):
        h2_ref[pl.ds(c, tm, stride=D // LANES), :] = h2[:, c * LANES:(c + 1) * LANES]

    lg = _dot3(wrt_ref[...], h2, _dot_nt) + br_ref[...]
    E = lg.shape[0]
    ee = lax.broadcasted_iota(jnp.int32, (E, tm), 0).astype(F32)
    work = lg
    member = jnp.zeros((E, tm), F32)
    picks, vals = [], []
    for _ in range(TOP_K):
        m = jnp.max(work, axis=0, keepdims=True)
        first = jnp.min(jnp.where(work == m, ee, float(E)), axis=0, keepdims=True)
        pick = ee == first
        work = jnp.where(pick, -jnp.inf, work)
        member = jnp.where(pick, 1.0, member)
        picks.append((pick, first))
        vals.append(m)
    exps = [jnp.exp(v - vals[0]) for v in vals]
    denom = exps[0] + exps[1] + exps[2] + exps[3]

    r_i = lax.broadcasted_iota(jnp.int32, (tm, tm), 0)
    c_i = lax.broadcasted_iota(jnp.int32, (tm, tm), 1)
    upper = jnp.where(r_i < c_i, 1.0, 0.0).astype(BF16)
    before = _dot(member.astype(BF16), upper) + cnt_ref[:, 0:1]
    cnt_ref[...] = cnt_ref[...] + jnp.sum(member, axis=1, keepdims=True)

    wrow = lax.broadcasted_iota(jnp.int32, (LANES, tm), 0)
    wpad = jnp.zeros((LANES, tm), F32)
    for kk in range(TOP_K):
        pick, first = picks[kk]
        idx_ref[kk:kk + 1, :] = first.astype(jnp.int32)
        rank_ref[kk:kk + 1, :] = jnp.sum(jnp.where(pick, before, 0.0), axis=0,
                                         keepdims=True).astype(jnp.int32)
        wpad = jnp.where(wrow == kk, exps[kk] / denom, wpad)
    wt_ref[...] = wpad.T


def _merge(x2, yatt2, mkT, mv, g1, w_rest, conv_w, b_gate, wba, wbc, wbx, wo, g2, wrt, br, S):
    T, D = x2.shape
    tm = TOKEN_TILE
    nt = T // tm
    tps = S // tm
    CW = conv_w.shape[1]
    XW, M = mkT.shape[1], mkT.shape[2]
    E = wrt.shape[0]
    const = lambda shape: pl.BlockSpec(shape, lambda i: (0,) * len(shape))
    return pl.pallas_call(
        functools.partial(_merge_kernel, tps),
        grid=(nt,),
        in_specs=[pl.BlockSpec((tm, D), lambda i: (i, 0)),
                  pl.BlockSpec((tm, yatt2.shape[1]), lambda i: (i, 0)),
                  pl.BlockSpec((1, XW, M), lambda i: (i // tps, 0, 0)),
                  pl.BlockSpec((1, M, XW), lambda i: (i // tps, 0, 0)),
                  const((1, D)), const(w_rest.shape), const(conv_w.shape), const(b_gate.shape),
                  const(wba.shape), const(wbc.shape), const(wbx.shape), const(wo.shape),
                  const((1, D)), const(wrt.shape), const(br.shape)],
        out_specs=[pl.BlockSpec((tm, D), lambda i: (i, 0)),
                   pl.BlockSpec((tm * (D // LANES), LANES), lambda i: (i, 0)),
                   pl.BlockSpec((TOP_K, tm), lambda i: (0, i)),
                   pl.BlockSpec((TOP_K, tm), lambda i: (0, i)),
                   pl.BlockSpec((tm, LANES), lambda i: (i, 0)),
                   pl.BlockSpec((E, LANES), lambda i: (0, 0))],
        out_shape=[jax.ShapeDtypeStruct((T, D), F32),
                   jax.ShapeDtypeStruct((T * (D // LANES), LANES), F32),
                   jax.ShapeDtypeStruct((TOP_K, T), jnp.int32),
                   jax.ShapeDtypeStruct((TOP_K, T), jnp.int32),
                   jax.ShapeDtypeStruct((T, LANES), F32),
                   jax.ShapeDtypeStruct((E, LANES), F32)],
        scratch_shapes=[pltpu.VMEM((8, CW), F32)],
        compiler_params=pltpu.CompilerParams(
            dimension_semantics=("arbitrary",), vmem_limit_bytes=VMEM_LIMIT),
        name="merge_route",
    )(x2, yatt2, mkT, mv, g1, w_rest, conv_w, b_gate, wba, wbc, wbx, wo, g2, wrt, br)


def _invert_kernel(dest_ref, cnt_ref, pe_ref, src_ref):
    bm = EXPERT_ROWS
    n_assign = dest_ref.shape[0]
    assert bm & (bm - 1) == 0
    shift = bm.bit_length() - 1

    def pad_id(r):
        return n_assign + ((r >> shift) & 1) * bm + (r & (bm - 1))

    def put_pad(r, c):
        src_ref[r] = pad_id(r)
        return c

    for e in range(N_EXPERTS):
        first_row = (pe_ref[e - 1] if e else 0) + cnt_ref[e]
        lax.fori_loop(first_row, pe_ref[e], put_pad, 0)

    def put_pad_block(blk, c):
        for r in range(bm):
            src_ref[blk * bm + r] = n_assign + (blk % 2) * bm + r
        return c

    last_blk = src_ref.shape[0] // bm - 1
    lax.fori_loop(pe_ref[N_EXPERTS - 1] // bm, last_blk, put_pad_block, 0)
    for r in range(bm):
        src_ref[last_blk * bm + r] = n_assign + 3 * bm + r

    def put(g, c):
        for u in range(8):
            a = g * 8 + u
            src_ref[dest_ref[a]] = a
        return c

    lax.fori_loop(0, n_assign // 8, put, 0)


def _invert(dest_flat, counts, pad_ends, n_src):
    return pl.pallas_call(
        _invert_kernel,
        grid_spec=pltpu.PrefetchScalarGridSpec(
            num_scalar_prefetch=3,
            grid=(1,),
            in_specs=[],
            out_specs=pl.BlockSpec(memory_space=pltpu.SMEM)),
        out_shape=jax.ShapeDtypeStruct((n_src,), jnp.int32),
        name="invert_routing",
    )(dest_flat, counts, pad_ends)


def _moe_kernel(T, be_ref, nu_ref, src_ref, h2t_ref, wgu_ref, bgu_ref, wd_ref, bd_ref, y4_ref,
                xbuf0, xbuf1, ybuf0, ybuf1, wgu_bf, wd_bf, gsem0, gsem1, ssem0, ssem1):
    b = pl.program_id(0)
    n_used = nu_ref[0]
    bm = EXPERT_ROWS
    F = wd_ref.shape[1]
    CH = wgu_ref.shape[1] // LANES
    n_assign = TOP_K * T
    xbuf = (xbuf0, xbuf1)
    ybuf = (ybuf0, ybuf1)
    gsem = (gsem0, gsem1)
    ssem = (ssem0, ssem1)

    def gather_start(blk, s):
        for r in range(bm):
            tok = src_ref[blk * bm + r] & (T - 1)
            pltpu.make_async_copy(h2t_ref.at[pl.ds(pl.multiple_of(tok * CH, CH), CH)],
                                  xbuf[s].at[pl.ds(r * CH, CH)], gsem[s]).start()

    def scatter_start(blk, s):
        for r in range(bm):
            a = src_ref[blk * bm + r]
            pltpu.make_async_copy(ybuf[s].at[pl.ds(r * CH, CH)],
                                  y4_ref.at[pl.ds(pl.multiple_of(a * CH, CH), CH)],
                                  ssem[s]).start()

    def gather_wait(s):
        pltpu.make_async_copy(h2t_ref.at[pl.ds(0, bm * CH)], xbuf[s], gsem[s]).wait()

    def scatter_wait(s):
        pltpu.make_async_copy(ybuf[s], y4_ref.at[pl.ds(0, bm * CH)], ssem[s]).wait()

    def load_rows(s):
        x = jnp.concatenate([xbuf[s][pl.ds(c, bm, stride=CH), :] for c in range(CH)], axis=1)
        return x.astype(BF16)

    def ffn(x):
        gu = _dot(x, wgu_bf[...]) + bgu_ref[0]
        g = jnp.minimum(gu[:, :F], SWIGLU_LIMIT)
        lin = jnp.clip(gu[:, F:], -SWIGLU_LIMIT, SWIGLU_LIMIT)
        act = (lin + 1.0) * (g * _sigmoid(SWIGLU_ALPHA * g))
        return _dot(act.astype(BF16), wd_bf[...]) + bd_ref[0]

    def store_rows(s, y):
        for c in range(CH):
            ybuf[s][pl.ds(c, bm, stride=CH), :] = y[:, c * LANES:(c + 1) * LANES]

    @pl.when((b < n_used) & ((b == 0) | (be_ref[b] != be_ref[jnp.maximum(b - 1, 0)])))
    def _():
        wgu_bf[...] = wgu_ref[0].astype(BF16)
        wd_bf[...] = wd_ref[0].astype(BF16)

    @pl.when(b == 0)
    def _():
        gather_start(0, 0)
        ybuf0[...] = jnp.zeros_like(ybuf0)
        ybuf1[...] = jnp.zeros_like(ybuf1)
        pltpu.make_async_copy(ybuf0, y4_ref.at[pl.ds((n_assign + 2 * bm) * CH, bm * CH)],
                              ssem0).start()

    for slot in (0, 1):
        other = 1 - slot

        @pl.when((b < n_used) & (b % 2 == slot))
        def _():
            gather_wait(slot)
            gather_start(b + 1, other)
            if slot == 0:
                scatter_start(jnp.where(b == 0, pl.num_programs(0), b - 1), other)
            else:
                scatter_start(b - 1, other)
            y = ffn(load_rows(slot))
            scatter_wait(slot)
            store_rows(slot, y)

    for slot in (0, 1):
        other = 1 - slot

        @pl.when((b == n_used - 1) & (b % 2 == slot))
        def _():
            scatter_start(b, slot)
            gather_wait(other)
            scatter_wait(other)
            scatter_wait(slot)


def _moe_experts(T, block_expert, n_used, src, h2t, wgu, bgu, wd, bd):
    bm = EXPERT_ROWS
    E, D, F2 = wgu.shape
    F = wd.shape[1]
    CH = D // LANES
    n_blocks = block_expert.shape[0]
    assert T & (T - 1) == 0 and src.shape[0] == (n_blocks + 1) * bm
    per_e = lambda b, be, nu, s: (be[b], 0, 0)
    return pl.pallas_call(
        functools.partial(_moe_kernel, T),
        grid_spec=pltpu.PrefetchScalarGridSpec(
            num_scalar_prefetch=3,
            grid=(n_blocks,),
            in_specs=[pl.BlockSpec(memory_space=pl.ANY),
                      pl.BlockSpec((1, D, F2), per_e),
                      pl.BlockSpec((1, 1, F2), per_e),
                      pl.BlockSpec((1, F, D), per_e),
                      pl.BlockSpec((1, 1, D), per_e)],
            out_specs=pl.BlockSpec(memory_space=pl.ANY),
            scratch_shapes=[pltpu.VMEM((bm * CH, LANES), F32)] * 4 + [
                            pltpu.VMEM((D, F2), BF16),
                            pltpu.VMEM((F, D), BF16),
                            pltpu.SemaphoreType.DMA(())] + [pltpu.SemaphoreType.DMA(())] * 3),
        out_shape=jax.ShapeDtypeStruct(((TOP_K * T + 4 * bm) * CH, LANES), F32),
        compiler_params=pltpu.CompilerParams(
            dimension_semantics=("arbitrary",), vmem_limit_bytes=VMEM_LIMIT),
        name="moe_experts",
    )(block_expert, n_used, src, h2t, wgu, bgu, wd, bd)


def _finish_kernel(x1_ref, wt_ref, g_ref, y0_ref, y1_ref, y2_ref, y3_ref, o_ref):
    tm, D = x1_ref.shape
    CH = D // LANES
    wt = wt_ref[...]
    y = x1_ref[...]
    for kk, yk_ref in enumerate((y0_ref, y1_ref, y2_ref, y3_ref)):
        yk = jnp.concatenate([yk_ref[pl.ds(c, tm, stride=CH), :] for c in range(CH)], axis=1)
        y = y + wt[:, kk:kk + 1] * yk
    o_ref[...] = _rms(y, g_ref[...])


def _finish(x1, wt, g, y4):
    T, D = x1.shape
    tm = TOKEN_TILE
    CH = D // LANES
    nt = T // tm
    choice = lambda kk: pl.BlockSpec((tm * CH, LANES), lambda i: (kk * nt + i, 0))
    return pl.pallas_call(
        _finish_kernel,
        grid=(nt,),
        in_specs=[pl.BlockSpec((tm, D), lambda i: (i, 0)),
                  pl.BlockSpec((tm, LANES), lambda i: (i, 0)),
                  pl.BlockSpec((1, D), lambda i: (0, 0)),
                  choice(0), choice(1), choice(2), choice(3)],
        out_specs=pl.BlockSpec((tm, D), lambda i: (i, 0)),
        out_shape=jax.ShapeDtypeStruct((T, D), F32),
        compiler_params=pltpu.CompilerParams(
            dimension_semantics=("arbitrary",), vmem_limit_bytes=VMEM_LIMIT),
        name="finish",
    )(x1, wt, g, y4, y4, y4, y4)


def _t5_bucket(dist):
    n = jnp.maximum(dist, 0)
    max_exact = REL_BUCKETS // 2
    nf = jnp.maximum(n, 1).astype(F32)
    large = max_exact + (jnp.log(nf / max_exact) / math.log(REL_MAX_DIST / max_exact)
                         * (REL_BUCKETS - max_exact)).astype(jnp.int32)
    large = jnp.minimum(large, REL_BUCKETS - 1)
    return jnp.where(n < max_exact, n, large)


def _bias_tiles(rel_bias):
    L = MOBA_BLOCK
    assert REL_MAX_DIST <= L
    d = np.arange(-L, 3 * L)
    onehot = (_t5_bucket(jnp.asarray(d))[:, None] == jnp.arange(REL_BUCKETS)).astype(F32)
    by_dist = jnp.dot(onehot, rel_bias.astype(F32), precision=HIGHEST)
    by_dist = jnp.where((d >= 0)[:, None], by_dist, NEG).T
    H = by_dist.shape[0]
    tiles = []
    for t in range(3):
        v = jnp.concatenate([by_dist[:, (t + 1) * L:(t + 2) * L], by_dist[:, t * L:(t + 1) * L]],
                            axis=1)
        flat = jnp.tile(v, (1, L))[:, :L * (2 * L - 1)]
        tiles.append(flat.reshape(H, L, 2 * L - 1)[:, :, :L])
    return jnp.stack(tiles) * LOG2E


def kernel(x, mem, rel_bias, norm_mix_g, w_in, b_gate, conv_w, norm_mem_g, w_mem_kv, w_br_att,
           w_br_conv, w_br_xatt, w_out, norm_ffn_g, w_router, b_router, w_gu, b_gu, w_down,
           b_down, norm_final_g):
    B, S, D = x.shape
    T = B * S
    depth = w_in.shape[0]
    assert depth == 1, "the finish step applies the final norm: single-layer configuration only"
    W = ATT_WIDTH
    XW = XATT_HEADS * XATT_HEAD_DIM
    E = w_router.shape[2]
    bm = EXPERT_ROWS
    tb = _bias_tiles(rel_bias)

    xc = x
    for l in range(depth):
        w_l = w_in[l]
        wk = w_l[:, W:2 * W].astype(BF16)
        wqvT = jnp.concatenate([w_l[:, :W], w_l[:, 2 * W:3 * W]], axis=1).T.astype(BF16)
        w_rest = w_l[:, 3 * W:].astype(BF16)
        wm = w_mem_kv[l]

        mkT, mv = _mem_kv(mem, norm_mem_g[l][None], wm[:, :XW].T.astype(BF16),
                          wm[:, XW:].astype(BF16))
        qT, k, vT_even, vT_odd, selbT = _qkv_select(xc, norm_mix_g[l][None], wk, wqvT)
        y_att = _moba(qT, selbT, k, vT_even, vT_odd, tb[:2], tb[2, :, 0, 0])

        x1, h2, idx, rank, wt, cnt = _merge(
            xc.reshape(T, D), y_att.reshape(T, W), mkT, mv, norm_mix_g[l][None], w_rest,
            conv_w[l], b_gate[l], w_br_att[l].astype(BF16), w_br_conv[l].astype(BF16),
            w_br_xatt[l].astype(BF16), w_out[l].astype(BF16), norm_ffn_g[l][None],
            w_router[l].T, b_router[l][:, None], S)

        counts = cnt[:, 0].astype(jnp.int32)
        padded = (counts + bm - 1) // bm * bm
        pad_ends = jnp.cumsum(padded).astype(jnp.int32)
        pad_starts = pad_ends - padded
        onehot = idx[..., None] == jnp.arange(E, dtype=jnp.int32)
        dest = (jnp.sum(jnp.where(onehot, pad_starts, 0), axis=-1) + rank).reshape(-1)
        n_rows = T * TOP_K + E * bm
        n_blocks = n_rows // bm
        n_used = pad_ends[-1] // bm
        blk = jnp.minimum(jnp.arange(n_blocks, dtype=jnp.int32), n_used - 1) * bm
        block_expert = jnp.minimum(jnp.sum(blk[:, None] >= pad_ends[None, :], axis=1),
                                   E - 1).astype(jnp.int32)

        src = _invert(dest, counts, pad_ends, n_rows + bm)

        y4 = _moe_experts(T, block_expert, n_used[None], src, h2, w_gu[l], b_gu[l][:, None],
                          w_down[l], b_down[l][:, None])
        xc = _finish(x1, wt, norm_final_g[None], y4).reshape(B, S, D)
    return xc
```

```python
import functools
import math

import jax
import jax.numpy as jnp
import numpy as np
from jax import lax
from jax.experimental import pallas as pl
from jax.experimental.pallas import tpu as pltpu

F32 = jnp.float32
BF16 = jnp.bfloat16
HIGHEST = lax.Precision.HIGHEST

ATT_HEADS = 8
ATT_HEAD_DIM = 64
ATT_WIDTH = ATT_HEADS * ATT_HEAD_DIM
MOBA_BLOCK = 256
MOBA_TOPK = 3
REL_BUCKETS = 32
REL_MAX_DIST = 128
XATT_HEADS = 4
XATT_HEAD_DIM = 128
N_EXPERTS = 32
TOP_K = 4
SWIGLU_LIMIT = 7.0
SWIGLU_ALPHA = 1.702
EPS = 1e-5
NEG = -1e30
LOG2E = math.log2(math.e)

LANES = 128
SEL_SLOTS = 16
TOKEN_TILE = 256
EXPERT_ROWS = 256
VMEM_LIMIT = 56 * 1024 * 1024


def _rms(x, g):
    return x * lax.rsqrt(jnp.mean(x * x, axis=-1, keepdims=True) + EPS) * g


def _dot(a, b):
    return jnp.dot(a, b, preferred_element_type=F32)


def _dot_nt(a, b):
    return lax.dot_general(a, b, (((1,), (1,)), ((), ())), preferred_element_type=F32)


def _sigmoid(x):
    return 1.0 / (1.0 + jnp.exp(-x))


def _split_bf16(x):
    hi = x.astype(BF16)
    return hi, (x - hi.astype(F32)).astype(BF16)


def _dot3(a, b, dot):
    m = a.shape[0]
    a_hi, a_lo = _split_bf16(a)
    b_hi, b_lo = _split_bf16(b)
    both = dot(jnp.concatenate([a_hi, a_lo], axis=0), b_hi)
    return both[:m] + both[m:] + dot(a_hi, b_lo)


def _mem_kv_kernel(mem_ref, g_ref, wkT_ref, wv_ref, mkT_ref, mv_ref):
    mn = _rms(mem_ref[0], g_ref[...]).astype(BF16)
    mkT_ref[0] = _dot_nt(wkT_ref[...], mn).astype(BF16)
    mv_ref[0] = _dot(mn, wv_ref[...]).astype(BF16)


def _mem_kv(mem, g, wkT, wv):
    B, M, D = mem.shape
    XW = wv.shape[1]
    return pl.pallas_call(
        _mem_kv_kernel,
        grid=(B,),
        in_specs=[pl.BlockSpec((1, M, D), lambda b: (b, 0, 0)),
                  pl.BlockSpec((1, D), lambda b: (0, 0)),
                  pl.BlockSpec((XW, D), lambda b: (0, 0)),
                  pl.BlockSpec((D, XW), lambda b: (0, 0))],
        out_specs=[pl.BlockSpec((1, XW, M), lambda b: (b, 0, 0)),
                   pl.BlockSpec((1, M, XW), lambda b: (b, 0, 0))],
        out_shape=[jax.ShapeDtypeStruct((B, XW, M), BF16),
                   jax.ShapeDtypeStruct((B, M, XW), BF16)],
        name="mem_kv",
    )(mem, g, wkT, wv)


def _qkv_select_kernel(x_ref, g_ref, wk_ref, wqvT_ref, qT_ref, k_ref, vTe_ref, vTo_ref, selbT_ref,
                       km_ref):
    i = pl.program_id(1)
    tq = x_ref.shape[1]
    W = ATT_WIDTH

    @pl.when(i == 0)
    def _():
        km_ref[...] = jnp.zeros_like(km_ref)

    h = _rms(x_ref[0], g_ref[...]).astype(BF16)
    k = _dot(h, wk_ref[...])
    k_ref[0] = k.astype(BF16)
    qvT = _dot_nt(wqvT_ref[...], h)
    qT = qvT[:W] * (1.0 / math.sqrt(ATT_HEAD_DIM))
    qT_ref[0] = (qT * LOG2E).astype(BF16)
    vT = qvT[W:]
    even_head = (lax.broadcasted_iota(jnp.int32, (W, tq), 0) // ATT_HEAD_DIM) % 2 == 0
    vTe_ref[0] = jnp.where(even_head, vT, 1.0).astype(BF16)
    vTo_ref[0] = jnp.where(even_head, 1.0, vT).astype(BF16)

    gate = _dot3(km_ref[...], qT, _dot)
    g3 = gate.reshape(ATT_HEADS, SEL_SLOTS, tq)
    jj = lax.broadcasted_iota(jnp.int32, g3.shape, 1).astype(F32)
    valid = jj < i.astype(F32)
    g3 = jnp.where(valid, g3, -jnp.inf)
    sel = jnp.zeros(g3.shape, F32)
    for _ in range(MOBA_TOPK):
        m = jnp.max(g3, axis=1, keepdims=True)
        first = jnp.min(jnp.where(g3 == m, jj, float(SEL_SLOTS)), axis=1, keepdims=True)
        pick = jj == first
        sel = jnp.where(pick, 1.0, sel)
        g3 = jnp.where(pick, -jnp.inf, g3)
    keep = ((sel > 0.5) & valid) | (jj == i.astype(F32))
    selbT_ref[0] = jnp.where(keep, 0.0, NEG).reshape(ATT_HEADS * SEL_SLOTS, tq).astype(BF16)

    kmean = jnp.sum(k, axis=0, keepdims=True) * (1.0 / tq)
    lane_head = lax.broadcasted_iota(jnp.int32, (1, W), 1) // ATT_HEAD_DIM
    for hh in range(ATT_HEADS):
        km_ref[pl.ds(hh * SEL_SLOTS + i, 1), :] = jnp.where(lane_head == hh, kmean, 0.0)


def _qkv_select(x, g, wk, wqvT):
    B, S, D = x.shape
    W = ATT_WIDTH
    tq = MOBA_BLOCK
    nb = S // tq
    assert nb <= SEL_SLOTS and ATT_HEADS * SEL_SLOTS == LANES
    by_row = pl.BlockSpec((1, tq, W), lambda b, i: (b, i, 0))
    by_col = pl.BlockSpec((1, W, tq), lambda b, i: (b, 0, i))
    return pl.pallas_call(
        _qkv_select_kernel,
        grid=(B, nb),
        in_specs=[pl.BlockSpec((1, tq, D), lambda b, i: (b, i, 0)),
                  pl.BlockSpec((1, D), lambda b, i: (0, 0)),
                  pl.BlockSpec((D, W), lambda b, i: (0, 0)),
                  pl.BlockSpec((2 * W, D), lambda b, i: (0, 0))],
        out_specs=[by_col, by_row, by_col, by_col,
                   pl.BlockSpec((1, LANES, tq), lambda b, i: (b, 0, i))],
        out_shape=[jax.ShapeDtypeStruct((B, W, S), BF16),
                   jax.ShapeDtypeStruct((B, S, W), BF16),
                   jax.ShapeDtypeStruct((B, W, S), BF16),
                   jax.ShapeDtypeStruct((B, W, S), BF16),
                   jax.ShapeDtypeStruct((B, LANES, S), BF16)],
        scratch_shapes=[pltpu.VMEM((LANES, W), F32)],
        compiler_params=pltpu.CompilerParams(
            dimension_semantics=("arbitrary", "arbitrary"), vmem_limit_bytes=VMEM_LIMIT),
        name="qkv_select",
    )(x, g, wk, wqvT)


def _moba_kernel(qT_ref, selbT_ref, k_ref, vTe_ref, vTo_ref, tbT_ref, cfar_ref, eT_ref, o_ref,
                 rhs_ref, m_ref, acc_ref):
    i = pl.program_id(1)
    tq = o_ref.shape[1]
    L = MOBA_BLOCK
    HD = ATT_HEAD_DIM
    H = ATT_HEADS
    selbT = selbT_ref[0].astype(F32)
    row = lax.broadcasted_iota(jnp.int32, (LANES, tq), 0)
    vT_refs = (vTe_ref, vTo_ref)
    slabs = [slice(p * LANES, (p + 1) * LANES) for p in range(H // 2)]

    for h in range(H):
        qTp = qT_ref[0, slabs[h // 2], :].astype(F32)
        own = (row >= HD) if h % 2 else (row < HD)
        rhs_ref[h, :LANES, :] = jnp.where(own, qTp, 0.0).astype(BF16)
        rhs_ref[h, LANES:, :] = jnp.where(row // SEL_SLOTS == h, selbT, 0.0).astype(BF16)
    m_ref[...] = jnp.full(m_ref.shape, -jnp.inf, F32)
    acc_ref[...] = jnp.zeros_like(acc_ref)

    def attend(js, tiles):
        koffs = [pl.multiple_of(j * L, L) for j in js]
        s = []
        for j, koff in zip(js, koffs):
            sj = []
            for p in range(H // 2):
                lhs = jnp.concatenate([k_ref[0, pl.ds(koff, L), slabs[p]], eT_ref[j]], axis=1)
                sj += [_dot(lhs, rhs_ref[2 * p + hh]) for hh in (0, 1)]
            s.append(sj)
        for sj, koff, tile in zip(s, koffs, tiles):
            pr, alpha = [], []
            for h in range(H):
                sh = sj[h] if tile is None else sj[h] + tbT_ref[tile, h]
                cm = jnp.max(jnp.max(sh.reshape(8, L // 8, tq), axis=0), axis=0, keepdims=True)
                m_old = m_ref[h:h + 1, :]
                if tile is None:
                    m_new = jnp.maximum(m_old, cm + cfar_ref[h])
                    pr.append(jnp.exp2(sh - (m_new - cfar_ref[h])).astype(BF16))
                else:
                    m_new = jnp.maximum(m_old, cm)
                    pr.append(jnp.exp2(sh - m_new).astype(BF16))
                m_ref[h:h + 1, :] = m_new
                alpha.append(jnp.exp2(m_old - m_new))
            for h in range(H):
                pv = _dot(vT_refs[h % 2][0, slabs[h // 2], pl.ds(koff, L)], pr[h])
                acc_ref[h] = alpha[h] * acc_ref[h] + pv

    n_far = jnp.maximum(i - 1, 0)

    def far_pair(jj, carry):
        attend([2 * jj, 2 * jj + 1], [None, None])
        return carry

    lax.fori_loop(0, n_far // 2, far_pair, 0)

    @pl.when(n_far % 2 == 1)
    def _():
        attend([n_far - 1], [None])

    @pl.when(i >= 1)
    def _():
        attend([i - 1, i], [1, 0])

    @pl.when(i == 0)
    def _():
        attend([i], [0])

    for p in range(H // 2):
        a0, a1 = acc_ref[2 * p], acc_ref[2 * p + 1]
        oT = jnp.concatenate([a0[:HD] / a0[HD:HD + 1], a1[HD:] / a1[0:1]], axis=0)
        o_ref[0, :, slabs[p]] = oT.T.astype(BF16)


def _moba(qT, selbT, k, vT_even, vT_odd, tbT, cfar):
    B, S, W = k.shape
    tq = MOBA_BLOCK
    nb = S // tq
    lane = np.arange(LANES)[None, None, :] % SEL_SLOTS
    eT_all = jnp.asarray(np.broadcast_to(lane == np.arange(SEL_SLOTS)[:, None, None],
                                         (SEL_SLOTS, MOBA_BLOCK, LANES)), BF16)
    whole_T = pl.BlockSpec((1, W, S), lambda b, i: (b, 0, 0))
    return pl.pallas_call(
        _moba_kernel,
        grid=(B, nb),
        in_specs=[pl.BlockSpec((1, W, tq), lambda b, i: (b, 0, i)),
                  pl.BlockSpec((1, LANES, tq), lambda b, i: (b, 0, i)),
                  pl.BlockSpec((1, S, W), lambda b, i: (b, 0, 0)),
                  whole_T, whole_T,
                  pl.BlockSpec(tbT.shape, lambda b, i: (0, 0, 0, 0)),
                  pl.BlockSpec(memory_space=pltpu.SMEM),
                  pl.BlockSpec(eT_all.shape, lambda b, i: (0, 0, 0))],
        out_specs=pl.BlockSpec((1, tq, W), lambda b, i: (b, i, 0)),
        out_shape=jax.ShapeDtypeStruct((B, S, W), BF16),
        scratch_shapes=[pltpu.VMEM((ATT_HEADS, 2 * LANES, tq), BF16),
                        pltpu.VMEM((ATT_HEADS, tq), F32),
                        pltpu.VMEM((ATT_HEADS, LANES, tq), F32)],
        compiler_params=pltpu.CompilerParams(
            dimension_semantics=("arbitrary", "arbitrary"), vmem_limit_bytes=VMEM_LIMIT),
        name="moba",
    )(qT, selbT, k, vT_even, vT_odd, tbT, cfar, eT_all)


def _merge_kernel(tiles_per_seq,
                  x_ref, ya_ref, mkT_ref, mv_ref, g1_ref, wr_ref, cw_ref, bg_ref,
                  wba_ref, wbc_ref, wbx_ref, wo_ref, g2_ref, wrt_ref, br_ref,
                  x1_ref, h2_ref, idx_ref, rank_ref, wt_ref, cnt_ref, zprev_ref):
    i = pl.program_id(0)
    tm, D = x_ref.shape
    CW = cw_ref.shape[1]
    XW = mv_ref.shape[2]

    @pl.when(i == 0)
    def _():
        cnt_ref[...] = jnp.zeros_like(cnt_ref)

    @pl.when(i % tiles_per_seq == 0)
    def _():
        zprev_ref[...] = jnp.zeros_like(zprev_ref)

    x = x_ref[...]
    h = _rms(x, g1_ref[...]).astype(BF16)
    pr = _dot(h, wr_ref[...])

    cb = pr[:, :CW]
    z = pr[:, CW:2 * CW] * pr[:, 2 * CW:3 * CW]
    row = lax.broadcasted_iota(jnp.int32, (tm, CW), 0)
    zp = zprev_ref[...]
    z1 = jnp.where(row == 0, zp[7:8], pltpu.roll(z, 1, 0))
    z2 = jnp.where(row == 0, zp[6:7], jnp.where(row == 1, zp[7:8], pltpu.roll(z, 2, 0)))
    zprev_ref[...] = z[tm - 8:]
    cw = cw_ref[...]
    y_conv = cb * (cw[0:1] * z2 + cw[1:2] * z1 + cw[2:3] * z)

    o0 = 3 * CW
    scale = 1.0 / math.sqrt(XATT_HEAD_DIM)
    ys = []
    for hh in range(XATT_HEADS):
        hs = slice(hh * XATT_HEAD_DIM, (hh + 1) * XATT_HEAD_DIM)
        qx = pr[:, o0 + hh * XATT_HEAD_DIM:o0 + (hh + 1) * XATT_HEAD_DIM].astype(BF16)
        s = _dot(qx, mkT_ref[0, hs, :]) * scale
        e = jnp.exp(s - jnp.max(s, axis=1, keepdims=True))
        l = jnp.sum(e, axis=1, keepdims=True)
        ys.append(_dot(e.astype(BF16), mv_ref[0, :, hs]) / l)
    y_x = jnp.concatenate(ys, axis=1)

    o1 = o0 + XW
    bg = bg_ref[...]
    merged = (_sigmoid(pr[:, o1:o1 + D] + bg[0:1]) * _dot(ya_ref[...], wba_ref[...])
              + _sigmoid(pr[:, o1 + D:o1 + 2 * D] + bg[1:2]) * _dot(y_conv.astype(BF16), wbc_ref[...])
              + _sigmoid(pr[:, o1 + 2 * D:o1 + 3 * D] + bg[2:3]) * _dot(y_x.astype(BF16), wbx_ref[...]))
    x1 = x + _dot(merged.astype(BF16), wo_ref[...])
    x1_ref[...] = x1
    h2 = _rms(x1, g2_ref[...])
    for c in range(D // LANES):
        h2_ref[pl.ds(c, tm, stride=D // LANES), :] = h2[:, c * LANES:(c + 1) * LANES]

    lg = _dot3(wrt_ref[...], h2, _dot_nt) + br_ref[...]
    E = lg.shape[0]
    ee = lax.broadcasted_iota(jnp.int32, (E, tm), 0).astype(F32)
    work = lg
    member = jnp.zeros((E, tm), F32)
    picks, vals = [], []
    for _ in range(TOP_K):
        m = jnp.max(work, axis=0, keepdims=True)
        first = jnp.min(jnp.where(work == m, ee, float(E)), axis=0, keepdims=True)
        pick = ee == first
        work = jnp.where(pick, -jnp.inf, work)
        member = jnp.where(pick, 1.0, member)
        picks.append((pick, first))
        vals.append(m)
    exps = [jnp.exp(v - vals[0]) for v in vals]
    denom = exps[0] + exps[1] + exps[2] + exps[3]

    r_i = lax.broadcasted_iota(jnp.int32, (tm, tm), 0)
    c_i = lax.broadcasted_iota(jnp.int32, (tm, tm), 1)
    upper = jnp.where(r_i < c_i, 1.0, 0.0).astype(BF16)
    before = _dot(member.astype(BF16), upper) + cnt_ref[:, 0:1]
    cnt_ref[...] = cnt_ref[...] + jnp.sum(member, axis=1, keepdims=True)

    wrow = lax.broadcasted_iota(jnp.int32, (LANES, tm), 0)
    wpad = jnp.zeros((LANES, tm), F32)
    for kk in range(TOP_K):
        pick, first = picks[kk]
        idx_ref[kk:kk + 1, :] = first.astype(jnp.int32)
        rank_ref[kk:kk + 1, :] = jnp.sum(jnp.where(pick, before, 0.0), axis=0,
                                         keepdims=True).astype(jnp.int32)
        wpad = jnp.where(wrow == kk, exps[kk] / denom, wpad)
    wt_ref[...] = wpad.T


def _merge(x2, yatt2, mkT, mv, g1, w_rest, conv_w, b_gate, wba, wbc, wbx, wo, g2, wrt, br, S):
    T, D = x2.shape
    tm = TOKEN_TILE
    nt = T // tm
    tps = S // tm
    CW = conv_w.shape[1]
    XW, M = mkT.shape[1], mkT.shape[2]
    E = wrt.shape[0]
    const = lambda shape: pl.BlockSpec(shape, lambda i: (0,) * len(shape))
    return pl.pallas_call(
        functools.partial(_merge_kernel, tps),
        grid=(nt,),
        in_specs=[pl.BlockSpec((tm, D), lambda i: (i, 0)),
                  pl.BlockSpec((tm, yatt2.shape[1]), lambda i: (i, 0)),
                  pl.BlockSpec((1, XW, M), lambda i: (i // tps, 0, 0)),
                  pl.BlockSpec((1, M, XW), lambda i: (i // tps, 0, 0)),
                  const((1, D)), const(w_rest.shape), const(conv_w.shape), const(b_gate.shape),
                  const(wba.shape), const(wbc.shape), const(wbx.shape), const(wo.shape),
                  const((1, D)), const(wrt.shape), const(br.shape)],
        out_specs=[pl.BlockSpec((tm, D), lambda i: (i, 0)),
                   pl.BlockSpec((tm * (D // LANES), LANES), lambda i: (i, 0)),
                   pl.BlockSpec((TOP_K, tm), lambda i: (0, i)),
                   pl.BlockSpec((TOP_K, tm), lambda i: (0, i)),
                   pl.BlockSpec((tm, LANES), lambda i: (i, 0)),
                   pl.BlockSpec((E, LANES), lambda i: (0, 0))],
        out_shape=[jax.ShapeDtypeStruct((T, D), F32),
                   jax.ShapeDtypeStruct((T * (D // LANES), LANES), F32),
                   jax.ShapeDtypeStruct((TOP_K, T), jnp.int32),
                   jax.ShapeDtypeStruct((TOP_K, T), jnp.int32),
                   jax.ShapeDtypeStruct((T, LANES), F32),
                   jax.ShapeDtypeStruct((E, LANES), F32)],
        scratch_shapes=[pltpu.VMEM((8, CW), F32)],
        compiler_params=pltpu.CompilerParams(
            dimension_semantics=("arbitrary",), vmem_limit_bytes=VMEM_LIMIT),
        name="merge_route",
    )(x2, yatt2, mkT, mv, g1, w_rest, conv_w, b_gate, wba, wbc, wbx, wo, g2, wrt, br)


def _invert_kernel(dest_ref, cnt_ref, pe_ref, src_ref):
    bm = EXPERT_ROWS
    n_assign = dest_ref.shape[0]
    assert bm & (bm - 1) == 0
    shift = bm.bit_length() - 1

    def pad_id(r):
        return n_assign + ((r >> shift) & 1) * bm + (r & (bm - 1))

    def put_pad(r, c):
        src_ref[r] = pad_id(r)
        return c

    for e in range(N_EXPERTS):
        first_row = (pe_ref[e - 1] if e else 0) + cnt_ref[e]
        lax.fori_loop(first_row, pe_ref[e], put_pad, 0)

    def put_pad_block(blk, c):
        for r in range(bm):
            src_ref[blk * bm + r] = n_assign + (blk % 2) * bm + r
        return c

    last_blk = src_ref.shape[0] // bm - 1
    lax.fori_loop(pe_ref[N_EXPERTS - 1] // bm, last_blk, put_pad_block, 0)
    for r in range(bm):
        src_ref[last_blk * bm + r] = n_assign + 3 * bm + r

    def put(g, c):
        for u in range(8):
            a = g * 8 + u
            src_ref[dest_ref[a]] = a
        return c

    lax.fori_loop(0, n_assign // 8, put, 0)


def _invert(dest_flat, counts, pad_ends, n_src):
    return pl.pallas_call(
        _invert_kernel,
        grid_spec=pltpu.PrefetchScalarGridSpec(
            num_scalar_prefetch=3,
            grid=(1,),
            in_specs=[],
            out_specs=pl.BlockSpec(memory_space=pltpu.SMEM)),
        out_shape=jax.ShapeDtypeStruct((n_src,), jnp.int32),
        name="invert_routing",
    )(dest_flat, counts, pad_ends)


def _moe_kernel(T, be_ref, nu_ref, src_ref, h2t_ref, wgu_ref, bgu_ref, wd_ref, bd_ref, y4_ref,
                xbuf0, xbuf1, ybuf0, ybuf1, wgu_bf, wd_bf, gsem0, gsem1, ssem0, ssem1):
    b = pl.program_id(0)
    n_used = nu_ref[0]
    bm = EXPERT_ROWS
    F = wd_ref.shape[1]
    CH = wgu_ref.shape[1] // LANES
    n_assign = TOP_K * T
    xbuf = (xbuf0, xbuf1)
    ybuf = (ybuf0, ybuf1)
    gsem = (gsem0, gsem1)
    ssem = (ssem0, ssem1)

    def gather_start(blk, s):
        for r in range(bm):
            tok = src_ref[blk * bm + r] & (T - 1)
            pltpu.make_async_copy(h2t_ref.at[pl.ds(pl.multiple_of(tok * CH, CH), CH)],
                                  xbuf[s].at[pl.ds(r * CH, CH)], gsem[s]).start()

    def scatter_start(blk, s):
        for r in range(bm):
            a = src_ref[blk * bm + r]
            pltpu.make_async_copy(ybuf[s].at[pl.ds(r * CH, CH)],
                                  y4_ref.at[pl.ds(pl.multiple_of(a * CH, CH), CH)],
                                  ssem[s]).start(priority=1)

    def gather_wait(s):
        pltpu.make_async_copy(h2t_ref.at[pl.ds(0, bm * CH)], xbuf[s], gsem[s]).wait()

    def scatter_wait(s):
        pltpu.make_async_copy(ybuf[s], y4_ref.at[pl.ds(0, bm * CH)], ssem[s]).wait()

    def load_rows(s):
        x = jnp.concatenate([xbuf[s][pl.ds(c, bm, stride=CH), :] for c in range(CH)], axis=1)
        return x.astype(BF16)

    def ffn(x):
        gu = _dot(x, wgu_bf[...]) + bgu_ref[0]
        g = jnp.minimum(gu[:, :F], SWIGLU_LIMIT)
        lin = jnp.clip(gu[:, F:], -SWIGLU_LIMIT, SWIGLU_LIMIT)
        act = (lin + 1.0) * (g * _sigmoid(SWIGLU_ALPHA * g))
        return _dot(act.astype(BF16), wd_bf[...]) + bd_ref[0]

    def store_rows(s, y):
        for c in range(CH):
            ybuf[s][pl.ds(c, bm, stride=CH), :] = y[:, c * LANES:(c + 1) * LANES]

    @pl.when((b < n_used) & ((b == 0) | (be_ref[b] != be_ref[jnp.maximum(b - 1, 0)])))
    def _():
        wgu_bf[...] = wgu_ref[0].astype(BF16)
        wd_bf[...] = wd_ref[0].astype(BF16)

    @pl.when(b == 0)
    def _():
        gather_start(0, 0)
        gather_wait(0)
        gather_start(1, 1)
        store_rows(0, ffn(load_rows(0)))

    for slot in (0, 1):
        other = 1 - slot

        @pl.when((b > 0) & (b < n_used) & (b % 2 == slot))
        def _():
            gather_start(b + 1, other)
            gather_wait(slot)

            @pl.when(b >= 2)
            def _():
                scatter_wait(slot)

            scatter_start(b - 1, other)
            store_rows(slot, ffn(load_rows(slot)))

        @pl.when((b == n_used - 1) & (b % 2 == slot))
        def _():
            scatter_start(b, slot)
            gather_wait(other)

            @pl.when(b >= 1)
            def _():
                scatter_wait(other)

            scatter_wait(slot)


def _moe_experts(T, block_expert, n_used, src, h2t, wgu, bgu, wd, bd):
    bm = EXPERT_ROWS
    E, D, F2 = wgu.shape
    F = wd.shape[1]
    CH = D // LANES
    n_blocks = block_expert.shape[0]
    assert T & (T - 1) == 0 and src.shape[0] == (n_blocks + 1) * bm
    per_e = lambda b, be, nu, s: (be[b], 0, 0)
    return pl.pallas_call(
        functools.partial(_moe_kernel, T),
        grid_spec=pltpu.PrefetchScalarGridSpec(
            num_scalar_prefetch=3,
            grid=(n_blocks,),
            in_specs=[pl.BlockSpec(memory_space=pl.ANY),
                      pl.BlockSpec((1, D, F2), per_e),
                      pl.BlockSpec((1, 1, F2), per_e),
                      pl.BlockSpec((1, F, D), per_e),
                      pl.BlockSpec((1, 1, D), per_e)],
            out_specs=pl.BlockSpec(memory_space=pl.ANY),
            scratch_shapes=[pltpu.VMEM((bm * CH, LANES), F32)] * 4 + [
                            pltpu.VMEM((D, F2), BF16),
                            pltpu.VMEM((F, D), BF16),
                            pltpu.SemaphoreType.DMA(())] + [pltpu.SemaphoreType.DMA(())] * 3),
        out_shape=jax.ShapeDtypeStruct(((TOP_K * T + 4 * bm) * CH, LANES), F32),
        compiler_params=pltpu.CompilerParams(
            dimension_semantics=("arbitrary",), vmem_limit_bytes=VMEM_LIMIT),
        name="moe_experts",
    )(block_expert, n_used, src, h2t, wgu, bgu, wd, bd)


def _finish_kernel(x1_ref, wt_ref, g_ref, y0_ref, y1_ref, y2_ref, y3_ref, o_ref):
    tm, D = x1_ref.shape
    CH = D // LANES
    wt = wt_ref[...]
    y = x1_ref[...]
    for kk, yk_ref in enumerate((y0_ref, y1_ref, y2_ref, y3_ref)):
        yk = jnp.concatenate([yk_ref[pl.ds(c, tm, stride=CH), :] for c in range(CH)], axis=1)
        y = y + wt[:, kk:kk + 1] * yk
    o_ref[...] = _rms(y, g_ref[...])


def _finish(x1, wt, g, y4):
    T, D = x1.shape
    tm = TOKEN_TILE
    CH = D // LANES
    nt = T // tm
    choice = lambda kk: pl.BlockSpec((tm * CH, LANES), lambda i: (kk * nt + i, 0))
    return pl.pallas_call(
        _finish_kernel,
        grid=(nt,),
        in_specs=[pl.BlockSpec((tm, D), lambda i: (i, 0)),
                  pl.BlockSpec((tm, LANES), lambda i: (i, 0)),
                  pl.BlockSpec((1, D), lambda i: (0, 0)),
                  choice(0), choice(1), choice(2), choice(3)],
        out_specs=pl.BlockSpec((tm, D), lambda i: (i, 0)),
        out_shape=jax.ShapeDtypeStruct((T, D), F32),
        compiler_params=pltpu.CompilerParams(
            dimension_semantics=("arbitrary",), vmem_limit_bytes=VMEM_LIMIT),
        name="finish",
    )(x1, wt, g, y4, y4, y4, y4)


def _t5_bucket(dist):
    n = jnp.maximum(dist, 0)
    max_exact = REL_BUCKETS // 2
    nf = jnp.maximum(n, 1).astype(F32)
    large = max_exact + (jnp.log(nf / max_exact) / math.log(REL_MAX_DIST / max_exact)
                         * (REL_BUCKETS - max_exact)).astype(jnp.int32)
    large = jnp.minimum(large, REL_BUCKETS - 1)
    return jnp.where(n < max_exact, n, large)


def _bias_tiles(rel_bias):
    L = MOBA_BLOCK
    assert REL_MAX_DIST <= L
    d = np.arange(-L, 3 * L)
    onehot = (_t5_bucket(jnp.asarray(d))[:, None] == jnp.arange(REL_BUCKETS)).astype(F32)
    by_dist = jnp.dot(onehot, rel_bias.astype(F32), precision=HIGHEST)
    by_dist = jnp.where((d >= 0)[:, None], by_dist, NEG).T
    H = by_dist.shape[0]
    tiles = []
    for t in range(3):
        v = jnp.concatenate([by_dist[:, (t + 1) * L:(t + 2) * L], by_dist[:, t * L:(t + 1) * L]],
                            axis=1)
        flat = jnp.tile(v, (1, L))[:, :L * (2 * L - 1)]
        tiles.append(flat.reshape(H, L, 2 * L - 1)[:, :, :L])
    return jnp.stack(tiles) * LOG2E


def kernel(x, mem, rel_bias, norm_mix_g, w_in, b_gate, conv_w, norm_mem_g, w_mem_kv, w_br_att,
           w_br_conv, w_br_xatt, w_out, norm_ffn_g, w_router, b_router, w_gu, b_gu, w_down,
           b_down, norm_final_g):
    B, S, D = x.shape
    T = B * S
    depth = w_in.shape[0]
    assert depth == 1, "the finish step applies the final norm: single-layer configuration only"
    W = ATT_WIDTH
    XW = XATT_HEADS * XATT_HEAD_DIM
    E = w_router.shape[2]
    bm = EXPERT_ROWS
    tb = _bias_tiles(rel_bias)

    xc = x
    for l in range(depth):
        w_l = w_in[l]
        wk = w_l[:, W:2 * W].astype(BF16)
        wqvT = jnp.concatenate([w_l[:, :W], w_l[:, 2 * W:3 * W]], axis=1).T.astype(BF16)
        w_rest = w_l[:, 3 * W:].astype(BF16)
        wm = w_mem_kv[l]

        mkT, mv = _mem_kv(mem, norm_mem_g[l][None], wm[:, :XW].T.astype(BF16),
                          wm[:, XW:].astype(BF16))
        qT, k, vT_even, vT_odd, selbT = _qkv_select(xc, norm_mix_g[l][None], wk, wqvT)
        y_att = _moba(qT, selbT, k, vT_even, vT_odd, tb[:2], tb[2, :, 0, 0])

        x1, h2, idx, rank, wt, cnt = _merge(
            xc.reshape(T, D), y_att.reshape(T, W), mkT, mv, norm_mix_g[l][None], w_rest,
            conv_w[l], b_gate[l], w_br_att[l].astype(BF16), w_br_conv[l].astype(BF16),
            w_br_xatt[l].astype(BF16), w_out[l].astype(BF16), norm_ffn_g[l][None],
            w_router[l].T, b_router[l][:, None], S)

        counts = cnt[:, 0].astype(jnp.int32)
        padded = (counts + bm - 1) // bm * bm
        pad_ends = jnp.cumsum(padded).astype(jnp.int32)
        pad_starts = pad_ends - padded
        onehot = idx[..., None] == jnp.arange(E, dtype=jnp.int32)
        dest = (jnp.sum(jnp.where(onehot, pad_starts, 0), axis=-1) + rank).reshape(-1)
        n_rows = T * TOP_K + E * bm
        n_blocks = n_rows // bm
        n_used = pad_ends[-1] // bm
        blk = jnp.minimum(jnp.arange(n_blocks, dtype=jnp.int32), n_used - 1) * bm
        block_expert = jnp.minimum(jnp.sum(blk[:, None] >= pad_ends[None, :], axis=1),
                                   E - 1).astype(jnp.int32)

        src = _invert(dest, counts, pad_ends, n_rows + bm)

        y4 = _moe_experts(T, block_expert, n_used[None], src, h2, w_gu[l], b_gu[l][:, None],
                          w_down[l], b_down[l][:, None])
        xc = _finish(x1, wt, norm_final_g[None], y4).reshape(B, S, D)
    return xc
```

```python
import functools
import math

import jax
import jax.numpy as jnp
import numpy as np
from jax import lax
from jax.experimental import pallas as pl
from jax.experimental.pallas import tpu as pltpu

F32 = jnp.float32
BF16 = jnp.bfloat16
HIGHEST = lax.Precision.HIGHEST

ATT_HEADS = 8
ATT_HEAD_DIM = 64
ATT_WIDTH = ATT_HEADS * ATT_HEAD_DIM
MOBA_BLOCK = 256
MOBA_TOPK = 3
REL_BUCKETS = 32
REL_MAX_DIST = 128
XATT_HEADS = 4
XATT_HEAD_DIM = 128
N_EXPERTS = 32
TOP_K = 4
SWIGLU_LIMIT = 7.0
SWIGLU_ALPHA = 1.702
EPS = 1e-5
NEG = -1e30
LOG2E = math.log2(math.e)

LANES = 128
SEL_SLOTS = 16
TOKEN_TILE = 256
EXPERT_ROWS = 256
VMEM_LIMIT = 56 * 1024 * 1024


def _rms(x, g):
    return x * lax.rsqrt(jnp.mean(x * x, axis=-1, keepdims=True) + EPS) * g


def _dot(a, b):
    return jnp.dot(a, b, preferred_element_type=F32)


def _dot_nt(a, b):
    return lax.dot_general(a, b, (((1,), (1,)), ((), ())), preferred_element_type=F32)


def _sigmoid(x):
    return 1.0 / (1.0 + jnp.exp(-x))


def _split_bf16(x):
    hi = x.astype(BF16)
    return hi, (x - hi.astype(F32)).astype(BF16)


def _dot3(a, b, dot):
    m = a.shape[0]
    a_hi, a_lo = _split_bf16(a)
    b_hi, b_lo = _split_bf16(b)
    both = dot(jnp.concatenate([a_hi, a_lo], axis=0), b_hi)
    return both[:m] + both[m:] + dot(a_hi, b_lo)


def _mem_kv_kernel(mem_ref, g_ref, wkT_ref, wv_ref, mkT_ref, mv_ref):
    mn = _rms(mem_ref[0], g_ref[...]).astype(BF16)
    mkT_ref[0] = _dot_nt(wkT_ref[...], mn).astype(BF16)
    mv_ref[0] = _dot(mn, wv_ref[...]).astype(BF16)


def _mem_kv(mem, g, wkT, wv):
    B, M, D = mem.shape
    XW = wv.shape[1]
    return pl.pallas_call(
        _mem_kv_kernel,
        grid=(B,),
        in_specs=[pl.BlockSpec((1, M, D), lambda b: (b, 0, 0)),
                  pl.BlockSpec((1, D), lambda b: (0, 0)),
                  pl.BlockSpec((XW, D), lambda b: (0, 0)),
                  pl.BlockSpec((D, XW), lambda b: (0, 0))],
        out_specs=[pl.BlockSpec((1, XW, M), lambda b: (b, 0, 0)),
                   pl.BlockSpec((1, M, XW), lambda b: (b, 0, 0))],
        out_shape=[jax.ShapeDtypeStruct((B, XW, M), BF16),
                   jax.ShapeDtypeStruct((B, M, XW), BF16)],
        name="mem_kv",
    )(mem, g, wkT, wv)


def _qkv_select_kernel(x_ref, g_ref, wk_ref, wqvT_ref, qT_ref, k_ref, vTe_ref, vTo_ref, selbT_ref,
                       km_ref):
    i = pl.program_id(1)
    tq = x_ref.shape[1]
    W = ATT_WIDTH

    @pl.when(i == 0)
    def _():
        km_ref[...] = jnp.zeros_like(km_ref)

    h = _rms(x_ref[0], g_ref[...]).astype(BF16)
    k = _dot(h, wk_ref[...])
    k_ref[0] = k.astype(BF16)
    qvT = _dot_nt(wqvT_ref[...], h)
    qT = qvT[:W] * (1.0 / math.sqrt(ATT_HEAD_DIM))
    qT_ref[0] = (qT * LOG2E).astype(BF16)
    vT = qvT[W:]
    even_head = (lax.broadcasted_iota(jnp.int32, (W, tq), 0) // ATT_HEAD_DIM) % 2 == 0
    vTe_ref[0] = jnp.where(even_head, vT, 1.0).astype(BF16)
    vTo_ref[0] = jnp.where(even_head, 1.0, vT).astype(BF16)

    gate = _dot3(km_ref[...], qT, _dot)
    g3 = gate.reshape(ATT_HEADS, SEL_SLOTS, tq)
    jj = lax.broadcasted_iota(jnp.int32, g3.shape, 1).astype(F32)
    valid = jj < i.astype(F32)
    g3 = jnp.where(valid, g3, -jnp.inf)
    sel = jnp.zeros(g3.shape, F32)
    for _ in range(MOBA_TOPK):
        m = jnp.max(g3, axis=1, keepdims=True)
        first = jnp.min(jnp.where(g3 == m, jj, float(SEL_SLOTS)), axis=1, keepdims=True)
        pick = jj == first
        sel = jnp.where(pick, 1.0, sel)
        g3 = jnp.where(pick, -jnp.inf, g3)
    keep = ((sel > 0.5) & valid) | (jj == i.astype(F32))
    selbT_ref[0] = jnp.where(keep, 0.0, NEG).reshape(ATT_HEADS * SEL_SLOTS, tq).astype(BF16)

    kmean = jnp.sum(k, axis=0, keepdims=True) * (1.0 / tq)
    lane_head = lax.broadcasted_iota(jnp.int32, (1, W), 1) // ATT_HEAD_DIM
    for hh in range(ATT_HEADS):
        km_ref[pl.ds(hh * SEL_SLOTS + i, 1), :] = jnp.where(lane_head == hh, kmean, 0.0)


def _qkv_select(x, g, wk, wqvT):
    B, S, D = x.shape
    W = ATT_WIDTH
    tq = MOBA_BLOCK
    nb = S // tq
    assert nb <= SEL_SLOTS and ATT_HEADS * SEL_SLOTS == LANES
    by_row = pl.BlockSpec((1, tq, W), lambda b, i: (b, i, 0))
    by_col = pl.BlockSpec((1, W, tq), lambda b, i: (b, 0, i))
    return pl.pallas_call(
        _qkv_select_kernel,
        grid=(B, nb),
        in_specs=[pl.BlockSpec((1, tq, D), lambda b, i: (b, i, 0)),
                  pl.BlockSpec((1, D), lambda b, i: (0, 0)),
                  pl.BlockSpec((D, W), lambda b, i: (0, 0)),
                  pl.BlockSpec((2 * W, D), lambda b, i: (0, 0))],
        out_specs=[by_col, by_row, by_col, by_col,
                   pl.BlockSpec((1, LANES, tq), lambda b, i: (b, 0, i))],
        out_shape=[jax.ShapeDtypeStruct((B, W, S), BF16),
                   jax.ShapeDtypeStruct((B, S, W), BF16),
                   jax.ShapeDtypeStruct((B, W, S), BF16),
                   jax.ShapeDtypeStruct((B, W, S), BF16),
                   jax.ShapeDtypeStruct((B, LANES, S), BF16)],
        scratch_shapes=[pltpu.VMEM((LANES, W), F32)],
        compiler_params=pltpu.CompilerParams(
            dimension_semantics=("arbitrary", "arbitrary"), vmem_limit_bytes=VMEM_LIMIT),
        name="qkv_select",
    )(x, g, wk, wqvT)


def _moba_kernel(qT_ref, selbT_ref, k_ref, vTe_ref, vTo_ref, tbT_ref, cfar_ref, eT_ref, o_ref,
                 rhs_ref, m_ref, acc_ref):
    i = pl.program_id(1)
    tq = o_ref.shape[1]
    L = MOBA_BLOCK
    HD = ATT_HEAD_DIM
    H = ATT_HEADS
    selbT = selbT_ref[0].astype(F32)
    row = lax.broadcasted_iota(jnp.int32, (LANES, tq), 0)
    vT_refs = (vTe_ref, vTo_ref)
    slabs = [slice(p * LANES, (p + 1) * LANES) for p in range(H // 2)]

    for h in range(H):
        qTp = qT_ref[0, slabs[h // 2], :].astype(F32)
        own = (row >= HD) if h % 2 else (row < HD)
        rhs_ref[h, :LANES, :] = jnp.where(own, qTp, 0.0).astype(BF16)
        rhs_ref[h, LANES:, :] = jnp.where(row // SEL_SLOTS == h, selbT, 0.0).astype(BF16)
    m_ref[...] = jnp.full(m_ref.shape, -jnp.inf, F32)
    acc_ref[...] = jnp.zeros_like(acc_ref)

    def attend(js, tiles):
        koffs = [pl.multiple_of(j * L, L) for j in js]
        s = []
        for j, koff in zip(js, koffs):
            sj = []
            for p in range(H // 2):
                lhs = jnp.concatenate([k_ref[0, pl.ds(koff, L), slabs[p]], eT_ref[j]], axis=1)
                sj += [_dot(lhs, rhs_ref[2 * p + hh]) for hh in (0, 1)]
            s.append(sj)
        for sj, koff, tile in zip(s, koffs, tiles):
            pr, alpha = [], []
            for h in range(H):
                sh = sj[h] if tile is None else sj[h] + tbT_ref[tile, h]
                cm = jnp.max(jnp.max(sh.reshape(8, L // 8, tq), axis=0), axis=0, keepdims=True)
                m_old = m_ref[h:h + 1, :]
                if tile is None:
                    m_new = jnp.maximum(m_old, cm + cfar_ref[h])
                    pr.append(jnp.exp2(sh - (m_new - cfar_ref[h])).astype(BF16))
                else:
                    m_new = jnp.maximum(m_old, cm)
                    pr.append(jnp.exp2(sh - m_new).astype(BF16))
                m_ref[h:h + 1, :] = m_new
                alpha.append(jnp.exp2(m_old - m_new))
            for h in range(H):
                pv = _dot(vT_refs[h % 2][0, slabs[h // 2], pl.ds(koff, L)], pr[h])
                acc_ref[h] = alpha[h] * acc_ref[h] + pv

    n_far = jnp.maximum(i - 1, 0)

    def far_quad(jj, carry):
        attend([4 * jj + u for u in range(4)], [None] * 4)
        return carry

    lax.fori_loop(0, n_far // 4, far_quad, 0)

    @pl.when(n_far % 4 >= 2)
    def _():
        j0 = n_far // 4 * 4
        attend([j0, j0 + 1], [None, None])

    @pl.when(n_far % 2 == 1)
    def _():
        attend([n_far - 1], [None])

    @pl.when(i >= 1)
    def _():
        attend([i - 1, i], [1, 0])

    @pl.when(i == 0)
    def _():
        attend([i], [0])

    for p in range(H // 2):
        a0, a1 = acc_ref[2 * p], acc_ref[2 * p + 1]
        oT = jnp.concatenate([a0[:HD] / a0[HD:HD + 1], a1[HD:] / a1[0:1]], axis=0)
        o_ref[0, :, slabs[p]] = oT.T.astype(BF16)


def _moba(qT, selbT, k, vT_even, vT_odd, tbT, cfar):
    B, S, W = k.shape
    tq = MOBA_BLOCK
    nb = S // tq
    lane = np.arange(LANES)[None, None, :] % SEL_SLOTS
    eT_all = jnp.asarray(np.broadcast_to(lane == np.arange(SEL_SLOTS)[:, None, None],
                                         (SEL_SLOTS, MOBA_BLOCK, LANES)), BF16)
    whole_T = pl.BlockSpec((1, W, S), lambda b, i: (b, 0, 0))
    return pl.pallas_call(
        _moba_kernel,
        grid=(B, nb),
        in_specs=[pl.BlockSpec((1, W, tq), lambda b, i: (b, 0, i)),
                  pl.BlockSpec((1, LANES, tq), lambda b, i: (b, 0, i)),
                  pl.BlockSpec((1, S, W), lambda b, i: (b, 0, 0)),
                  whole_T, whole_T,
                  pl.BlockSpec(tbT.shape, lambda b, i: (0, 0, 0, 0)),
                  pl.BlockSpec(memory_space=pltpu.SMEM),
                  pl.BlockSpec(eT_all.shape, lambda b, i: (0, 0, 0))],
        out_specs=pl.BlockSpec((1, tq, W), lambda b, i: (b, i, 0)),
        out_shape=jax.ShapeDtypeStruct((B, S, W), BF16),
        scratch_shapes=[pltpu.VMEM((ATT_HEADS, 2 * LANES, tq), BF16),
                        pltpu.VMEM((ATT_HEADS, tq), F32),
                        pltpu.VMEM((ATT_HEADS, LANES, tq), F32)],
        compiler_params=pltpu.CompilerParams(
            dimension_semantics=("arbitrary", "arbitrary"), vmem_limit_bytes=VMEM_LIMIT),
        name="moba",
    )(qT, selbT, k, vT_even, vT_odd, tbT, cfar, eT_all)


def _merge_kernel(tiles_per_seq,
                  x_ref, ya_ref, mkT_ref, mv_ref, g1_ref, wr_ref, cw_ref, bg_ref,
                  wba_ref, wbc_ref, wbx_ref, wo_ref, g2_ref, wrt_ref, br_ref,
                  x1_ref, h2_ref, idx_ref, rank_ref, wt_ref, cnt_ref, zprev_ref):
    i = pl.program_id(0)
    tm, D = x_ref.shape
    CW = cw_ref.shape[1]
    XW = mv_ref.shape[2]

    @pl.when(i == 0)
    def _():
        cnt_ref[...] = jnp.zeros_like(cnt_ref)

    @pl.when(i % tiles_per_seq == 0)
    def _():
        zprev_ref[...] = jnp.zeros_like(zprev_ref)

    x = x_ref[...]
    h = _rms(x, g1_ref[...]).astype(BF16)
    pr = _dot(h, wr_ref[...])

    cb = pr[:, :CW]
    z = pr[:, CW:2 * CW] * pr[:, 2 * CW:3 * CW]
    row = lax.broadcasted_iota(jnp.int32, (tm, CW), 0)
    zp = zprev_ref[...]
    z1 = jnp.where(row == 0, zp[7:8], pltpu.roll(z, 1, 0))
    z2 = jnp.where(row == 0, zp[6:7], jnp.where(row == 1, zp[7:8], pltpu.roll(z, 2, 0)))
    zprev_ref[...] = z[tm - 8:]
    cw = cw_ref[...]
    y_conv = cb * (cw[0:1] * z2 + cw[1:2] * z1 + cw[2:3] * z)

    o0 = 3 * CW
    scale = 1.0 / math.sqrt(XATT_HEAD_DIM)
    ys = []
    for hh in range(XATT_HEADS):
        hs = slice(hh * XATT_HEAD_DIM, (hh + 1) * XATT_HEAD_DIM)
        qx = pr[:, o0 + hh * XATT_HEAD_DIM:o0 + (hh + 1) * XATT_HEAD_DIM].astype(BF16)
        s = _dot(qx, mkT_ref[0, hs, :]) * scale
        e = jnp.exp(s - jnp.max(s, axis=1, keepdims=True))
        l = jnp.sum(e, axis=1, keepdims=True)
        ys.append(_dot(e.astype(BF16), mv_ref[0, :, hs]) / l)
    y_x = jnp.concatenate(ys, axis=1)

    o1 = o0 + XW
    bg = bg_ref[...]
    merged = (_sigmoid(pr[:, o1:o1 + D] + bg[0:1]) * _dot(ya_ref[...], wba_ref[...])
              + _sigmoid(pr[:, o1 + D:o1 + 2 * D] + bg[1:2]) * _dot(y_conv.astype(BF16), wbc_ref[...])
              + _sigmoid(pr[:, o1 + 2 * D:o1 + 3 * D] + bg[2:3]) * _dot(y_x.astype(BF16), wbx_ref[...]))
    x1 = x + _dot(merged.astype(BF16), wo_ref[...])
    x1_ref[...] = x1
    h2 = _rms(x1, g2_ref[...])
    for c in range(D // LANES):
        h2_ref[pl.ds(c, tm, stride=D // LANES), :] = h2[:, c * LANES:(c + 1) * LANES]

    lg = _dot3(wrt_ref[...], h2, _dot_nt) + br_ref[...]
    E = lg.shape[0]
    ee = lax.broadcasted_iota(jnp.int32, (E, tm), 0).astype(F32)
    work = lg
    member = jnp.zeros((E, tm), F32)
    picks, vals = [], []
    for _ in range(TOP_K):
        m = jnp.max(work, axis=0, keepdims=True)
        first = jnp.min(jnp.where(work == m, ee, float(E)), axis=0, keepdims=True)
        pick = ee == first
        work = jnp.where(pick, -jnp.inf, work)
        member = jnp.where(pick, 1.0, member)
        picks.append((pick, first))
        vals.append(m)
    exps = [jnp.exp(v - vals[0]) for v in vals]
    denom = exps[0] + exps[1] + exps[2] + exps[3]

    r_i = lax.broadcasted_iota(jnp.int32, (tm, tm), 0)
    c_i = lax.broadcasted_iota(jnp.int32, (tm, tm), 1)
    upper = jnp.where(r_i < c_i, 1.0, 0.0).astype(BF16)
    before = _dot(member.astype(BF16), upper) + cnt_ref[:, 0:1]
    cnt_ref[...] = cnt_ref[...] + jnp.sum(member, axis=1, keepdims=True)

    wrow = lax.broadcasted_iota(jnp.int32, (LANES, tm), 0)
    wpad = jnp.zeros((LANES, tm), F32)
    for kk in range(TOP_K):
        pick, first = picks[kk]
        idx_ref[kk:kk + 1, :] = first.astype(jnp.int32)
        rank_ref[kk:kk + 1, :] = jnp.sum(jnp.where(pick, before, 0.0), axis=0,
                                         keepdims=True).astype(jnp.int32)
        wpad = jnp.where(wrow == kk, exps[kk] / denom, wpad)
    wt_ref[...] = wpad.T


def _merge(x2, yatt2, mkT, mv, g1, w_rest, conv_w, b_gate, wba, wbc, wbx, wo, g2, wrt, br, S):
    T, D = x2.shape
    tm = TOKEN_TILE
    nt = T // tm
    tps = S // tm
    CW = conv_w.shape[1]
    XW, M = mkT.shape[1], mkT.shape[2]
    E = wrt.shape[0]
    const = lambda shape: pl.BlockSpec(shape, lambda i: (0,) * len(shape))
    return pl.pallas_call(
        functools.partial(_merge_kernel, tps),
        grid=(nt,),
        in_specs=[pl.BlockSpec((tm, D), lambda i: (i, 0)),
                  pl.BlockSpec((tm, yatt2.shape[1]), lambda i: (i, 0)),
                  pl.BlockSpec((1, XW, M), lambda i: (i // tps, 0, 0)),
                  pl.BlockSpec((1, M, XW), lambda i: (i // tps, 0, 0)),
                  const((1, D)), const(w_rest.shape), const(conv_w.shape), const(b_gate.shape),
                  const(wba.shape), const(wbc.shape), const(wbx.shape), const(wo.shape),
                  const((1, D)), const(wrt.shape), const(br.shape)],
        out_specs=[pl.BlockSpec((tm, D), lambda i: (i, 0)),
                   pl.BlockSpec((tm * (D // LANES), LANES), lambda i: (i, 0)),
                   pl.BlockSpec((TOP_K, tm), lambda i: (0, i)),
                   pl.BlockSpec((TOP_K, tm), lambda i: (0, i)),
                   pl.BlockSpec((tm, LANES), lambda i: (i, 0)),
                   pl.BlockSpec((E, LANES), lambda i: (0, 0))],
        out_shape=[jax.ShapeDtypeStruct((T, D), F32),
                   jax.ShapeDtypeStruct((T * (D // LANES), LANES), F32),
                   jax.ShapeDtypeStruct((TOP_K, T), jnp.int32),
                   jax.ShapeDtypeStruct((TOP_K, T), jnp.int32),
                   jax.ShapeDtypeStruct((T, LANES), F32),
                   jax.ShapeDtypeStruct((E, LANES), F32)],
        scratch_shapes=[pltpu.VMEM((8, CW), F32)],
        compiler_params=pltpu.CompilerParams(
            dimension_semantics=("arbitrary",), vmem_limit_bytes=VMEM_LIMIT),
        name="merge_route",
    )(x2, yatt2, mkT, mv, g1, w_rest, conv_w, b_gate, wba, wbc, wbx, wo, g2, wrt, br)


def _invert_kernel(dest_ref, cnt_ref, pe_ref, src_ref):
    bm = EXPERT_ROWS
    n_assign = dest_ref.shape[0]
    assert bm & (bm - 1) == 0
    shift = bm.bit_length() - 1

    def pad_id(r):
        return n_assign + ((r >> shift) & 1) * bm + (r & (bm - 1))

    def put_pad(r, c):
        src_ref[r] = pad_id(r)
        return c

    for e in range(N_EXPERTS):
        first_row = (pe_ref[e - 1] if e else 0) + cnt_ref[e]
        lax.fori_loop(first_row, pe_ref[e], put_pad, 0)

    def put_pad_block(blk, c):
        for r in range(bm):
            src_ref[blk * bm + r] = n_assign + (blk % 2) * bm + r
        return c

    last_blk = src_ref.shape[0] // bm - 1
    lax.fori_loop(pe_ref[N_EXPERTS - 1] // bm, last_blk, put_pad_block, 0)
    for r in range(bm):
        src_ref[last_blk * bm + r] = n_assign + 3 * bm + r

    def put(g, c):
        for u in range(32):
            a = g * 32 + u
            src_ref[dest_ref[a]] = a
        return c

    lax.fori_loop(0, n_assign // 32, put, 0)


def _invert(dest_flat, counts, pad_ends, n_src):
    return pl.pallas_call(
        _invert_kernel,
        grid_spec=pltpu.PrefetchScalarGridSpec(
            num_scalar_prefetch=3,
            grid=(1,),
            in_specs=[],
            out_specs=pl.BlockSpec(memory_space=pltpu.SMEM)),
        out_shape=jax.ShapeDtypeStruct((n_src,), jnp.int32),
        name="invert_routing",
    )(dest_flat, counts, pad_ends)


def _moe_kernel(T, be_ref, nu_ref, src_ref, h2t_ref, wgu_ref, bgu_ref, wd_ref, bd_ref, y4_ref,
                xbuf0, xbuf1, ybuf0, ybuf1, wgu_bf, wd_bf, gsem0, gsem1, ssem0, ssem1):
    b = pl.program_id(0)
    n_used = nu_ref[0]
    bm = EXPERT_ROWS
    F = wd_ref.shape[1]
    CH = wgu_ref.shape[1] // LANES
    xbuf = (xbuf0, xbuf1)
    ybuf = (ybuf0, ybuf1)
    gsem = (gsem0, gsem1)
    ssem = (ssem0, ssem1)

    def gather_start(blk, s):
        for r in range(bm):
            tok = src_ref[blk * bm + r] & (T - 1)
            pltpu.make_async_copy(h2t_ref.at[pl.ds(pl.multiple_of(tok * CH, CH), CH)],
                                  xbuf[s].at[pl.ds(r * CH, CH)], gsem[s]).start()

    def scatter_start(blk, s):
        for r in range(bm):
            a = src_ref[blk * bm + r]
            pltpu.make_async_copy(ybuf[s].at[pl.ds(r * CH, CH)],
                                  y4_ref.at[pl.ds(pl.multiple_of(a * CH, CH), CH)],
                                  ssem[s]).start()

    def gather_wait(s):
        pltpu.make_async_copy(h2t_ref.at[pl.ds(0, bm * CH)], xbuf[s], gsem[s]).wait()

    def scatter_wait(s):
        pltpu.make_async_copy(ybuf[s], y4_ref.at[pl.ds(0, bm * CH)], ssem[s]).wait()

    def load_rows(s):
        x = jnp.concatenate([xbuf[s][pl.ds(c, bm, stride=CH), :] for c in range(CH)], axis=1)
        return x.astype(BF16)

    def ffn(x):
        gu = _dot(x, wgu_bf[...]) + bgu_ref[0]
        g = jnp.minimum(gu[:, :F], SWIGLU_LIMIT)
        lin = jnp.clip(gu[:, F:], -SWIGLU_LIMIT, SWIGLU_LIMIT)
        act = (lin + 1.0) * (g * _sigmoid(SWIGLU_ALPHA * g))
        return _dot(act.astype(BF16), wd_bf[...]) + bd_ref[0]

    def store_rows(s, y):
        for c in range(CH):
            ybuf[s][pl.ds(c, bm, stride=CH), :] = y[:, c * LANES:(c + 1) * LANES]

    @pl.when((b < n_used) & ((b == 0) | (be_ref[b] != be_ref[jnp.maximum(b - 1, 0)])))
    def _():
        wgu_bf[...] = wgu_ref[0].astype(BF16)
        wd_bf[...] = wd_ref[0].astype(BF16)

    @pl.when(b == 0)
    def _():
        gather_start(0, 0)
        gather_wait(0)
        gather_start(1, 1)
        store_rows(0, ffn(load_rows(0)))

    for slot in (0, 1):
        other = 1 - slot

        @pl.when((b > 0) & (b < n_used) & (b % 2 == slot))
        def _():
            gather_start(b + 1, other)
            gather_wait(slot)

            @pl.when(b >= 2)
            def _():
                scatter_wait(slot)

            scatter_start(b - 1, other)
            store_rows(slot, ffn(load_rows(slot)))

        @pl.when((b == n_used - 1) & (b % 2 == slot))
        def _():
            scatter_start(b, slot)
            gather_wait(other)

            @pl.when(b >= 1)
            def _():
                scatter_wait(other)

            scatter_wait(slot)


def _moe_experts(T, block_expert, n_used, src, h2t, wgu, bgu, wd, bd):
    bm = EXPERT_ROWS
    E, D, F2 = wgu.shape
    F = wd.shape[1]
    CH = D // LANES
    n_blocks = block_expert.shape[0]
    assert T & (T - 1) == 0 and src.shape[0] == (n_blocks + 1) * bm
    per_e = lambda b, be, nu, s: (be[b], 0, 0)
    return pl.pallas_call(
        functools.partial(_moe_kernel, T),
        grid_spec=pltpu.PrefetchScalarGridSpec(
            num_scalar_prefetch=3,
            grid=(n_blocks,),
            in_specs=[pl.BlockSpec(memory_space=pl.ANY),
                      pl.BlockSpec((1, D, F2), per_e),
                      pl.BlockSpec((1, 1, F2), per_e),
                      pl.BlockSpec((1, F, D), per_e),
                      pl.BlockSpec((1, 1, D), per_e)],
            out_specs=pl.BlockSpec(memory_space=pl.ANY),
            scratch_shapes=[pltpu.VMEM((bm * CH, LANES), F32)] * 4 + [
                            pltpu.VMEM((D, F2), BF16),
                            pltpu.VMEM((F, D), BF16),
                            pltpu.SemaphoreType.DMA(())] + [pltpu.SemaphoreType.DMA(())] * 3),
        out_shape=jax.ShapeDtypeStruct(((TOP_K * T + 4 * bm) * CH, LANES), F32),
        compiler_params=pltpu.CompilerParams(
            dimension_semantics=("arbitrary",), vmem_limit_bytes=VMEM_LIMIT),
        name="moe_experts",
    )(block_expert, n_used, src, h2t, wgu, bgu, wd, bd)


def _finish_kernel(x1_ref, wt_ref, g_ref, y0_ref, y1_ref, y2_ref, y3_ref, o_ref):
    tm, D = x1_ref.shape
    CH = D // LANES
    wt = wt_ref[...]
    y = x1_ref[...]
    for kk, yk_ref in enumerate((y0_ref, y1_ref, y2_ref, y3_ref)):
        yk = jnp.concatenate([yk_ref[pl.ds(c, tm, stride=CH), :] for c in range(CH)], axis=1)
        y = y + wt[:, kk:kk + 1] * yk
    o_ref[...] = _rms(y, g_ref[...])


def _finish(x1, wt, g, y4):
    T, D = x1.shape
    tm = TOKEN_TILE
    CH = D // LANES
    nt = T // tm
    choice = lambda kk: pl.BlockSpec((tm * CH, LANES), lambda i: (kk * nt + i, 0))
    return pl.pallas_call(
        _finish_kernel,
        grid=(nt,),
        in_specs=[pl.BlockSpec((tm, D), lambda i: (i, 0)),
                  pl.BlockSpec((tm, LANES), lambda i: (i, 0)),
                  pl.BlockSpec((1, D), lambda i: (0, 0)),
                  choice(0), choice(1), choice(2), choice(3)],
        out_specs=pl.BlockSpec((tm, D), lambda i: (i, 0)),
        out_shape=jax.ShapeDtypeStruct((T, D), F32),
        compiler_params=pltpu.CompilerParams(
            dimension_semantics=("arbitrary",), vmem_limit_bytes=VMEM_LIMIT),
        name="finish",
    )(x1, wt, g, y4, y4, y4, y4)


def _t5_bucket(dist):
    n = jnp.maximum(dist, 0)
    max_exact = REL_BUCKETS // 2
    nf = jnp.maximum(n, 1).astype(F32)
    large = max_exact + (jnp.log(nf / max_exact) / math.log(REL_MAX_DIST / max_exact)
                         * (REL_BUCKETS - max_exact)).astype(jnp.int32)
    large = jnp.minimum(large, REL_BUCKETS - 1)
    return jnp.where(n < max_exact, n, large)


def _bias_tiles(rel_bias):
    L = MOBA_BLOCK
    assert REL_MAX_DIST <= L
    d = np.arange(-L, 2 * L + 1)
    onehot = (_t5_bucket(jnp.asarray(d))[:, None] == jnp.arange(REL_BUCKETS)).astype(F32)
    by_dist = jnp.dot(onehot, rel_bias.astype(F32), precision=HIGHEST)
    by_dist = jnp.where((d >= 0)[:, None], by_dist, NEG).T * LOG2E
    H = by_dist.shape[0]
    tiles = []
    for t in range(2):
        v = jnp.concatenate([by_dist[:, (t + 1) * L:(t + 2) * L], by_dist[:, t * L:(t + 1) * L]],
                            axis=1)
        flat = jnp.tile(v, (1, L))[:, :L * (2 * L - 1)]
        tiles.append(flat.reshape(H, L, 2 * L - 1)[:, :, :L])
    return jnp.stack(tiles), by_dist[:, 3 * L]


def kernel(x, mem, rel_bias, norm_mix_g, w_in, b_gate, conv_w, norm_mem_g, w_mem_kv, w_br_att,
           w_br_conv, w_br_xatt, w_out, norm_ffn_g, w_router, b_router, w_gu, b_gu, w_down,
           b_down, norm_final_g):
    B, S, D = x.shape
    T = B * S
    depth = w_in.shape[0]
    assert depth == 1, "the finish step applies the final norm: single-layer configuration only"
    W = ATT_WIDTH
    XW = XATT_HEADS * XATT_HEAD_DIM
    E = w_router.shape[2]
    bm = EXPERT_ROWS
    tb, cfar = _bias_tiles(rel_bias)

    xc = x
    for l in range(depth):
        w_l = w_in[l]
        wk = w_l[:, W:2 * W].astype(BF16)
        wqvT = jnp.concatenate([w_l[:, :W], w_l[:, 2 * W:3 * W]], axis=1).T.astype(BF16)
        w_rest = w_l[:, 3 * W:].astype(BF16)
        wm = w_mem_kv[l]

        mkT, mv = _mem_kv(mem, norm_mem_g[l][None], wm[:, :XW].T.astype(BF16),
                          wm[:, XW:].astype(BF16))
        qT, k, vT_even, vT_odd, selbT = _qkv_select(xc, norm_mix_g[l][None], wk, wqvT)
        y_att = _moba(qT, selbT, k, vT_even, vT_odd, tb, cfar)

        x1, h2, idx, rank, wt, cnt = _merge(
            xc.reshape(T, D), y_att.reshape(T, W), mkT, mv, norm_mix_g[l][None], w_rest,
            conv_w[l], b_gate[l], w_br_att[l].astype(BF16), w_br_conv[l].astype(BF16),
            w_br_xatt[l].astype(BF16), w_out[l].astype(BF16), norm_ffn_g[l][None],
            w_router[l].T, b_router[l][:, None], S)

        counts = cnt[:, 0].astype(jnp.int32)
        padded = (counts + bm - 1) // bm * bm
        pad_ends = jnp.cumsum(padded).astype(jnp.int32)
        pad_starts = pad_ends - padded
        onehot = idx[..., None] == jnp.arange(E, dtype=jnp.int32)
        dest = (jnp.sum(jnp.where(onehot, pad_starts, 0), axis=-1) + rank).reshape(-1)
        n_rows = T * TOP_K + E * bm
        n_blocks = n_rows // bm
        n_used = pad_ends[-1] // bm
        blk = jnp.minimum(jnp.arange(n_blocks, dtype=jnp.int32), n_used - 1) * bm
        block_expert = jnp.minimum(jnp.sum(blk[:, None] >= pad_ends[None, :], axis=1),
                                   E - 1).astype(jnp.int32)

        src = _invert(dest, counts, pad_ends, n_rows + bm)

        y4 = _moe_experts(T, block_expert, n_used[None], src, h2, w_gu[l], b_gu[l][:, None],
                          w_down[l], b_down[l][:, None])
        xc = _finish(x1, wt, norm_final_g[None], y4).reshape(B, S, D)
    return xc
```

```python
import functools
import math

import jax
import jax.numpy as jnp
import numpy as np
from jax import lax
from jax.experimental import pallas as pl
from jax.experimental.pallas import tpu as pltpu

F32 = jnp.float32
BF16 = jnp.bfloat16
HIGHEST = lax.Precision.HIGHEST

ATT_HEADS = 8
ATT_HEAD_DIM = 64
ATT_WIDTH = ATT_HEADS * ATT_HEAD_DIM
MOBA_BLOCK = 256
MOBA_TOPK = 3
REL_BUCKETS = 32
REL_MAX_DIST = 128
XATT_HEADS = 4
XATT_HEAD_DIM = 128
N_EXPERTS = 32
TOP_K = 4
SWIGLU_LIMIT = 7.0
SWIGLU_ALPHA = 1.702
EPS = 1e-5
NEG = -1e30
LOG2E = math.log2(math.e)

LANES = 128
SEL_SLOTS = 16
TOKEN_TILE = 256
EXPERT_ROWS = 256
VMEM_LIMIT = 56 * 1024 * 1024


def _rms(x, g):
    return x * lax.rsqrt(jnp.mean(x * x, axis=-1, keepdims=True) + EPS) * g


def _dot(a, b):
    return jnp.dot(a, b, preferred_element_type=F32)


def _dot_nt(a, b):
    return lax.dot_general(a, b, (((1,), (1,)), ((), ())), preferred_element_type=F32)


def _sigmoid(x):
    return 1.0 / (1.0 + jnp.exp(-x))


def _split_bf16(x):
    hi = x.astype(BF16)
    return hi, (x - hi.astype(F32)).astype(BF16)


def _dot3(a, b, dot):
    m = a.shape[0]
    a_hi, a_lo = _split_bf16(a)
    b_hi, b_lo = _split_bf16(b)
    both = dot(jnp.concatenate([a_hi, a_lo], axis=0), b_hi)
    return both[:m] + both[m:] + dot(a_hi, b_lo)


def _mem_kv_kernel(mem_ref, g_ref, wkT_ref, wv_ref, mkT_ref, mv_ref):
    mn = _rms(mem_ref[0], g_ref[...]).astype(BF16)
    mkT_ref[0] = _dot_nt(wkT_ref[...], mn).astype(BF16)
    mv_ref[0] = _dot(mn, wv_ref[...]).astype(BF16)


def _mem_kv(mem, g, wkT, wv):
    B, M, D = mem.shape
    XW = wv.shape[1]
    return pl.pallas_call(
        _mem_kv_kernel,
        grid=(B,),
        in_specs=[pl.BlockSpec((1, M, D), lambda b: (b, 0, 0)),
                  pl.BlockSpec((1, D), lambda b: (0, 0)),
                  pl.BlockSpec((XW, D), lambda b: (0, 0)),
                  pl.BlockSpec((D, XW), lambda b: (0, 0))],
        out_specs=[pl.BlockSpec((1, XW, M), lambda b: (b, 0, 0)),
                   pl.BlockSpec((1, M, XW), lambda b: (b, 0, 0))],
        out_shape=[jax.ShapeDtypeStruct((B, XW, M), BF16),
                   jax.ShapeDtypeStruct((B, M, XW), BF16)],
        name="mem_kv",
    )(mem, g, wkT, wv)


def _qkv_select_kernel(x_ref, g_ref, wk_ref, wqvT_ref, qT_ref, k_ref, vTe_ref, vTo_ref, selbT_ref,
                       km_ref):
    i = pl.program_id(1)
    tq = x_ref.shape[1]
    W = ATT_WIDTH

    @pl.when(i == 0)
    def _():
        km_ref[...] = jnp.zeros_like(km_ref)

    h = _rms(x_ref[0], g_ref[...]).astype(BF16)
    k = _dot(h, wk_ref[...])
    k_ref[0] = k.astype(BF16)
    qvT = _dot_nt(wqvT_ref[...], h)
    qT = qvT[:W] * (1.0 / math.sqrt(ATT_HEAD_DIM))
    qT_ref[0] = (qT * LOG2E).astype(BF16)
    vT = qvT[W:]
    even_head = (lax.broadcasted_iota(jnp.int32, (W, tq), 0) // ATT_HEAD_DIM) % 2 == 0
    vTe_ref[0] = jnp.where(even_head, vT, 1.0).astype(BF16)
    vTo_ref[0] = jnp.where(even_head, 1.0, vT).astype(BF16)

    gate = _dot3(km_ref[...], qT, _dot)
    g3 = gate.reshape(ATT_HEADS, SEL_SLOTS, tq)
    jj = lax.broadcasted_iota(jnp.int32, g3.shape, 1).astype(F32)
    valid = jj < i.astype(F32)
    g3 = jnp.where(valid, g3, -jnp.inf)
    sel = jnp.zeros(g3.shape, F32)
    for _ in range(MOBA_TOPK):
        m = jnp.max(g3, axis=1, keepdims=True)
        first = jnp.min(jnp.where(g3 == m, jj, float(SEL_SLOTS)), axis=1, keepdims=True)
        pick = jj == first
        sel = jnp.where(pick, 1.0, sel)
        g3 = jnp.where(pick, -jnp.inf, g3)
    keep = ((sel > 0.5) & valid) | (jj == i.astype(F32))
    selbT_ref[0] = jnp.where(keep, 0.0, NEG).reshape(ATT_HEADS * SEL_SLOTS, tq).astype(BF16)

    kmean = jnp.sum(k, axis=0, keepdims=True) * (1.0 / tq)
    lane_head = lax.broadcasted_iota(jnp.int32, (1, W), 1) // ATT_HEAD_DIM
    for hh in range(ATT_HEADS):
        km_ref[pl.ds(hh * SEL_SLOTS + i, 1), :] = jnp.where(lane_head == hh, kmean, 0.0)


def _qkv_select(x, g, wk, wqvT):
    B, S, D = x.shape
    W = ATT_WIDTH
    tq = MOBA_BLOCK
    nb = S // tq
    assert nb <= SEL_SLOTS and ATT_HEADS * SEL_SLOTS == LANES
    by_row = pl.BlockSpec((1, tq, W), lambda b, i: (b, i, 0))
    by_col = pl.BlockSpec((1, W, tq), lambda b, i: (b, 0, i))
    return pl.pallas_call(
        _qkv_select_kernel,
        grid=(B, nb),
        in_specs=[pl.BlockSpec((1, tq, D), lambda b, i: (b, i, 0)),
                  pl.BlockSpec((1, D), lambda b, i: (0, 0)),
                  pl.BlockSpec((D, W), lambda b, i: (0, 0)),
                  pl.BlockSpec((2 * W, D), lambda b, i: (0, 0))],
        out_specs=[by_col, by_row, by_col, by_col,
                   pl.BlockSpec((1, LANES, tq), lambda b, i: (b, 0, i))],
        out_shape=[jax.ShapeDtypeStruct((B, W, S), BF16),
                   jax.ShapeDtypeStruct((B, S, W), BF16),
                   jax.ShapeDtypeStruct((B, W, S), BF16),
                   jax.ShapeDtypeStruct((B, W, S), BF16),
                   jax.ShapeDtypeStruct((B, LANES, S), BF16)],
        scratch_shapes=[pltpu.VMEM((LANES, W), F32)],
        compiler_params=pltpu.CompilerParams(
            dimension_semantics=("arbitrary", "arbitrary"), vmem_limit_bytes=VMEM_LIMIT),
        name="qkv_select",
    )(x, g, wk, wqvT)


def _moba_kernel(qT_ref, selbT_ref, k_ref, vTe_ref, vTo_ref, tbT_ref, cfar_ref, eT_ref, o_ref,
                 rhs_ref, m_ref, acc_ref):
    i = pl.program_id(1)
    tq = o_ref.shape[1]
    L = MOBA_BLOCK
    HD = ATT_HEAD_DIM
    H = ATT_HEADS
    selbT = selbT_ref[0].astype(F32)
    row = lax.broadcasted_iota(jnp.int32, (LANES, tq), 0)
    vT_refs = (vTe_ref, vTo_ref)
    slabs = [slice(p * LANES, (p + 1) * LANES) for p in range(H // 2)]

    for h in range(H):
        qTp = qT_ref[0, slabs[h // 2], :].astype(F32)
        own = (row >= HD) if h % 2 else (row < HD)
        rhs_ref[h, :LANES, :] = jnp.where(own, qTp, 0.0).astype(BF16)
        rhs_ref[h, LANES:, :] = jnp.where(row // SEL_SLOTS == h, selbT, 0.0).astype(BF16)
    m_ref[...] = jnp.full(m_ref.shape, -jnp.inf, F32)
    acc_ref[...] = jnp.zeros_like(acc_ref)

    def attend(js, tiles):
        koffs = [pl.multiple_of(j * L, L) for j in js]
        s = []
        for j, koff in zip(js, koffs):
            sj = []
            for p in range(H // 2):
                lhs = jnp.concatenate([k_ref[0, pl.ds(koff, L), slabs[p]], eT_ref[j]], axis=1)
                sj += [_dot(lhs, rhs_ref[2 * p + hh]) for hh in (0, 1)]
            s.append(sj)
        for sj, koff, tile in zip(s, koffs, tiles):
            pr, alpha = [], []
            for h in range(H):
                sh = sj[h] if tile is None else sj[h] + tbT_ref[tile, h]
                cm = jnp.max(jnp.max(sh.reshape(8, L // 8, tq), axis=0), axis=0, keepdims=True)
                m_old = m_ref[h:h + 1, :]
                if tile is None:
                    m_new = jnp.maximum(m_old, cm + cfar_ref[h])
                    pr.append(jnp.exp2(sh - (m_new - cfar_ref[h])).astype(BF16))
                else:
                    m_new = jnp.maximum(m_old, cm)
                    pr.append(jnp.exp2(sh - m_new).astype(BF16))
                m_ref[h:h + 1, :] = m_new
                alpha.append(jnp.exp2(m_old - m_new))
            for h in range(H):
                pv = _dot(vT_refs[h % 2][0, slabs[h // 2], pl.ds(koff, L)], pr[h])
                acc_ref[h] = alpha[h] * acc_ref[h] + pv

    n_far = jnp.maximum(i - 1, 0)

    def far_quad(jj, carry):
        attend([4 * jj + u for u in range(4)], [None] * 4)
        return carry

    lax.fori_loop(0, n_far // 4, far_quad, 0)

    @pl.when(n_far % 4 >= 2)
    def _():
        j0 = n_far // 4 * 4
        attend([j0, j0 + 1], [None, None])

    @pl.when(n_far % 2 == 1)
    def _():
        attend([n_far - 1], [None])

    @pl.when(i >= 1)
    def _():
        attend([i - 1, i], [1, 0])

    @pl.when(i == 0)
    def _():
        attend([i], [0])

    for p in range(H // 2):
        a0, a1 = acc_ref[2 * p], acc_ref[2 * p + 1]
        oT = jnp.concatenate([a0[:HD] / a0[HD:HD + 1], a1[HD:] / a1[0:1]], axis=0)
        o_ref[0, :, slabs[p]] = oT.T.astype(BF16)


def _moba(qT, selbT, k, vT_even, vT_odd, tbT, cfar):
    B, S, W = k.shape
    tq = MOBA_BLOCK
    nb = S // tq
    lane = np.arange(LANES)[None, None, :] % SEL_SLOTS
    eT_all = jnp.asarray(np.broadcast_to(lane == np.arange(SEL_SLOTS)[:, None, None],
                                         (SEL_SLOTS, MOBA_BLOCK, LANES)), BF16)
    whole_T = pl.BlockSpec((1, W, S), lambda b, i: (b, 0, 0))
    return pl.pallas_call(
        _moba_kernel,
        grid=(B, nb),
        in_specs=[pl.BlockSpec((1, W, tq), lambda b, i: (b, 0, i)),
                  pl.BlockSpec((1, LANES, tq), lambda b, i: (b, 0, i)),
                  pl.BlockSpec((1, S, W), lambda b, i: (b, 0, 0)),
                  whole_T, whole_T,
                  pl.BlockSpec(tbT.shape, lambda b, i: (0, 0, 0, 0)),
                  pl.BlockSpec(memory_space=pltpu.SMEM),
                  pl.BlockSpec(eT_all.shape, lambda b, i: (0, 0, 0))],
        out_specs=pl.BlockSpec((1, tq, W), lambda b, i: (b, i, 0)),
        out_shape=jax.ShapeDtypeStruct((B, S, W), BF16),
        scratch_shapes=[pltpu.VMEM((ATT_HEADS, 2 * LANES, tq), BF16),
                        pltpu.VMEM((ATT_HEADS, tq), F32),
                        pltpu.VMEM((ATT_HEADS, LANES, tq), F32)],
        compiler_params=pltpu.CompilerParams(
            dimension_semantics=("arbitrary", "arbitrary"), vmem_limit_bytes=VMEM_LIMIT),
        name="moba",
    )(qT, selbT, k, vT_even, vT_odd, tbT, cfar, eT_all)


def _merge_kernel(tiles_per_seq,
                  x_ref, ya_ref, mkT_ref, mv_ref, g1_ref, wr_ref, cw_ref, bg_ref,
                  wba_ref, wbc_ref, wbx_ref, wo_ref, g2_ref, wrt_ref, br_ref,
                  x1_ref, h2_ref, idx_ref, rank_ref, wt_ref, cnt_ref, zprev_ref):
    i = pl.program_id(0)
    tm, D = x_ref.shape
    CW = cw_ref.shape[1]
    XW = mv_ref.shape[2]

    @pl.when(i == 0)
    def _():
        cnt_ref[...] = jnp.zeros_like(cnt_ref)

    @pl.when(i % tiles_per_seq == 0)
    def _():
        zprev_ref[...] = jnp.zeros_like(zprev_ref)

    x = x_ref[...]
    h = _rms(x, g1_ref[...]).astype(BF16)
    pr = _dot(h, wr_ref[...])

    cb = pr[:, :CW]
    z = pr[:, CW:2 * CW] * pr[:, 2 * CW:3 * CW]
    row = lax.broadcasted_iota(jnp.int32, (tm, CW), 0)
    zp = zprev_ref[...]
    z1 = jnp.where(row == 0, zp[7:8], pltpu.roll(z, 1, 0))
    z2 = jnp.where(row == 0, zp[6:7], jnp.where(row == 1, zp[7:8], pltpu.roll(z, 2, 0)))
    zprev_ref[...] = z[tm - 8:]
    cw = cw_ref[...]
    y_conv = cb * (cw[0:1] * z2 + cw[1:2] * z1 + cw[2:3] * z)

    o0 = 3 * CW
    scale = 1.0 / math.sqrt(XATT_HEAD_DIM)
    ys = []
    for hh in range(XATT_HEADS):
        hs = slice(hh * XATT_HEAD_DIM, (hh + 1) * XATT_HEAD_DIM)
        qx = pr[:, o0 + hh * XATT_HEAD_DIM:o0 + (hh + 1) * XATT_HEAD_DIM].astype(BF16)
        s = _dot(qx, mkT_ref[0, hs, :]) * scale
        e = jnp.exp(s - jnp.max(s, axis=1, keepdims=True))
        l = jnp.sum(e, axis=1, keepdims=True)
        ys.append(_dot(e.astype(BF16), mv_ref[0, :, hs]) / l)
    y_x = jnp.concatenate(ys, axis=1)

    o1 = o0 + XW
    bg = bg_ref[...]
    merged = (_sigmoid(pr[:, o1:o1 + D] + bg[0:1]) * _dot(ya_ref[...], wba_ref[...])
              + _sigmoid(pr[:, o1 + D:o1 + 2 * D] + bg[1:2]) * _dot(y_conv.astype(BF16), wbc_ref[...])
              + _sigmoid(pr[:, o1 + 2 * D:o1 + 3 * D] + bg[2:3]) * _dot(y_x.astype(BF16), wbx_ref[...]))
    x1 = x + _dot(merged.astype(BF16), wo_ref[...])
    x1_ref[...] = x1
    h2 = _rms(x1, g2_ref[...])
    for c in range(D // LANES):
        h2_ref[pl.ds(c, tm, stride=D // LANES), :] = h2[:, c * LANES:(c + 1) * LANES]

    lg = _dot3(wrt_ref[...], h2, _dot_nt) + br_ref[...]
    E = lg.shape[0]
    ee = lax.broadcasted_iota(jnp.int32, (E, tm), 0).astype(F32)
    work = lg
    member = jnp.zeros((E, tm), F32)
    picks, vals = [], []
    for _ in range(TOP_K):
        m = jnp.max(work, axis=0, keepdims=True)
        first = jnp.min(jnp.where(work == m, ee, float(E)), axis=0, keepdims=True)
        pick = ee == first
        work = jnp.where(pick, -jnp.inf, work)
        member = jnp.where(pick, 1.0, member)
        picks.append((pick, first))
        vals.append(m)
    exps = [jnp.exp(v - vals[0]) for v in vals]
    denom = exps[0] + exps[1] + exps[2] + exps[3]

    r_i = lax.broadcasted_iota(jnp.int32, (tm, tm), 0)
    c_i = lax.broadcasted_iota(jnp.int32, (tm, tm), 1)
    upper = jnp.where(r_i < c_i, 1.0, 0.0).astype(BF16)
    before = _dot(member.astype(BF16), upper) + cnt_ref[:, 0:1]
    cnt_ref[...] = cnt_ref[...] + jnp.sum(member, axis=1, keepdims=True)

    wrow = lax.broadcasted_iota(jnp.int32, (LANES, tm), 0)
    wpad = jnp.zeros((LANES, tm), F32)
    for kk in range(TOP_K):
        pick, first = picks[kk]
        idx_ref[kk:kk + 1, :] = first.astype(jnp.int32)
        rank_ref[kk:kk + 1, :] = jnp.sum(jnp.where(pick, before, 0.0), axis=0,
                                         keepdims=True).astype(jnp.int32)
        wpad = jnp.where(wrow == kk, exps[kk] / denom, wpad)
    wt_ref[...] = wpad.T


def _merge(x2, yatt2, mkT, mv, g1, w_rest, conv_w, b_gate, wba, wbc, wbx, wo, g2, wrt, br, S):
    T, D = x2.shape
    tm = TOKEN_TILE
    nt = T // tm
    tps = S // tm
    CW = conv_w.shape[1]
    XW, M = mkT.shape[1], mkT.shape[2]
    E = wrt.shape[0]
    const = lambda shape: pl.BlockSpec(shape, lambda i: (0,) * len(shape))
    return pl.pallas_call(
        functools.partial(_merge_kernel, tps),
        grid=(nt,),
        in_specs=[pl.BlockSpec((tm, D), lambda i: (i, 0)),
                  pl.BlockSpec((tm, yatt2.shape[1]), lambda i: (i, 0)),
                  pl.BlockSpec((1, XW, M), lambda i: (i // tps, 0, 0)),
                  pl.BlockSpec((1, M, XW), lambda i: (i // tps, 0, 0)),
                  const((1, D)), const(w_rest.shape), const(conv_w.shape), const(b_gate.shape),
                  const(wba.shape), const(wbc.shape), const(wbx.shape), const(wo.shape),
                  const((1, D)), const(wrt.shape), const(br.shape)],
        out_specs=[pl.BlockSpec((tm, D), lambda i: (i, 0)),
                   pl.BlockSpec((tm * (D // LANES), LANES), lambda i: (i, 0)),
                   pl.BlockSpec((TOP_K, tm), lambda i: (0, i)),
                   pl.BlockSpec((TOP_K, tm), lambda i: (0, i)),
                   pl.BlockSpec((tm, LANES), lambda i: (i, 0)),
                   pl.BlockSpec((E, LANES), lambda i: (0, 0))],
        out_shape=[jax.ShapeDtypeStruct((T, D), F32),
                   jax.ShapeDtypeStruct((T * (D // LANES), LANES), F32),
                   jax.ShapeDtypeStruct((TOP_K, T), jnp.int32),
                   jax.ShapeDtypeStruct((TOP_K, T), jnp.int32),
                   jax.ShapeDtypeStruct((T, LANES), F32),
                   jax.ShapeDtypeStruct((E, LANES), F32)],
        scratch_shapes=[pltpu.VMEM((8, CW), F32)],
        compiler_params=pltpu.CompilerParams(
            dimension_semantics=("arbitrary",), vmem_limit_bytes=VMEM_LIMIT),
        name="merge_route",
    )(x2, yatt2, mkT, mv, g1, w_rest, conv_w, b_gate, wba, wbc, wbx, wo, g2, wrt, br)


def _invert_kernel(dest_ref, cnt_ref, pe_ref, src_ref):
    bm = EXPERT_ROWS
    n_assign = dest_ref.shape[0]
    assert bm & (bm - 1) == 0
    shift = bm.bit_length() - 1

    def pad_id(r):
        return n_assign + ((r >> shift) & 1) * bm + (r & (bm - 1))

    def put_pad(r, c):
        src_ref[r] = pad_id(r)
        return c

    for e in range(N_EXPERTS):
        first_row = (pe_ref[e - 1] if e else 0) + cnt_ref[e]
        lax.fori_loop(first_row, pe_ref[e], put_pad, 0)

    def put_pad_block(blk, c):
        for r in range(bm):
            src_ref[blk * bm + r] = n_assign + (blk % 2) * bm + r
        return c

    last_blk = src_ref.shape[0] // bm - 1
    lax.fori_loop(pe_ref[N_EXPERTS - 1] // bm, last_blk, put_pad_block, 0)
    for r in range(bm):
        src_ref[last_blk * bm + r] = n_assign + 3 * bm + r

    def put(g, c):
        for u in range(32):
            a = g * 32 + u
            src_ref[dest_ref[a]] = a
        return c

    lax.fori_loop(0, n_assign // 32, put, 0)


def _invert(dest_flat, counts, pad_ends, n_src):
    return pl.pallas_call(
        _invert_kernel,
        grid_spec=pltpu.PrefetchScalarGridSpec(
            num_scalar_prefetch=3,
            grid=(1,),
            in_specs=[],
            out_specs=pl.BlockSpec(memory_space=pltpu.SMEM)),
        out_shape=jax.ShapeDtypeStruct((n_src,), jnp.int32),
        name="invert_routing",
    )(dest_flat, counts, pad_ends)


def _moe_kernel(T, be_ref, nxt_ref, eord_ref, nu_ref, src_ref, h2t_ref, wgu_ref, bgu_ref, wd_ref,
                bd_ref, y4_ref, xbuf0, xbuf1, ybuf0, ybuf1, wgu_f32, wd_f32, wgu_bf, wd_bf,
                gsem0, gsem1, ssem0, ssem1, wsem):
    b = pl.program_id(0)
    n_used = nu_ref[0]
    bm = EXPERT_ROWS
    F = wd_bf.shape[0]
    CH = wd_bf.shape[1] // LANES
    xbuf = (xbuf0, xbuf1)
    ybuf = (ybuf0, ybuf1)
    gsem = (gsem0, gsem1)
    ssem = (ssem0, ssem1)

    def gather_start(blk, s):
        for r in range(bm):
            tok = src_ref[blk * bm + r] & (T - 1)
            pltpu.make_async_copy(h2t_ref.at[pl.ds(pl.multiple_of(tok * CH, CH), CH)],
                                  xbuf[s].at[pl.ds(r * CH, CH)], gsem[s]).start()

    def scatter_start(blk, s):
        for r in range(bm):
            a = src_ref[blk * bm + r]
            pltpu.make_async_copy(ybuf[s].at[pl.ds(r * CH, CH)],
                                  y4_ref.at[pl.ds(pl.multiple_of(a * CH, CH), CH)],
                                  ssem[s]).start()

    def gather_wait(s):
        pltpu.make_async_copy(h2t_ref.at[pl.ds(0, bm * CH)], xbuf[s], gsem[s]).wait()

    def scatter_wait(s):
        pltpu.make_async_copy(ybuf[s], y4_ref.at[pl.ds(0, bm * CH)], ssem[s]).wait()

    def load_rows(s):
        x = jnp.concatenate([xbuf[s][pl.ds(c, bm, stride=CH), :] for c in range(CH)], axis=1)
        return x.astype(BF16)

    def ffn(x):
        gu = _dot(x, wgu_bf[...]) + bgu_ref[0]
        g = jnp.minimum(gu[:, :F], SWIGLU_LIMIT)
        lin = jnp.clip(gu[:, F:], -SWIGLU_LIMIT, SWIGLU_LIMIT)
        act = (lin + 1.0) * (g * _sigmoid(SWIGLU_ALPHA * g))
        return _dot(act.astype(BF16), wd_bf[...]) + bd_ref[0]

    def store_rows(s, y):
        for c in range(CH):
            ybuf[s][pl.ds(c, bm, stride=CH), :] = y[:, c * LANES:(c + 1) * LANES]

    def weight_copies(e, ws):
        return (pltpu.make_async_copy(wgu_ref.at[e], wgu_f32.at[ws], wsem.at[ws]),
                pltpu.make_async_copy(wd_ref.at[e], wd_f32.at[ws], wsem.at[ws]))

    @pl.when((b < n_used) & ((b == 0) | (be_ref[b] != be_ref[jnp.maximum(b - 1, 0)])))
    def _():
        e = be_ref[b]
        ws = eord_ref[b] % 2

        @pl.when(b == 0)
        def _():
            for cp in weight_copies(e, ws):
                cp.start()

        for cp in weight_copies(e, ws):
            cp.wait()
        wgu_bf[...] = wgu_f32[ws].astype(BF16)
        wd_bf[...] = wd_f32[ws].astype(BF16)

        @pl.when(nxt_ref[b] != e)
        def _():
            for cp in weight_copies(nxt_ref[b], 1 - ws):
                cp.start()

    @pl.when(b == 0)
    def _():
        gather_start(0, 0)
        gather_wait(0)
        gather_start(1, 1)
        store_rows(0, ffn(load_rows(0)))

    for slot in (0, 1):
        other = 1 - slot

        @pl.when((b > 0) & (b < n_used) & (b % 2 == slot))
        def _():
            gather_start(b + 1, other)
            gather_wait(slot)

            @pl.when(b >= 2)
            def _():
                scatter_wait(slot)

            scatter_start(b - 1, other)
            store_rows(slot, ffn(load_rows(slot)))

        @pl.when((b == n_used - 1) & (b % 2 == slot))
        def _():
            scatter_start(b, slot)
            gather_wait(other)

            @pl.when(b >= 1)
            def _():
                scatter_wait(other)

            scatter_wait(slot)


def _moe_experts(T, block_expert, next_expert, expert_ord, n_used, src, h2t, wgu, bgu, wd, bd):
    bm = EXPERT_ROWS
    E, D, F2 = wgu.shape
    F = wd.shape[1]
    CH = D // LANES
    n_blocks = block_expert.shape[0]
    assert T & (T - 1) == 0 and src.shape[0] == (n_blocks + 1) * bm
    per_e = lambda b, be, nxt, eord, nu, s: (be[b], 0, 0)
    return pl.pallas_call(
        functools.partial(_moe_kernel, T),
        grid_spec=pltpu.PrefetchScalarGridSpec(
            num_scalar_prefetch=5,
            grid=(n_blocks,),
            in_specs=[pl.BlockSpec(memory_space=pl.ANY),
                      pl.BlockSpec(memory_space=pl.ANY),
                      pl.BlockSpec((1, 1, F2), per_e),
                      pl.BlockSpec(memory_space=pl.ANY),
                      pl.BlockSpec((1, 1, D), per_e)],
            out_specs=pl.BlockSpec(memory_space=pl.ANY),
            scratch_shapes=[pltpu.VMEM((bm * CH, LANES), F32)] * 4 + [
                            pltpu.VMEM((2, D, F2), F32),
                            pltpu.VMEM((2, F, D), F32),
                            pltpu.VMEM((D, F2), BF16),
                            pltpu.VMEM((F, D), BF16)] + [pltpu.SemaphoreType.DMA(())] * 4 + [
                            pltpu.SemaphoreType.DMA((2,))]),
        out_shape=jax.ShapeDtypeStruct(((TOP_K * T + 4 * bm) * CH, LANES), F32),
        compiler_params=pltpu.CompilerParams(
            dimension_semantics=("arbitrary",), vmem_limit_bytes=VMEM_LIMIT),
        name="moe_experts",
    )(block_expert, next_expert, expert_ord, n_used, src, h2t, wgu, bgu, wd, bd)


def _finish_kernel(x1_ref, wt_ref, g_ref, y0_ref, y1_ref, y2_ref, y3_ref, o_ref):
    tm, D = x1_ref.shape
    CH = D // LANES
    wt = wt_ref[...]
    y = x1_ref[...]
    for kk, yk_ref in enumerate((y0_ref, y1_ref, y2_ref, y3_ref)):
        yk = jnp.concatenate([yk_ref[pl.ds(c, tm, stride=CH), :] for c in range(CH)], axis=1)
        y = y + wt[:, kk:kk + 1] * yk
    o_ref[...] = _rms(y, g_ref[...])


def _finish(x1, wt, g, y4):
    T, D = x1.shape
    tm = TOKEN_TILE
    CH = D // LANES
    nt = T // tm
    choice = lambda kk: pl.BlockSpec((tm * CH, LANES), lambda i: (kk * nt + i, 0))
    return pl.pallas_call(
        _finish_kernel,
        grid=(nt,),
        in_specs=[pl.BlockSpec((tm, D), lambda i: (i, 0)),
                  pl.BlockSpec((tm, LANES), lambda i: (i, 0)),
                  pl.BlockSpec((1, D), lambda i: (0, 0)),
                  choice(0), choice(1), choice(2), choice(3)],
        out_specs=pl.BlockSpec((tm, D), lambda i: (i, 0)),
        out_shape=jax.ShapeDtypeStruct((T, D), F32),
        compiler_params=pltpu.CompilerParams(
            dimension_semantics=("arbitrary",), vmem_limit_bytes=VMEM_LIMIT),
        name="finish",
    )(x1, wt, g, y4, y4, y4, y4)


def _t5_bucket(dist):
    n = jnp.maximum(dist, 0)
    max_exact = REL_BUCKETS // 2
    nf = jnp.maximum(n, 1).astype(F32)
    large = max_exact + (jnp.log(nf / max_exact) / math.log(REL_MAX_DIST / max_exact)
                         * (REL_BUCKETS - max_exact)).astype(jnp.int32)
    large = jnp.minimum(large, REL_BUCKETS - 1)
    return jnp.where(n < max_exact, n, large)


def _bias_tiles(rel_bias):
    L = MOBA_BLOCK
    assert REL_MAX_DIST <= L
    d = np.arange(-L, 2 * L + 1)
    onehot = (_t5_bucket(jnp.asarray(d))[:, None] == jnp.arange(REL_BUCKETS)).astype(F32)
    by_dist = jnp.dot(onehot, rel_bias.astype(F32), precision=HIGHEST)
    by_dist = jnp.where((d >= 0)[:, None], by_dist, NEG).T * LOG2E
    H = by_dist.shape[0]
    tiles = []
    for t in range(2):
        v = jnp.concatenate([by_dist[:, (t + 1) * L:(t + 2) * L], by_dist[:, t * L:(t + 1) * L]],
                            axis=1)
        flat = jnp.tile(v, (1, L))[:, :L * (2 * L - 1)]
        tiles.append(flat.reshape(H, L, 2 * L - 1)[:, :, :L])
    return jnp.stack(tiles), by_dist[:, 3 * L]


def kernel(x, mem, rel_bias, norm_mix_g, w_in, b_gate, conv_w, norm_mem_g, w_mem_kv, w_br_att,
           w_br_conv, w_br_xatt, w_out, norm_ffn_g, w_router, b_router, w_gu, b_gu, w_down,
           b_down, norm_final_g):
    B, S, D = x.shape
    T = B * S
    depth = w_in.shape[0]
    assert depth == 1, "the finish step applies the final norm: single-layer configuration only"
    W = ATT_WIDTH
    XW = XATT_HEADS * XATT_HEAD_DIM
    E = w_router.shape[2]
    bm = EXPERT_ROWS
    tb, cfar = _bias_tiles(rel_bias)

    xc = x
    for l in range(depth):
        w_l = w_in[l]
        wk = w_l[:, W:2 * W].astype(BF16)
        wqvT = jnp.concatenate([w_l[:, :W], w_l[:, 2 * W:3 * W]], axis=1).T.astype(BF16)
        w_rest = w_l[:, 3 * W:].astype(BF16)
        wm = w_mem_kv[l]

        mkT, mv = _mem_kv(mem, norm_mem_g[l][None], wm[:, :XW].T.astype(BF16),
                          wm[:, XW:].astype(BF16))
        qT, k, vT_even, vT_odd, selbT = _qkv_select(xc, norm_mix_g[l][None], wk, wqvT)
        y_att = _moba(qT, selbT, k, vT_even, vT_odd, tb, cfar)

        x1, h2, idx, rank, wt, cnt = _merge(
            xc.reshape(T, D), y_att.reshape(T, W), mkT, mv, norm_mix_g[l][None], w_rest,
            conv_w[l], b_gate[l], w_br_att[l].astype(BF16), w_br_conv[l].astype(BF16),
            w_br_xatt[l].astype(BF16), w_out[l].astype(BF16), norm_ffn_g[l][None],
            w_router[l].T, b_router[l][:, None], S)

        counts = cnt[:, 0].astype(jnp.int32)
        padded = (counts + bm - 1) // bm * bm
        pad_ends = jnp.cumsum(padded).astype(jnp.int32)
        pad_starts = pad_ends - padded
        onehot = idx[..., None] == jnp.arange(E, dtype=jnp.int32)
        dest = (jnp.sum(jnp.where(onehot, pad_starts, 0), axis=-1) + rank).reshape(-1)
        n_rows = T * TOP_K + E * bm
        n_blocks = n_rows // bm
        n_used = pad_ends[-1] // bm
        blk = jnp.minimum(jnp.arange(n_blocks, dtype=jnp.int32), n_used - 1) * bm
        block_expert = jnp.minimum(jnp.sum(blk[:, None] >= pad_ends[None, :], axis=1),
                                   E - 1).astype(jnp.int32)

        src = _invert(dest, counts, pad_ends, n_rows + bm)

        next_expert = block_expert[jnp.minimum(pad_ends[block_expert] // bm, n_used - 1)]
        first = jnp.concatenate([jnp.ones((1,), bool), block_expert[1:] != block_expert[:-1]])
        expert_ord = jnp.cumsum(first.astype(jnp.int32)) - 1

        y4 = _moe_experts(T, block_expert, next_expert, expert_ord, n_used[None], src, h2, w_gu[l],
                          b_gu[l][:, None], w_down[l], b_down[l][:, None])
        xc = _finish(x1, wt, norm_final_g[None], y4).reshape(B, S, D)
    return xc
```

```python
import functools
import math

import jax
import jax.numpy as jnp
import numpy as np
from jax import lax
from jax.experimental import pallas as pl
from jax.experimental.pallas import tpu as pltpu

F32 = jnp.float32
BF16 = jnp.bfloat16
HIGHEST = lax.Precision.HIGHEST

ATT_HEADS = 8
ATT_HEAD_DIM = 64
ATT_WIDTH = ATT_HEADS * ATT_HEAD_DIM
MOBA_BLOCK = 256
MOBA_TOPK = 3
REL_BUCKETS = 32
REL_MAX_DIST = 128
XATT_HEADS = 4
XATT_HEAD_DIM = 128
N_EXPERTS = 32
TOP_K = 4
SWIGLU_LIMIT = 7.0
SWIGLU_ALPHA = 1.702
EPS = 1e-5
NEG = -1e30
LOG2E = math.log2(math.e)

LANES = 128
SEL_SLOTS = 16
TOKEN_TILE = 256
EXPERT_ROWS = 256
VMEM_LIMIT = 56 * 1024 * 1024


def _rms(x, g):
    return x * lax.rsqrt(jnp.mean(x * x, axis=-1, keepdims=True) + EPS) * g


def _dot(a, b):
    return jnp.dot(a, b, preferred_element_type=F32)


def _dot_nt(a, b):
    return lax.dot_general(a, b, (((1,), (1,)), ((), ())), preferred_element_type=F32)


def _sigmoid(x):
    return 1.0 / (1.0 + jnp.exp(-x))


def _split_bf16(x):
    hi = x.astype(BF16)
    return hi, (x - hi.astype(F32)).astype(BF16)


def _dot3(a, b, dot):
    m = a.shape[0]
    a_hi, a_lo = _split_bf16(a)
    b_hi, b_lo = _split_bf16(b)
    both = dot(jnp.concatenate([a_hi, a_lo], axis=0), b_hi)
    return both[:m] + both[m:] + dot(a_hi, b_lo)


def _mem_kv_kernel(mem_ref, g_ref, wkT_ref, wv_ref, mkT_ref, mv_ref):
    mn = _rms(mem_ref[0], g_ref[...]).astype(BF16)
    mkT_ref[0] = _dot_nt(wkT_ref[...], mn).astype(BF16)
    mv_ref[0] = _dot(mn, wv_ref[...]).astype(BF16)


def _mem_kv(mem, g, wkT, wv):
    B, M, D = mem.shape
    XW = wv.shape[1]
    return pl.pallas_call(
        _mem_kv_kernel,
        grid=(B,),
        in_specs=[pl.BlockSpec((1, M, D), lambda b: (b, 0, 0)),
                  pl.BlockSpec((1, D), lambda b: (0, 0)),
                  pl.BlockSpec((XW, D), lambda b: (0, 0)),
                  pl.BlockSpec((D, XW), lambda b: (0, 0))],
        out_specs=[pl.BlockSpec((1, XW, M), lambda b: (b, 0, 0)),
                   pl.BlockSpec((1, M, XW), lambda b: (b, 0, 0))],
        out_shape=[jax.ShapeDtypeStruct((B, XW, M), BF16),
                   jax.ShapeDtypeStruct((B, M, XW), BF16)],
        name="mem_kv",
    )(mem, g, wkT, wv)


def _qkv_select_kernel(x_ref, g_ref, wk_ref, wqvT_ref, qT_ref, k_ref, vTe_ref, vTo_ref, selbT_ref,
                       km_ref):
    i = pl.program_id(1)
    tq = x_ref.shape[1]
    W = ATT_WIDTH

    @pl.when(i == 0)
    def _():
        km_ref[...] = jnp.zeros_like(km_ref)

    h = _rms(x_ref[0], g_ref[...]).astype(BF16)
    k = _dot(h, wk_ref[...])
    k_ref[0] = k.astype(BF16)
    qvT = _dot_nt(wqvT_ref[...], h)
    qT = qvT[:W] * (1.0 / math.sqrt(ATT_HEAD_DIM))
    qT_ref[0] = (qT * LOG2E).astype(BF16)
    vT = qvT[W:]
    even_head = (lax.broadcasted_iota(jnp.int32, (W, tq), 0) // ATT_HEAD_DIM) % 2 == 0
    vTe_ref[0] = jnp.where(even_head, vT, 1.0).astype(BF16)
    vTo_ref[0] = jnp.where(even_head, 1.0, vT).astype(BF16)

    gate = _dot3(km_ref[...], qT, _dot)
    g3 = gate.reshape(ATT_HEADS, SEL_SLOTS, tq)
    jj = lax.broadcasted_iota(jnp.int32, g3.shape, 1).astype(F32)
    valid = jj < i.astype(F32)
    g3 = jnp.where(valid, g3, -jnp.inf)
    sel = jnp.zeros(g3.shape, F32)
    for _ in range(MOBA_TOPK):
        m = jnp.max(g3, axis=1, keepdims=True)
        first = jnp.min(jnp.where(g3 == m, jj, float(SEL_SLOTS)), axis=1, keepdims=True)
        pick = jj == first
        sel = jnp.where(pick, 1.0, sel)
        g3 = jnp.where(pick, -jnp.inf, g3)
    keep = ((sel > 0.5) & valid) | (jj == i.astype(F32))
    selbT_ref[0] = jnp.where(keep, 0.0, NEG).reshape(ATT_HEADS * SEL_SLOTS, tq).astype(BF16)

    kmean = jnp.sum(k, axis=0, keepdims=True) * (1.0 / tq)
    lane_head = lax.broadcasted_iota(jnp.int32, (1, W), 1) // ATT_HEAD_DIM
    for hh in range(ATT_HEADS):
        km_ref[pl.ds(hh * SEL_SLOTS + i, 1), :] = jnp.where(lane_head == hh, kmean, 0.0)


def _qkv_select(x, g, wk, wqvT):
    B, S, D = x.shape
    W = ATT_WIDTH
    tq = MOBA_BLOCK
    nb = S // tq
    assert nb <= SEL_SLOTS and ATT_HEADS * SEL_SLOTS == LANES
    by_row = pl.BlockSpec((1, tq, W), lambda b, i: (b, i, 0))
    by_col = pl.BlockSpec((1, W, tq), lambda b, i: (b, 0, i))
    return pl.pallas_call(
        _qkv_select_kernel,
        grid=(B, nb),
        in_specs=[pl.BlockSpec((1, tq, D), lambda b, i: (b, i, 0)),
                  pl.BlockSpec((1, D), lambda b, i: (0, 0)),
                  pl.BlockSpec((D, W), lambda b, i: (0, 0)),
                  pl.BlockSpec((2 * W, D), lambda b, i: (0, 0))],
        out_specs=[by_col, by_row, by_col, by_col,
                   pl.BlockSpec((1, LANES, tq), lambda b, i: (b, 0, i))],
        out_shape=[jax.ShapeDtypeStruct((B, W, S), BF16),
                   jax.ShapeDtypeStruct((B, S, W), BF16),
                   jax.ShapeDtypeStruct((B, W, S), BF16),
                   jax.ShapeDtypeStruct((B, W, S), BF16),
                   jax.ShapeDtypeStruct((B, LANES, S), BF16)],
        scratch_shapes=[pltpu.VMEM((LANES, W), F32)],
        compiler_params=pltpu.CompilerParams(
            dimension_semantics=("arbitrary", "arbitrary"), vmem_limit_bytes=VMEM_LIMIT),
        name="qkv_select",
    )(x, g, wk, wqvT)


def _moba_kernel(qT_ref, selbT_ref, k_ref, vTe_ref, vTo_ref, tbT_ref, cfar_ref, eT_ref, o_ref,
                 rhs_ref, m_ref, acc_ref):
    i = pl.program_id(1)
    tq = o_ref.shape[1]
    L = MOBA_BLOCK
    HD = ATT_HEAD_DIM
    H = ATT_HEADS
    selbT = selbT_ref[0].astype(F32)
    row = lax.broadcasted_iota(jnp.int32, (LANES, tq), 0)
    vT_refs = (vTe_ref, vTo_ref)
    slabs = [slice(p * LANES, (p + 1) * LANES) for p in range(H // 2)]

    for h in range(H):
        qTp = qT_ref[0, slabs[h // 2], :].astype(F32)
        own = (row >= HD) if h % 2 else (row < HD)
        rhs_ref[h, :LANES, :] = jnp.where(own, qTp, 0.0).astype(BF16)
        rhs_ref[h, LANES:, :] = jnp.where(row // SEL_SLOTS == h, selbT, 0.0).astype(BF16)
    m_ref[...] = jnp.full(m_ref.shape, -jnp.inf, F32)
    acc_ref[...] = jnp.zeros_like(acc_ref)

    def attend(js, tiles):
        koffs = [pl.multiple_of(j * L, L) for j in js]
        s = []
        for j, koff in zip(js, koffs):
            sj = []
            for p in range(H // 2):
                lhs = jnp.concatenate([k_ref[0, pl.ds(koff, L), slabs[p]], eT_ref[j]], axis=1)
                sj += [_dot(lhs, rhs_ref[2 * p + hh]) for hh in (0, 1)]
            s.append(sj)
        for sj, koff, tile in zip(s, koffs, tiles):
            pr, alpha = [], []
            for h in range(H):
                sh = sj[h] if tile is None else sj[h] + tbT_ref[tile, h]
                cm = jnp.max(jnp.max(sh.reshape(8, L // 8, tq), axis=0), axis=0, keepdims=True)
                m_old = m_ref[h:h + 1, :]
                if tile is None:
                    m_new = jnp.maximum(m_old, cm + cfar_ref[h])
                    pr.append(jnp.exp2(sh - (m_new - cfar_ref[h])).astype(BF16))
                else:
                    m_new = jnp.maximum(m_old, cm)
                    pr.append(jnp.exp2(sh - m_new).astype(BF16))
                m_ref[h:h + 1, :] = m_new
                alpha.append(jnp.exp2(m_old - m_new))
            for h in range(H):
                pv = _dot(vT_refs[h % 2][0, slabs[h // 2], pl.ds(koff, L)], pr[h])
                acc_ref[h] = alpha[h] * acc_ref[h] + pv

    n_far = jnp.maximum(i - 1, 0)

    def far_quad(jj, carry):
        attend([4 * jj + u for u in range(4)], [None] * 4)
        return carry

    lax.fori_loop(0, n_far // 4, far_quad, 0)

    @pl.when(n_far % 4 >= 2)
    def _():
        j0 = n_far // 4 * 4
        attend([j0, j0 + 1], [None, None])

    @pl.when(n_far % 2 == 1)
    def _():
        attend([n_far - 1], [None])

    @pl.when(i >= 1)
    def _():
        attend([i - 1, i], [1, 0])

    @pl.when(i == 0)
    def _():
        attend([i], [0])

    for p in range(H // 2):
        a0, a1 = acc_ref[2 * p], acc_ref[2 * p + 1]
        oT = jnp.concatenate([a0[:HD] / a0[HD:HD + 1], a1[HD:] / a1[0:1]], axis=0)
        o_ref[0, :, slabs[p]] = oT.T.astype(BF16)


def _moba(qT, selbT, k, vT_even, vT_odd, tbT, cfar):
    B, S, W = k.shape
    tq = MOBA_BLOCK
    nb = S // tq
    lane = np.arange(LANES)[None, None, :] % SEL_SLOTS
    eT_all = jnp.asarray(np.broadcast_to(lane == np.arange(SEL_SLOTS)[:, None, None],
                                         (SEL_SLOTS, MOBA_BLOCK, LANES)), BF16)
    whole_T = pl.BlockSpec((1, W, S), lambda b, i: (b, 0, 0))
    return pl.pallas_call(
        _moba_kernel,
        grid=(B, nb),
        in_specs=[pl.BlockSpec((1, W, tq), lambda b, i: (b, 0, i)),
                  pl.BlockSpec((1, LANES, tq), lambda b, i: (b, 0, i)),
                  pl.BlockSpec((1, S, W), lambda b, i: (b, 0, 0)),
                  whole_T, whole_T,
                  pl.BlockSpec(tbT.shape, lambda b, i: (0, 0, 0, 0)),
                  pl.BlockSpec(memory_space=pltpu.SMEM),
                  pl.BlockSpec(eT_all.shape, lambda b, i: (0, 0, 0))],
        out_specs=pl.BlockSpec((1, tq, W), lambda b, i: (b, i, 0)),
        out_shape=jax.ShapeDtypeStruct((B, S, W), BF16),
        scratch_shapes=[pltpu.VMEM((ATT_HEADS, 2 * LANES, tq), BF16),
                        pltpu.VMEM((ATT_HEADS, tq), F32),
                        pltpu.VMEM((ATT_HEADS, LANES, tq), F32)],
        compiler_params=pltpu.CompilerParams(
            dimension_semantics=("arbitrary", "arbitrary"), vmem_limit_bytes=VMEM_LIMIT),
        name="moba",
    )(qT, selbT, k, vT_even, vT_odd, tbT, cfar, eT_all)


def _merge_kernel(tiles_per_seq,
                  x_ref, ya_ref, mkT_ref, mv_ref, g1_ref, wr_ref, cw_ref, bg_ref,
                  wba_ref, wbc_ref, wbx_ref, wo_ref, g2_ref, wrt_ref, br_ref,
                  x1_ref, xs_ref, idx_ref, rank_ref, wt_ref, cnt_ref,
                  zprev_ref, hbuf, dvm, dsm, cvm, csm, dsem, ssem, zsem):
    i = pl.program_id(0)
    nt = pl.num_programs(0)
    tm, D = x_ref.shape
    CW = cw_ref.shape[1]
    XW = mv_ref.shape[2]
    CH = D // LANES
    cap = nt * tm + EXPERT_ROWS
    slot = i % 2
    prev = 1 - slot

    def rows_copy(s, kk):
        return pltpu.make_async_copy(dvm.at[kk], dsm.at[pl.ds((s * TOP_K + kk) * tm, tm)],
                                     dsem.at[s])

    def scatter_tile(s):
        for kk in range(TOP_K):
            for t in range(tm):
                d = dsm[(s * TOP_K + kk) * tm + t]
                pltpu.make_async_copy(hbuf.at[s, pl.ds(t * CH, CH)],
                                      xs_ref.at[pl.ds(pl.multiple_of(d * CH, CH), CH)],
                                      ssem.at[s]).start()

    def scatter_wait(s):
        for kk in range(TOP_K):
            pltpu.make_async_copy(hbuf.at[s], xs_ref.at[pl.ds(0, tm * CH)], ssem.at[s]).wait()

    @pl.when(i == 0)
    def _():
        n_e = cnt_ref.shape[0]
        spare = n_e * cap + lax.broadcasted_iota(jnp.int32, dvm.shape, 0) * tm \
            + lax.broadcasted_iota(jnp.int32, dvm.shape, 1)
        dvm[...] = spare
        for kk in range(TOP_K):
            rows_copy(1, kk).start()
        hbuf[...] = jnp.zeros_like(hbuf)
        for kk in range(TOP_K):
            pltpu.make_async_copy(
                hbuf.at[0], xs_ref.at[pl.ds((n_e * cap + (TOP_K + kk) * tm) * CH, tm * CH)],
                ssem.at[0]).start()

    @pl.when(i == 0)
    def _():
        cnt_ref[...] = jnp.zeros_like(cnt_ref)

    @pl.when(i % tiles_per_seq == 0)
    def _():
        zprev_ref[...] = jnp.zeros_like(zprev_ref)

    for kk in range(TOP_K):
        rows_copy(prev, kk).wait()
    scatter_tile(prev)

    x = x_ref[...]
    h = _rms(x, g1_ref[...]).astype(BF16)
    pr = _dot(h, wr_ref[...])

    cb = pr[:, :CW]
    z = pr[:, CW:2 * CW] * pr[:, 2 * CW:3 * CW]
    row = lax.broadcasted_iota(jnp.int32, (tm, CW), 0)
    zp = zprev_ref[...]
    z1 = jnp.where(row == 0, zp[7:8], pltpu.roll(z, 1, 0))
    z2 = jnp.where(row == 0, zp[6:7], jnp.where(row == 1, zp[7:8], pltpu.roll(z, 2, 0)))
    zprev_ref[...] = z[tm - 8:]
    cw = cw_ref[...]
    y_conv = cb * (cw[0:1] * z2 + cw[1:2] * z1 + cw[2:3] * z)

    o0 = 3 * CW
    scale = 1.0 / math.sqrt(XATT_HEAD_DIM)
    ys = []
    for hh in range(XATT_HEADS):
        hs = slice(hh * XATT_HEAD_DIM, (hh + 1) * XATT_HEAD_DIM)
        qx = pr[:, o0 + hh * XATT_HEAD_DIM:o0 + (hh + 1) * XATT_HEAD_DIM].astype(BF16)
        s = _dot(qx, mkT_ref[0, hs, :]) * scale
        e = jnp.exp(s - jnp.max(s, axis=1, keepdims=True))
        l = jnp.sum(e, axis=1, keepdims=True)
        ys.append(_dot(e.astype(BF16), mv_ref[0, :, hs]) / l)
    y_x = jnp.concatenate(ys, axis=1)

    o1 = o0 + XW
    bg = bg_ref[...]
    merged = (_sigmoid(pr[:, o1:o1 + D] + bg[0:1]) * _dot(ya_ref[...], wba_ref[...])
              + _sigmoid(pr[:, o1 + D:o1 + 2 * D] + bg[1:2]) * _dot(y_conv.astype(BF16), wbc_ref[...])
              + _sigmoid(pr[:, o1 + 2 * D:o1 + 3 * D] + bg[2:3]) * _dot(y_x.astype(BF16), wbx_ref[...]))
    x1 = x + _dot(merged.astype(BF16), wo_ref[...])
    x1_ref[...] = x1
    h2 = _rms(x1, g2_ref[...])

    scatter_wait(slot)

    for c in range(CH):
        hbuf[slot, pl.ds(c, tm, stride=CH), :] = h2[:, c * LANES:(c + 1) * LANES]

    lg = _dot3(wrt_ref[...], h2, _dot_nt) + br_ref[...]
    E = lg.shape[0]
    ee = lax.broadcasted_iota(jnp.int32, (E, tm), 0).astype(F32)
    work = lg
    member = jnp.zeros((E, tm), F32)
    picks, vals = [], []
    for _ in range(TOP_K):
        m = jnp.max(work, axis=0, keepdims=True)
        first = jnp.min(jnp.where(work == m, ee, float(E)), axis=0, keepdims=True)
        pick = ee == first
        work = jnp.where(pick, -jnp.inf, work)
        member = jnp.where(pick, 1.0, member)
        picks.append((pick, first))
        vals.append(m)
    exps = [jnp.exp(v - vals[0]) for v in vals]
    denom = exps[0] + exps[1] + exps[2] + exps[3]

    r_i = lax.broadcasted_iota(jnp.int32, (tm, tm), 0)
    c_i = lax.broadcasted_iota(jnp.int32, (tm, tm), 1)
    upper = jnp.where(r_i < c_i, 1.0, 0.0).astype(BF16)
    before = _dot(member.astype(BF16), upper) + cnt_ref[:, 0:1]
    cnt_ref[...] = cnt_ref[...] + jnp.sum(member, axis=1, keepdims=True)

    wrow = lax.broadcasted_iota(jnp.int32, (LANES, tm), 0)
    wpad = jnp.zeros((LANES, tm), F32)
    for kk in range(TOP_K):
        pick, first = picks[kk]
        e_k = first.astype(jnp.int32)
        rank_k = jnp.sum(jnp.where(pick, before, 0.0), axis=0, keepdims=True).astype(jnp.int32)
        idx_ref[kk:kk + 1, :] = e_k
        rank_ref[kk:kk + 1, :] = rank_k
        dvm[kk:kk + 1, :] = e_k * cap + rank_k
        wpad = jnp.where(wrow == kk, exps[kk] / denom, wpad)
    wt_ref[...] = wpad.T
    for kk in range(TOP_K):
        rows_copy(slot, kk).start()

    @pl.when(i == nt - 1)
    def _():
        for kk in range(TOP_K):
            rows_copy(slot, kk).wait()
        scatter_tile(slot)
        scatter_wait(prev)
        cnt128 = jnp.concatenate([cnt_ref[...]] * (LANES // cnt_ref.shape[0]), axis=0)
        cvm[...] = cnt128.T[:8].astype(jnp.int32)
        cnt_cp = pltpu.make_async_copy(cvm.at[0], csm, dsem.at[prev])
        cnt_cp.start()
        hbuf[prev] = jnp.zeros(hbuf.shape[1:], F32)
        cnt_cp.wait()
        zeros = [pltpu.make_async_copy(
            hbuf.at[prev],
            xs_ref.at[pl.ds(pl.multiple_of((e * cap + csm[e]) * CH, CH), EXPERT_ROWS * CH)], zsem)
            for e in range(cnt_ref.shape[0])]
        for cp in zeros:
            cp.start()
        for cp in zeros:
            cp.wait()
        scatter_wait(slot)


def _merge(x2, yatt2, mkT, mv, g1, w_rest, conv_w, b_gate, wba, wbc, wbx, wo, g2, wrt, br, S):
    T, D = x2.shape
    tm = TOKEN_TILE
    nt = T // tm
    tps = S // tm
    CW = conv_w.shape[1]
    XW, M = mkT.shape[1], mkT.shape[2]
    E = wrt.shape[0]
    CH = D // LANES
    assert tm == EXPERT_ROWS and E <= LANES and LANES % E == 0
    cap = T + EXPERT_ROWS
    const = lambda shape: pl.BlockSpec(shape, lambda i: (0,) * len(shape))
    return pl.pallas_call(
        functools.partial(_merge_kernel, tps),
        grid=(nt,),
        in_specs=[pl.BlockSpec((tm, D), lambda i: (i, 0)),
                  pl.BlockSpec((tm, yatt2.shape[1]), lambda i: (i, 0)),
                  pl.BlockSpec((1, XW, M), lambda i: (i // tps, 0, 0)),
                  pl.BlockSpec((1, M, XW), lambda i: (i // tps, 0, 0)),
                  const((1, D)), const(w_rest.shape), const(conv_w.shape), const(b_gate.shape),
                  const(wba.shape), const(wbc.shape), const(wbx.shape), const(wo.shape),
                  const((1, D)), const(wrt.shape), const(br.shape)],
        out_specs=[pl.BlockSpec((tm, D), lambda i: (i, 0)),
                   pl.BlockSpec(memory_space=pl.ANY),
                   pl.BlockSpec((TOP_K, tm), lambda i: (0, i)),
                   pl.BlockSpec((TOP_K, tm), lambda i: (0, i)),
                   pl.BlockSpec((tm, LANES), lambda i: (i, 0)),
                   pl.BlockSpec((E, LANES), lambda i: (0, 0))],
        out_shape=[jax.ShapeDtypeStruct((T, D), F32),
                   jax.ShapeDtypeStruct(((E * cap + 2 * TOP_K * tm) * CH, LANES), F32),
                   jax.ShapeDtypeStruct((TOP_K, T), jnp.int32),
                   jax.ShapeDtypeStruct((TOP_K, T), jnp.int32),
                   jax.ShapeDtypeStruct((T, LANES), F32),
                   jax.ShapeDtypeStruct((E, LANES), F32)],
        scratch_shapes=[pltpu.VMEM((8, CW), F32),
                        pltpu.VMEM((2, tm * CH, LANES), F32),
                        pltpu.VMEM((TOP_K, tm), jnp.int32),
                        pltpu.SMEM((2 * TOP_K * tm,), jnp.int32),
                        pltpu.VMEM((8, LANES), jnp.int32),
                        pltpu.SMEM((LANES,), jnp.int32),
                        pltpu.SemaphoreType.DMA((2,)),
                        pltpu.SemaphoreType.DMA((2,)),
                        pltpu.SemaphoreType.DMA(())],
        compiler_params=pltpu.CompilerParams(
            dimension_semantics=("arbitrary",), vmem_limit_bytes=VMEM_LIMIT),
        name="merge_route",
    )(x2, yatt2, mkT, mv, g1, w_rest, conv_w, b_gate, wba, wbc, wbx, wo, g2, wrt, br)


def _invert_kernel(dest_ref, cnt_ref, pe_ref, src_ref):
    bm = EXPERT_ROWS
    n_assign = dest_ref.shape[0]
    assert bm & (bm - 1) == 0
    shift = bm.bit_length() - 1

    def pad_id(r):
        return n_assign + ((r >> shift) & 1) * bm + (r & (bm - 1))

    def put_pad(r, c):
        src_ref[r] = pad_id(r)
        return c

    for e in range(N_EXPERTS):
        first_row = (pe_ref[e - 1] if e else 0) + cnt_ref[e]
        lax.fori_loop(first_row, pe_ref[e], put_pad, 0)

    def put_pad_block(blk, c):
        for r in range(bm):
            src_ref[blk * bm + r] = n_assign + (blk % 2) * bm + r
        return c

    last_blk = src_ref.shape[0] // bm - 1
    lax.fori_loop(pe_ref[N_EXPERTS - 1] // bm, last_blk, put_pad_block, 0)
    for r in range(bm):
        src_ref[last_blk * bm + r] = n_assign + 3 * bm + r

    def put(g, c):
        for u in range(32):
            a = g * 32 + u
            src_ref[dest_ref[a]] = a
        return c

    lax.fori_loop(0, n_assign // 32, put, 0)


def _invert(dest_flat, counts, pad_ends, n_src):
    return pl.pallas_call(
        _invert_kernel,
        grid_spec=pltpu.PrefetchScalarGridSpec(
            num_scalar_prefetch=3,
            grid=(1,),
            in_specs=[],
            out_specs=pl.BlockSpec(memory_space=pltpu.SMEM)),
        out_shape=jax.ShapeDtypeStruct((n_src,), jnp.int32),
        name="invert_routing",
    )(dest_flat, counts, pad_ends)


def _moe_kernel(be_ref, bie_ref, nu_ref, src_ref, xs_ref, wgu_ref, bgu_ref, wd_ref, bd_ref, y4_ref,
                ybuf0, ybuf1, wgu_bf, wd_bf, ssem0, ssem1):
    b = pl.program_id(0)
    n_used = nu_ref[0]
    bm = EXPERT_ROWS
    F = wd_ref.shape[1]
    CH = wgu_ref.shape[1] // LANES
    ybuf = (ybuf0, ybuf1)
    ssem = (ssem0, ssem1)

    def scatter_start(blk, s):
        for r in range(bm):
            a = src_ref[blk * bm + r]
            pltpu.make_async_copy(ybuf[s].at[pl.ds(r * CH, CH)],
                                  y4_ref.at[pl.ds(pl.multiple_of(a * CH, CH), CH)],
                                  ssem[s]).start()

    def scatter_wait(s):
        pltpu.make_async_copy(ybuf[s], y4_ref.at[pl.ds(0, bm * CH)], ssem[s]).wait()

    def load_rows():
        x = jnp.concatenate([xs_ref[pl.ds(c, bm, stride=CH), :] for c in range(CH)], axis=1)
        return x.astype(BF16)

    def ffn(x):
        gu = _dot(x, wgu_bf[...]) + bgu_ref[0]
        g = jnp.minimum(gu[:, :F], SWIGLU_LIMIT)
        lin = jnp.clip(gu[:, F:], -SWIGLU_LIMIT, SWIGLU_LIMIT)
        act = (lin + 1.0) * (g * _sigmoid(SWIGLU_ALPHA * g))
        return _dot(act.astype(BF16), wd_bf[...]) + bd_ref[0]

    def store_rows(s, y):
        for c in range(CH):
            ybuf[s][pl.ds(c, bm, stride=CH), :] = y[:, c * LANES:(c + 1) * LANES]

    @pl.when((b < n_used) & ((b == 0) | (be_ref[b] != be_ref[jnp.maximum(b - 1, 0)])))
    def _():
        wgu_bf[...] = wgu_ref[0].astype(BF16)
        wd_bf[...] = wd_ref[0].astype(BF16)

    for slot in (0, 1):
        other = 1 - slot

        @pl.when((b < n_used) & (b % 2 == slot))
        def _():
            @pl.when(b >= 2)
            def _():
                scatter_wait(slot)

            @pl.when(b >= 1)
            def _():
                scatter_start(b - 1, other)

            store_rows(slot, ffn(load_rows()))

        @pl.when((b == n_used - 1) & (b % 2 == slot))
        def _():
            scatter_start(b, slot)

            @pl.when(b >= 1)
            def _():
                scatter_wait(other)

            scatter_wait(slot)


def _moe_experts(T, block_expert, block_in_expert, n_used, src, xs, wgu, bgu, wd, bd):
    bm = EXPERT_ROWS
    E, D, F2 = wgu.shape
    F = wd.shape[1]
    CH = D // LANES
    n_blocks = block_expert.shape[0]
    blocks_per_expert = T // bm + 1
    assert src.shape[0] == (n_blocks + 1) * bm
    per_e = lambda b, be, bie, nu, s: (be[b], 0, 0)
    return pl.pallas_call(
        _moe_kernel,
        grid_spec=pltpu.PrefetchScalarGridSpec(
            num_scalar_prefetch=4,
            grid=(n_blocks,),
            in_specs=[pl.BlockSpec((bm * CH, LANES),
                                   lambda b, be, bie, nu, s: (be[b] * blocks_per_expert + bie[b], 0)),
                      pl.BlockSpec((1, D, F2), per_e),
                      pl.BlockSpec((1, 1, F2), per_e),
                      pl.BlockSpec((1, F, D), per_e),
                      pl.BlockSpec((1, 1, D), per_e)],
            out_specs=pl.BlockSpec(memory_space=pl.ANY),
            scratch_shapes=[pltpu.VMEM((bm * CH, LANES), F32)] * 2 + [
                            pltpu.VMEM((D, F2), BF16),
                            pltpu.VMEM((F, D), BF16),
                            pltpu.SemaphoreType.DMA(()),
                            pltpu.SemaphoreType.DMA(())]),
        out_shape=jax.ShapeDtypeStruct(((TOP_K * T + 4 * bm) * CH, LANES), F32),
        compiler_params=pltpu.CompilerParams(
            dimension_semantics=("arbitrary",), vmem_limit_bytes=VMEM_LIMIT),
        name="moe_experts",
    )(block_expert, block_in_expert, n_used, src, xs, wgu, bgu, wd, bd)


def _finish_kernel(x1_ref, wt_ref, g_ref, y0_ref, y1_ref, y2_ref, y3_ref, o_ref):
    tm, D = x1_ref.shape
    CH = D // LANES
    wt = wt_ref[...]
    y = x1_ref[...]
    for kk, yk_ref in enumerate((y0_ref, y1_ref, y2_ref, y3_ref)):
        yk = jnp.concatenate([yk_ref[pl.ds(c, tm, stride=CH), :] for c in range(CH)], axis=1)
        y = y + wt[:, kk:kk + 1] * yk
    o_ref[...] = _rms(y, g_ref[...])


def _finish(x1, wt, g, y4):
    T, D = x1.shape
    tm = TOKEN_TILE
    CH = D // LANES
    nt = T // tm
    choice = lambda kk: pl.BlockSpec((tm * CH, LANES), lambda i: (kk * nt + i, 0))
    return pl.pallas_call(
        _finish_kernel,
        grid=(nt,),
        in_specs=[pl.BlockSpec((tm, D), lambda i: (i, 0)),
                  pl.BlockSpec((tm, LANES), lambda i: (i, 0)),
                  pl.BlockSpec((1, D), lambda i: (0, 0)),
                  choice(0), choice(1), choice(2), choice(3)],
        out_specs=pl.BlockSpec((tm, D), lambda i: (i, 0)),
        out_shape=jax.ShapeDtypeStruct((T, D), F32),
        compiler_params=pltpu.CompilerParams(
            dimension_semantics=("arbitrary",), vmem_limit_bytes=VMEM_LIMIT),
        name="finish",
    )(x1, wt, g, y4, y4, y4, y4)


def _t5_bucket(dist):
    n = jnp.maximum(dist, 0)
    max_exact = REL_BUCKETS // 2
    nf = jnp.maximum(n, 1).astype(F32)
    large = max_exact + (jnp.log(nf / max_exact) / math.log(REL_MAX_DIST / max_exact)
                         * (REL_BUCKETS - max_exact)).astype(jnp.int32)
    large = jnp.minimum(large, REL_BUCKETS - 1)
    return jnp.where(n < max_exact, n, large)


def _bias_tiles(rel_bias):
    L = MOBA_BLOCK
    assert REL_MAX_DIST <= L
    d = np.arange(-L, 2 * L + 1)
    onehot = (_t5_bucket(jnp.asarray(d))[:, None] == jnp.arange(REL_BUCKETS)).astype(F32)
    by_dist = jnp.dot(onehot, rel_bias.astype(F32), precision=HIGHEST)
    by_dist = jnp.where((d >= 0)[:, None], by_dist, NEG).T * LOG2E
    H = by_dist.shape[0]
    tiles = []
    for t in range(2):
        v = jnp.concatenate([by_dist[:, (t + 1) * L:(t + 2) * L], by_dist[:, t * L:(t + 1) * L]],
                            axis=1)
        flat = jnp.tile(v, (1, L))[:, :L * (2 * L - 1)]
        tiles.append(flat.reshape(H, L, 2 * L - 1)[:, :, :L])
    return jnp.stack(tiles), by_dist[:, 3 * L]


def kernel(x, mem, rel_bias, norm_mix_g, w_in, b_gate, conv_w, norm_mem_g, w_mem_kv, w_br_att,
           w_br_conv, w_br_xatt, w_out, norm_ffn_g, w_router, b_router, w_gu, b_gu, w_down,
           b_down, norm_final_g):
    B, S, D = x.shape
    T = B * S
    depth = w_in.shape[0]
    assert depth == 1, "the finish step applies the final norm: single-layer configuration only"
    W = ATT_WIDTH
    XW = XATT_HEADS * XATT_HEAD_DIM
    E = w_router.shape[2]
    bm = EXPERT_ROWS
    tb, cfar = _bias_tiles(rel_bias)

    xc = x
    for l in range(depth):
        w_l = w_in[l]
        wk = w_l[:, W:2 * W].astype(BF16)
        wqvT = jnp.concatenate([w_l[:, :W], w_l[:, 2 * W:3 * W]], axis=1).T.astype(BF16)
        w_rest = w_l[:, 3 * W:].astype(BF16)
        wm = w_mem_kv[l]

        mkT, mv = _mem_kv(mem, norm_mem_g[l][None], wm[:, :XW].T.astype(BF16),
                          wm[:, XW:].astype(BF16))
        qT, k, vT_even, vT_odd, selbT = _qkv_select(xc, norm_mix_g[l][None], wk, wqvT)
        y_att = _moba(qT, selbT, k, vT_even, vT_odd, tb, cfar)

        x1, xs, idx, rank, wt, cnt = _merge(
            xc.reshape(T, D), y_att.reshape(T, W), mkT, mv, norm_mix_g[l][None], w_rest,
            conv_w[l], b_gate[l], w_br_att[l].astype(BF16), w_br_conv[l].astype(BF16),
            w_br_xatt[l].astype(BF16), w_out[l].astype(BF16), norm_ffn_g[l][None],
            w_router[l].T, b_router[l][:, None], S)

        counts = cnt[:, 0].astype(jnp.int32)
        padded = (counts + bm - 1) // bm * bm
        pad_ends = jnp.cumsum(padded).astype(jnp.int32)
        pad_starts = pad_ends - padded
        onehot = idx[..., None] == jnp.arange(E, dtype=jnp.int32)
        dest = (jnp.sum(jnp.where(onehot, pad_starts, 0), axis=-1) + rank).reshape(-1)
        n_rows = T * TOP_K + E * bm
        n_blocks = n_rows // bm
        n_used = pad_ends[-1] // bm
        blk = jnp.minimum(jnp.arange(n_blocks, dtype=jnp.int32), n_used - 1) * bm
        block_expert = jnp.minimum(jnp.sum(blk[:, None] >= pad_ends[None, :], axis=1),
                                   E - 1).astype(jnp.int32)

        src = _invert(dest, counts, pad_ends, n_rows + bm)

        is_e = block_expert[:, None] == jnp.arange(E, dtype=jnp.int32)
        block_in_expert = (blk - jnp.sum(jnp.where(is_e, pad_starts, 0), axis=1)) // bm

        y4 = _moe_experts(T, block_expert, block_in_expert, n_used[None], src, xs, w_gu[l],
                          b_gu[l][:, None], w_down[l], b_down[l][:, None])
        xc = _finish(x1, wt, norm_final_g[None], y4).reshape(B, S, D)
    return xc
```

```python
import functools
import math

import jax
import jax.numpy as jnp
import numpy as np
from jax import lax
from jax.experimental import pallas as pl
from jax.experimental.pallas import tpu as pltpu

F32 = jnp.float32
BF16 = jnp.bfloat16
HIGHEST = lax.Precision.HIGHEST

ATT_HEADS = 8
ATT_HEAD_DIM = 64
ATT_WIDTH = ATT_HEADS * ATT_HEAD_DIM
MOBA_BLOCK = 256
MOBA_TOPK = 3
REL_BUCKETS = 32
REL_MAX_DIST = 128
XATT_HEADS = 4
XATT_HEAD_DIM = 128
N_EXPERTS = 32
TOP_K = 4
SWIGLU_LIMIT = 7.0
SWIGLU_ALPHA = 1.702
EPS = 1e-5
NEG = -1e30
LOG2E = math.log2(math.e)

LANES = 128
SEL_SLOTS = 16
TOKEN_TILE = 512
EXPERT_ROWS = 256
VMEM_LIMIT = 56 * 1024 * 1024


def _rms(x, g):
    return x * lax.rsqrt(jnp.mean(x * x, axis=-1, keepdims=True) + EPS) * g


def _dot(a, b):
    return jnp.dot(a, b, preferred_element_type=F32)


def _dot_nt(a, b):
    return lax.dot_general(a, b, (((1,), (1,)), ((), ())), preferred_element_type=F32)


def _sigmoid(x):
    return 1.0 / (1.0 + jnp.exp(-x))


def _split_bf16(x):
    hi = x.astype(BF16)
    return hi, (x - hi.astype(F32)).astype(BF16)


def _dot3(a, b, dot):
    m = a.shape[0]
    a_hi, a_lo = _split_bf16(a)
    b_hi, b_lo = _split_bf16(b)
    both = dot(jnp.concatenate([a_hi, a_lo], axis=0), b_hi)
    return both[:m] + both[m:] + dot(a_hi, b_lo)


def _mem_kv_kernel(mem_ref, g_ref, wkT_ref, wv_ref, mkT_ref, mv_ref):
    mn = _rms(mem_ref[0], g_ref[...]).astype(BF16)
    mkT_ref[0] = _dot_nt(wkT_ref[...], mn).astype(BF16)
    mv_ref[0] = _dot(mn, wv_ref[...]).astype(BF16)


def _mem_kv(mem, g, wkT, wv):
    B, M, D = mem.shape
    XW = wv.shape[1]
    return pl.pallas_call(
        _mem_kv_kernel,
        grid=(B,),
        in_specs=[pl.BlockSpec((1, M, D), lambda b: (b, 0, 0)),
                  pl.BlockSpec((1, D), lambda b: (0, 0)),
                  pl.BlockSpec((XW, D), lambda b: (0, 0)),
                  pl.BlockSpec((D, XW), lambda b: (0, 0))],
        out_specs=[pl.BlockSpec((1, XW, M), lambda b: (b, 0, 0)),
                   pl.BlockSpec((1, M, XW), lambda b: (b, 0, 0))],
        out_shape=[jax.ShapeDtypeStruct((B, XW, M), BF16),
                   jax.ShapeDtypeStruct((B, M, XW), BF16)],
        name="mem_kv",
    )(mem, g, wkT, wv)


def _qkv_select_kernel(x_ref, g_ref, wk_ref, wqvT_ref, qT_ref, k_ref, vTe_ref, vTo_ref, selbT_ref,
                       km_ref):
    i = pl.program_id(1)
    tq = x_ref.shape[1]
    W = ATT_WIDTH

    @pl.when(i == 0)
    def _():
        km_ref[...] = jnp.zeros_like(km_ref)

    h = _rms(x_ref[0], g_ref[...]).astype(BF16)
    k = _dot(h, wk_ref[...])
    k_ref[0] = k.astype(BF16)
    qvT = _dot_nt(wqvT_ref[...], h)
    qT = qvT[:W] * (1.0 / math.sqrt(ATT_HEAD_DIM))
    qT_ref[0] = (qT * LOG2E).astype(BF16)
    vT = qvT[W:]
    even_head = (lax.broadcasted_iota(jnp.int32, (W, tq), 0) // ATT_HEAD_DIM) % 2 == 0
    vTe_ref[0] = jnp.where(even_head, vT, 1.0).astype(BF16)
    vTo_ref[0] = jnp.where(even_head, 1.0, vT).astype(BF16)

    gate = _dot3(km_ref[...], qT, _dot)
    g3 = gate.reshape(ATT_HEADS, SEL_SLOTS, tq)
    jj = lax.broadcasted_iota(jnp.int32, g3.shape, 1).astype(F32)
    valid = jj < i.astype(F32)
    g3 = jnp.where(valid, g3, -jnp.inf)
    sel = jnp.zeros(g3.shape, F32)
    for _ in range(MOBA_TOPK):
        m = jnp.max(g3, axis=1, keepdims=True)
        first = jnp.min(jnp.where(g3 == m, jj, float(SEL_SLOTS)), axis=1, keepdims=True)
        pick = jj == first
        sel = jnp.where(pick, 1.0, sel)
        g3 = jnp.where(pick, -jnp.inf, g3)
    keep = ((sel > 0.5) & valid) | (jj == i.astype(F32))
    selbT_ref[0] = jnp.where(keep, 0.0, NEG).reshape(ATT_HEADS * SEL_SLOTS, tq).astype(BF16)

    kmean = jnp.sum(k, axis=0, keepdims=True) * (1.0 / tq)
    lane_head = lax.broadcasted_iota(jnp.int32, (1, W), 1) // ATT_HEAD_DIM
    for hh in range(ATT_HEADS):
        km_ref[pl.ds(hh * SEL_SLOTS + i, 1), :] = jnp.where(lane_head == hh, kmean, 0.0)


def _qkv_select(x, g, wk, wqvT):
    B, S, D = x.shape
    W = ATT_WIDTH
    tq = MOBA_BLOCK
    nb = S // tq
    assert nb <= SEL_SLOTS and ATT_HEADS * SEL_SLOTS == LANES
    by_row = pl.BlockSpec((1, tq, W), lambda b, i: (b, i, 0))
    by_col = pl.BlockSpec((1, W, tq), lambda b, i: (b, 0, i))
    return pl.pallas_call(
        _qkv_select_kernel,
        grid=(B, nb),
        in_specs=[pl.BlockSpec((1, tq, D), lambda b, i: (b, i, 0)),
                  pl.BlockSpec((1, D), lambda b, i: (0, 0)),
                  pl.BlockSpec((D, W), lambda b, i: (0, 0)),
                  pl.BlockSpec((2 * W, D), lambda b, i: (0, 0))],
        out_specs=[by_col, by_row, by_col, by_col,
                   pl.BlockSpec((1, LANES, tq), lambda b, i: (b, 0, i))],
        out_shape=[jax.ShapeDtypeStruct((B, W, S), BF16),
                   jax.ShapeDtypeStruct((B, S, W), BF16),
                   jax.ShapeDtypeStruct((B, W, S), BF16),
                   jax.ShapeDtypeStruct((B, W, S), BF16),
                   jax.ShapeDtypeStruct((B, LANES, S), BF16)],
        scratch_shapes=[pltpu.VMEM((LANES, W), F32)],
        compiler_params=pltpu.CompilerParams(
            dimension_semantics=("arbitrary", "arbitrary"), vmem_limit_bytes=VMEM_LIMIT),
        name="qkv_select",
    )(x, g, wk, wqvT)


def _moba_kernel(qT_ref, selbT_ref, k_ref, vTe_ref, vTo_ref, tbT_ref, cfar_ref, eT_ref, o_ref,
                 rhs_ref, m_ref, acc_ref):
    i = pl.program_id(1)
    tq = o_ref.shape[1]
    L = MOBA_BLOCK
    HD = ATT_HEAD_DIM
    H = ATT_HEADS
    selbT = selbT_ref[0].astype(F32)
    row = lax.broadcasted_iota(jnp.int32, (LANES, tq), 0)
    vT_refs = (vTe_ref, vTo_ref)
    slabs = [slice(p * LANES, (p + 1) * LANES) for p in range(H // 2)]

    for h in range(H):
        qTp = qT_ref[0, slabs[h // 2], :].astype(F32)
        own = (row >= HD) if h % 2 else (row < HD)
        rhs_ref[h, :LANES, :] = jnp.where(own, qTp, 0.0).astype(BF16)
        rhs_ref[h, LANES:, :] = jnp.where(row // SEL_SLOTS == h, selbT, 0.0).astype(BF16)
    m_ref[...] = jnp.full(m_ref.shape, -jnp.inf, F32)
    acc_ref[...] = jnp.zeros_like(acc_ref)

    def attend(js, tiles):
        koffs = [pl.multiple_of(j * L, L) for j in js]
        s = []
        for j, koff in zip(js, koffs):
            sj = []
            for p in range(H // 2):
                lhs = jnp.concatenate([k_ref[0, pl.ds(koff, L), slabs[p]], eT_ref[j]], axis=1)
                sj += [_dot(lhs, rhs_ref[2 * p + hh]) for hh in (0, 1)]
            s.append(sj)
        for sj, koff, tile in zip(s, koffs, tiles):
            pr, alpha = [], []
            for h in range(H):
                sh = sj[h] if tile is None else sj[h] + tbT_ref[tile, h]
                cm = jnp.max(jnp.max(sh.reshape(8, L // 8, tq), axis=0), axis=0, keepdims=True)
                m_old = m_ref[h:h + 1, :]
                if tile is None:
                    m_new = jnp.maximum(m_old, cm + cfar_ref[h])
                    pr.append(jnp.exp2(sh - (m_new - cfar_ref[h])).astype(BF16))
                else:
                    m_new = jnp.maximum(m_old, cm)
                    pr.append(jnp.exp2(sh - m_new).astype(BF16))
                m_ref[h:h + 1, :] = m_new
                alpha.append(jnp.exp2(m_old - m_new))
            for h in range(H):
                pv = _dot(vT_refs[h % 2][0, slabs[h // 2], pl.ds(koff, L)], pr[h])
                acc_ref[h] = alpha[h] * acc_ref[h] + pv

    n_far = jnp.maximum(i - 1, 0)

    def far_quad(jj, carry):
        attend([4 * jj + u for u in range(4)], [None] * 4)
        return carry

    lax.fori_loop(0, n_far // 4, far_quad, 0)

    @pl.when(n_far % 4 >= 2)
    def _():
        j0 = n_far // 4 * 4
        attend([j0, j0 + 1], [None, None])

    @pl.when(n_far % 2 == 1)
    def _():
        attend([n_far - 1], [None])

    @pl.when(i >= 1)
    def _():
        attend([i - 1, i], [1, 0])

    @pl.when(i == 0)
    def _():
        attend([i], [0])

    for p in range(H // 2):
        a0, a1 = acc_ref[2 * p], acc_ref[2 * p + 1]
        oT = jnp.concatenate([a0[:HD] / a0[HD:HD + 1], a1[HD:] / a1[0:1]], axis=0)
        o_ref[0, :, slabs[p]] = oT.T.astype(BF16)


def _moba(qT, selbT, k, vT_even, vT_odd, tbT, cfar):
    B, S, W = k.shape
    tq = MOBA_BLOCK
    nb = S // tq
    lane = np.arange(LANES)[None, None, :] % SEL_SLOTS
    eT_all = jnp.asarray(np.broadcast_to(lane == np.arange(SEL_SLOTS)[:, None, None],
                                         (SEL_SLOTS, MOBA_BLOCK, LANES)), BF16)
    whole_T = pl.BlockSpec((1, W, S), lambda b, i: (b, 0, 0))
    return pl.pallas_call(
        _moba_kernel,
        grid=(B, nb),
        in_specs=[pl.BlockSpec((1, W, tq), lambda b, i: (b, 0, i)),
                  pl.BlockSpec((1, LANES, tq), lambda b, i: (b, 0, i)),
                  pl.BlockSpec((1, S, W), lambda b, i: (b, 0, 0)),
                  whole_T, whole_T,
                  pl.BlockSpec(tbT.shape, lambda b, i: (0, 0, 0, 0)),
                  pl.BlockSpec(memory_space=pltpu.SMEM),
                  pl.BlockSpec(eT_all.shape, lambda b, i: (0, 0, 0))],
        out_specs=pl.BlockSpec((1, tq, W), lambda b, i: (b, i, 0)),
        out_shape=jax.ShapeDtypeStruct((B, S, W), BF16),
        scratch_shapes=[pltpu.VMEM((ATT_HEADS, 2 * LANES, tq), BF16),
                        pltpu.VMEM((ATT_HEADS, tq), F32),
                        pltpu.VMEM((ATT_HEADS, LANES, tq), F32)],
        compiler_params=pltpu.CompilerParams(
            dimension_semantics=("arbitrary", "arbitrary"), vmem_limit_bytes=VMEM_LIMIT),
        name="moba",
    )(qT, selbT, k, vT_even, vT_odd, tbT, cfar, eT_all)


def _merge_kernel(tiles_per_seq,
                  x_ref, ya_ref, mkT_ref, mv_ref, g1_ref, wr_ref, cw_ref, bg_ref,
                  wba_ref, wbc_ref, wbx_ref, wo_ref, g2_ref, wrt_ref, br_ref,
                  x1_ref, h2_ref, idx_ref, rank_ref, wt_ref, cnt_ref, zprev_ref):
    i = pl.program_id(0)
    tm, D = x_ref.shape
    CW = cw_ref.shape[1]
    XW = mv_ref.shape[2]

    @pl.when(i == 0)
    def _():
        cnt_ref[...] = jnp.zeros_like(cnt_ref)

    @pl.when(i % tiles_per_seq == 0)
    def _():
        zprev_ref[...] = jnp.zeros_like(zprev_ref)

    x = x_ref[...]
    h = _rms(x, g1_ref[...]).astype(BF16)
    pr = _dot(h, wr_ref[...])

    cb = pr[:, :CW]
    z = pr[:, CW:2 * CW] * pr[:, 2 * CW:3 * CW]
    row = lax.broadcasted_iota(jnp.int32, (tm, CW), 0)
    zp = zprev_ref[...]
    z1 = jnp.where(row == 0, zp[7:8], pltpu.roll(z, 1, 0))
    z2 = jnp.where(row == 0, zp[6:7], jnp.where(row == 1, zp[7:8], pltpu.roll(z, 2, 0)))
    zprev_ref[...] = z[tm - 8:]
    cw = cw_ref[...]
    y_conv = cb * (cw[0:1] * z2 + cw[1:2] * z1 + cw[2:3] * z)

    o0 = 3 * CW
    scale = 1.0 / math.sqrt(XATT_HEAD_DIM)
    ys = []
    for hh in range(XATT_HEADS):
        hs = slice(hh * XATT_HEAD_DIM, (hh + 1) * XATT_HEAD_DIM)
        qx = pr[:, o0 + hh * XATT_HEAD_DIM:o0 + (hh + 1) * XATT_HEAD_DIM].astype(BF16)
        s = _dot(qx, mkT_ref[0, hs, :]) * scale
        e = jnp.exp(s - jnp.max(s, axis=1, keepdims=True))
        l = jnp.sum(e, axis=1, keepdims=True)
        ys.append(_dot(e.astype(BF16), mv_ref[0, :, hs]) / l)
    y_x = jnp.concatenate(ys, axis=1)

    o1 = o0 + XW
    bg = bg_ref[...]
    merged = (_sigmoid(pr[:, o1:o1 + D] + bg[0:1]) * _dot(ya_ref[...], wba_ref[...])
              + _sigmoid(pr[:, o1 + D:o1 + 2 * D] + bg[1:2]) * _dot(y_conv.astype(BF16), wbc_ref[...])
              + _sigmoid(pr[:, o1 + 2 * D:o1 + 3 * D] + bg[2:3]) * _dot(y_x.astype(BF16), wbx_ref[...]))
    x1 = x + _dot(merged.astype(BF16), wo_ref[...])
    x1_ref[...] = x1
    h2 = _rms(x1, g2_ref[...])
    for c in range(D // LANES):
        h2_ref[pl.ds(c, tm, stride=D // LANES), :] = h2[:, c * LANES:(c + 1) * LANES]

    lg = _dot3(wrt_ref[...], h2, _dot_nt) + br_ref[...]
    E = lg.shape[0]
    ee = lax.broadcasted_iota(jnp.int32, (E, tm), 0).astype(F32)
    work = lg
    member = jnp.zeros((E, tm), F32)
    picks, vals = [], []
    for _ in range(TOP_K):
        m = jnp.max(work, axis=0, keepdims=True)
        first = jnp.min(jnp.where(work == m, ee, float(E)), axis=0, keepdims=True)
        pick = ee == first
        work = jnp.where(pick, -jnp.inf, work)
        member = jnp.where(pick, 1.0, member)
        picks.append((pick, first))
        vals.append(m)
    exps = [jnp.exp(v - vals[0]) for v in vals]
    denom = exps[0] + exps[1] + exps[2] + exps[3]

    r_i = lax.broadcasted_iota(jnp.int32, (tm, tm), 0)
    c_i = lax.broadcasted_iota(jnp.int32, (tm, tm), 1)
    upper = jnp.where(r_i < c_i, 1.0, 0.0).astype(BF16)
    before = _dot(member.astype(BF16), upper) + cnt_ref[:, 0:1]
    cnt_ref[...] = cnt_ref[...] + jnp.sum(member, axis=1, keepdims=True)

    wrow = lax.broadcasted_iota(jnp.int32, (LANES, tm), 0)
    wpad = jnp.zeros((LANES, tm), F32)
    for kk in range(TOP_K):
        pick, first = picks[kk]
        idx_ref[kk:kk + 1, :] = first.astype(jnp.int32)
        rank_ref[kk:kk + 1, :] = jnp.sum(jnp.where(pick, before, 0.0), axis=0,
                                         keepdims=True).astype(jnp.int32)
        wpad = jnp.where(wrow == kk, exps[kk] / denom, wpad)
    wt_ref[...] = wpad.T


def _merge(x2, yatt2, mkT, mv, g1, w_rest, conv_w, b_gate, wba, wbc, wbx, wo, g2, wrt, br, S):
    T, D = x2.shape
    tm = TOKEN_TILE
    nt = T // tm
    tps = S // tm
    CW = conv_w.shape[1]
    XW, M = mkT.shape[1], mkT.shape[2]
    E = wrt.shape[0]
    const = lambda shape: pl.BlockSpec(shape, lambda i: (0,) * len(shape))
    return pl.pallas_call(
        functools.partial(_merge_kernel, tps),
        grid=(nt,),
        in_specs=[pl.BlockSpec((tm, D), lambda i: (i, 0)),
                  pl.BlockSpec((tm, yatt2.shape[1]), lambda i: (i, 0)),
                  pl.BlockSpec((1, XW, M), lambda i: (i // tps, 0, 0)),
                  pl.BlockSpec((1, M, XW), lambda i: (i // tps, 0, 0)),
                  const((1, D)), const(w_rest.shape), const(conv_w.shape), const(b_gate.shape),
                  const(wba.shape), const(wbc.shape), const(wbx.shape), const(wo.shape),
                  const((1, D)), const(wrt.shape), const(br.shape)],
        out_specs=[pl.BlockSpec((tm, D), lambda i: (i, 0)),
                   pl.BlockSpec((tm * (D // LANES), LANES), lambda i: (i, 0)),
                   pl.BlockSpec((TOP_K, tm), lambda i: (0, i)),
                   pl.BlockSpec((TOP_K, tm), lambda i: (0, i)),
                   pl.BlockSpec((tm, LANES), lambda i: (i, 0)),
                   pl.BlockSpec((E, LANES), lambda i: (0, 0))],
        out_shape=[jax.ShapeDtypeStruct((T, D), F32),
                   jax.ShapeDtypeStruct((T * (D // LANES), LANES), F32),
                   jax.ShapeDtypeStruct((TOP_K, T), jnp.int32),
                   jax.ShapeDtypeStruct((TOP_K, T), jnp.int32),
                   jax.ShapeDtypeStruct((T, LANES), F32),
                   jax.ShapeDtypeStruct((E, LANES), F32)],
        scratch_shapes=[pltpu.VMEM((8, CW), F32)],
        compiler_params=pltpu.CompilerParams(
            dimension_semantics=("arbitrary",), vmem_limit_bytes=VMEM_LIMIT),
        name="merge_route",
    )(x2, yatt2, mkT, mv, g1, w_rest, conv_w, b_gate, wba, wbc, wbx, wo, g2, wrt, br)


def _invert_kernel(dest_ref, cnt_ref, pe_ref, src_ref):
    bm = EXPERT_ROWS
    n_assign = dest_ref.shape[0]
    assert bm & (bm - 1) == 0
    shift = bm.bit_length() - 1

    def pad_id(r):
        return n_assign + ((r >> shift) & 1) * bm + (r & (bm - 1))

    def put_pad(r, c):
        src_ref[r] = pad_id(r)
        return c

    for e in range(N_EXPERTS):
        first_row = (pe_ref[e - 1] if e else 0) + cnt_ref[e]
        lax.fori_loop(first_row, pe_ref[e], put_pad, 0)

    def put_pad_block(blk, c):
        for r in range(bm):
            src_ref[blk * bm + r] = n_assign + (blk % 2) * bm + r
        return c

    last_blk = src_ref.shape[0] // bm - 1
    lax.fori_loop(pe_ref[N_EXPERTS - 1] // bm, last_blk, put_pad_block, 0)
    for r in range(bm):
        src_ref[last_blk * bm + r] = n_assign + 3 * bm + r

    def put(g, c):
        for u in range(32):
            a = g * 32 + u
            src_ref[dest_ref[a]] = a
        return c

    lax.fori_loop(0, n_assign // 32, put, 0)


def _invert(dest_flat, counts, pad_ends, n_src):
    return pl.pallas_call(
        _invert_kernel,
        grid_spec=pltpu.PrefetchScalarGridSpec(
            num_scalar_prefetch=3,
            grid=(1,),
            in_specs=[],
            out_specs=pl.BlockSpec(memory_space=pltpu.SMEM)),
        out_shape=jax.ShapeDtypeStruct((n_src,), jnp.int32),
        name="invert_routing",
    )(dest_flat, counts, pad_ends)


def _moe_kernel(T, be_ref, nu_ref, src_ref, h2t_ref, wgu_ref, bgu_ref, wd_ref, bd_ref, y4_ref,
                xbuf0, xbuf1, ybuf0, ybuf1, wgu_bf, wd_bf, gsem0, gsem1, ssem0, ssem1):
    b = pl.program_id(0)
    n_used = nu_ref[0]
    bm = EXPERT_ROWS
    F = wd_ref.shape[1]
    CH = wgu_ref.shape[1] // LANES
    xbuf = (xbuf0, xbuf1)
    ybuf = (ybuf0, ybuf1)
    gsem = (gsem0, gsem1)
    ssem = (ssem0, ssem1)

    def gather_start(blk, s):
        for r in range(bm):
            tok = src_ref[blk * bm + r] & (T - 1)
            pltpu.make_async_copy(h2t_ref.at[pl.ds(pl.multiple_of(tok * CH, CH), CH)],
                                  xbuf[s].at[pl.ds(r * CH, CH)], gsem[s]).start()

    def scatter_start(blk, s):
        for r in range(bm):
            a = src_ref[blk * bm + r]
            pltpu.make_async_copy(ybuf[s].at[pl.ds(r * CH, CH)],
                                  y4_ref.at[pl.ds(pl.multiple_of(a * CH, CH), CH)],
                                  ssem[s]).start()

    def gather_wait(s):
        pltpu.make_async_copy(h2t_ref.at[pl.ds(0, bm * CH)], xbuf[s], gsem[s]).wait()

    def scatter_wait(s):
        pltpu.make_async_copy(ybuf[s], y4_ref.at[pl.ds(0, bm * CH)], ssem[s]).wait()

    def load_rows(s):
        x = jnp.concatenate([xbuf[s][pl.ds(c, bm, stride=CH), :] for c in range(CH)], axis=1)
        return x.astype(BF16)

    def ffn(x):
        gu = _dot(x, wgu_bf[...]) + bgu_ref[0]
        g = jnp.minimum(gu[:, :F], SWIGLU_LIMIT)
        lin = jnp.clip(gu[:, F:], -SWIGLU_LIMIT, SWIGLU_LIMIT)
        act = (lin + 1.0) * (g * _sigmoid(SWIGLU_ALPHA * g))
        return _dot(act.astype(BF16), wd_bf[...]) + bd_ref[0]

    def store_rows(s, y):
        for c in range(CH):
            ybuf[s][pl.ds(c, bm, stride=CH), :] = y[:, c * LANES:(c + 1) * LANES]

    @pl.when((b < n_used) & ((b == 0) | (be_ref[b] != be_ref[jnp.maximum(b - 1, 0)])))
    def _():
        wgu_bf[...] = wgu_ref[0].astype(BF16)
        wd_bf[...] = wd_ref[0].astype(BF16)

    @pl.when(b == 0)
    def _():
        gather_start(0, 0)
        gather_wait(0)
        gather_start(1, 1)
        store_rows(0, ffn(load_rows(0)))

    for slot in (0, 1):
        other = 1 - slot

        @pl.when((b > 0) & (b < n_used) & (b % 2 == slot))
        def _():
            gather_start(b + 1, other)
            gather_wait(slot)

            @pl.when(b >= 2)
            def _():
                scatter_wait(slot)

            scatter_start(b - 1, other)
            store_rows(slot, ffn(load_rows(slot)))

        @pl.when((b == n_used - 1) & (b % 2 == slot))
        def _():
            scatter_start(b, slot)
            gather_wait(other)

            @pl.when(b >= 1)
            def _():
                scatter_wait(other)

            scatter_wait(slot)


def _moe_experts(T, block_expert, n_used, src, h2t, wgu, bgu, wd, bd):
    bm = EXPERT_ROWS
    E, D, F2 = wgu.shape
    F = wd.shape[1]
    CH = D // LANES
    n_blocks = block_expert.shape[0]
    assert T & (T - 1) == 0 and src.shape[0] == (n_blocks + 1) * bm
    per_e = lambda b, be, nu, s: (be[b], 0, 0)
    return pl.pallas_call(
        functools.partial(_moe_kernel, T),
        grid_spec=pltpu.PrefetchScalarGridSpec(
            num_scalar_prefetch=3,
            grid=(n_blocks,),
            in_specs=[pl.BlockSpec(memory_space=pl.ANY),
                      pl.BlockSpec((1, D, F2), per_e),
                      pl.BlockSpec((1, 1, F2), per_e),
                      pl.BlockSpec((1, F, D), per_e),
                      pl.BlockSpec((1, 1, D), per_e)],
            out_specs=pl.BlockSpec(memory_space=pl.ANY),
            scratch_shapes=[pltpu.VMEM((bm * CH, LANES), F32)] * 4 + [
                            pltpu.VMEM((D, F2), BF16),
                            pltpu.VMEM((F, D), BF16),
                            pltpu.SemaphoreType.DMA(())] + [pltpu.SemaphoreType.DMA(())] * 3),
        out_shape=jax.ShapeDtypeStruct(((TOP_K * T + 4 * bm) * CH, LANES), F32),
        compiler_params=pltpu.CompilerParams(
            dimension_semantics=("arbitrary",), vmem_limit_bytes=VMEM_LIMIT),
        name="moe_experts",
    )(block_expert, n_used, src, h2t, wgu, bgu, wd, bd)


def _finish_kernel(x1_ref, wt_ref, g_ref, y0_ref, y1_ref, y2_ref, y3_ref, o_ref):
    tm, D = x1_ref.shape
    CH = D // LANES
    wt = wt_ref[...]
    y = x1_ref[...]
    for kk, yk_ref in enumerate((y0_ref, y1_ref, y2_ref, y3_ref)):
        yk = jnp.concatenate([yk_ref[pl.ds(c, tm, stride=CH), :] for c in range(CH)], axis=1)
        y = y + wt[:, kk:kk + 1] * yk
    o_ref[...] = _rms(y, g_ref[...])


def _finish(x1, wt, g, y4):
    T, D = x1.shape
    tm = TOKEN_TILE
    CH = D // LANES
    nt = T // tm
    choice = lambda kk: pl.BlockSpec((tm * CH, LANES), lambda i: (kk * nt + i, 0))
    return pl.pallas_call(
        _finish_kernel,
        grid=(nt,),
        in_specs=[pl.BlockSpec((tm, D), lambda i: (i, 0)),
                  pl.BlockSpec((tm, LANES), lambda i: (i, 0)),
                  pl.BlockSpec((1, D), lambda i: (0, 0)),
                  choice(0), choice(1), choice(2), choice(3)],
        out_specs=pl.BlockSpec((tm, D), lambda i: (i, 0)),
        out_shape=jax.ShapeDtypeStruct((T, D), F32),
        compiler_params=pltpu.CompilerParams(
            dimension_semantics=("arbitrary",), vmem_limit_bytes=VMEM_LIMIT),
        name="finish",
    )(x1, wt, g, y4, y4, y4, y4)


def _t5_bucket(dist):
    n = jnp.maximum(dist, 0)
    max_exact = REL_BUCKETS // 2
    nf = jnp.maximum(n, 1).astype(F32)
    large = max_exact + (jnp.log(nf / max_exact) / math.log(REL_MAX_DIST / max_exact)
                         * (REL_BUCKETS - max_exact)).astype(jnp.int32)
    large = jnp.minimum(large, REL_BUCKETS - 1)
    return jnp.where(n < max_exact, n, large)


def _bias_tiles(rel_bias):
    L = MOBA_BLOCK
    assert REL_MAX_DIST <= L
    d = np.arange(-L, 2 * L + 1)
    onehot = (_t5_bucket(jnp.asarray(d))[:, None] == jnp.arange(REL_BUCKETS)).astype(F32)
    by_dist = jnp.dot(onehot, rel_bias.astype(F32), precision=HIGHEST)
    by_dist = jnp.where((d >= 0)[:, None], by_dist, NEG).T * LOG2E
    H = by_dist.shape[0]
    tiles = []
    for t in range(2):
        v = jnp.concatenate([by_dist[:, (t + 1) * L:(t + 2) * L], by_dist[:, t * L:(t + 1) * L]],
                            axis=1)
        flat = jnp.tile(v, (1, L))[:, :L * (2 * L - 1)]
        tiles.append(flat.reshape(H, L, 2 * L - 1)[:, :, :L])
    return jnp.stack(tiles), by_dist[:, 3 * L]


def kernel(x, mem, rel_bias, norm_mix_g, w_in, b_gate, conv_w, norm_mem_g, w_mem_kv, w_br_att,
           w_br_conv, w_br_xatt, w_out, norm_ffn_g, w_router, b_router, w_gu, b_gu, w_down,
           b_down, norm_final_g):
    B, S, D = x.shape
    T = B * S
    depth = w_in.shape[0]
    assert depth == 1, "the finish step applies the final norm: single-layer configuration only"
    W = ATT_WIDTH
    XW = XATT_HEADS * XATT_HEAD_DIM
    E = w_router.shape[2]
    bm = EXPERT_ROWS
    tb, cfar = _bias_tiles(rel_bias)

    xc = x
    for l in range(depth):
        w_l = w_in[l]
        wk = w_l[:, W:2 * W].astype(BF16)
        wqvT = jnp.concatenate([w_l[:, :W], w_l[:, 2 * W:3 * W]], axis=1).T.astype(BF16)
        w_rest = w_l[:, 3 * W:].astype(BF16)
        wm = w_mem_kv[l]

        mkT, mv = _mem_kv(mem, norm_mem_g[l][None], wm[:, :XW].T.astype(BF16),
                          wm[:, XW:].astype(BF16))
        qT, k, vT_even, vT_odd, selbT = _qkv_select(xc, norm_mix_g[l][None], wk, wqvT)
        y_att = _moba(qT, selbT, k, vT_even, vT_odd, tb, cfar)

        x1, h2, idx, rank, wt, cnt = _merge(
            xc.reshape(T, D), y_att.reshape(T, W), mkT, mv, norm_mix_g[l][None], w_rest,
            conv_w[l], b_gate[l], w_br_att[l].astype(BF16), w_br_conv[l].astype(BF16),
            w_br_xatt[l].astype(BF16), w_out[l].astype(BF16), norm_ffn_g[l][None],
            w_router[l].T, b_router[l][:, None], S)

        counts = cnt[:, 0].astype(jnp.int32)
        padded = (counts + bm - 1) // bm * bm
        pad_ends = jnp.cumsum(padded).astype(jnp.int32)
        pad_starts = pad_ends - padded
        onehot = idx[..., None] == jnp.arange(E, dtype=jnp.int32)
        dest = (jnp.sum(jnp.where(onehot, pad_starts, 0), axis=-1) + rank).reshape(-1)
        n_rows = T * TOP_K + E * bm
        n_blocks = n_rows // bm
        n_used = pad_ends[-1] // bm
        blk = jnp.minimum(jnp.arange(n_blocks, dtype=jnp.int32), n_used - 1) * bm
        block_expert = jnp.minimum(jnp.sum(blk[:, None] >= pad_ends[None, :], axis=1),
                                   E - 1).astype(jnp.int32)

        src = _invert(dest, counts, pad_ends, n_rows + bm)

        y4 = _moe_experts(T, block_expert, n_used[None], src, h2, w_gu[l], b_gu[l][:, None],
                          w_down[l], b_down[l][:, None])
        xc = _finish(x1, wt, norm_final_g[None], y4).reshape(B, S, D)
    return xc
```

```python
import functools
import math

import jax
import jax.numpy as jnp
import numpy as np
from jax import lax
from jax.experimental import pallas as pl
from jax.experimental.pallas import tpu as pltpu

F32 = jnp.float32
BF16 = jnp.bfloat16
HIGHEST = lax.Precision.HIGHEST

ATT_HEADS = 8
ATT_HEAD_DIM = 64
ATT_WIDTH = ATT_HEADS * ATT_HEAD_DIM
MOBA_BLOCK = 256
MOBA_TOPK = 3
REL_BUCKETS = 32
REL_MAX_DIST = 128
XATT_HEADS = 4
XATT_HEAD_DIM = 128
N_EXPERTS = 32
TOP_K = 4
SWIGLU_LIMIT = 7.0
SWIGLU_ALPHA = 1.702
EPS = 1e-5
NEG = -1e30
LOG2E = math.log2(math.e)

LANES = 128
SEL_SLOTS = 16
TOKEN_TILE = 512
EXPERT_ROWS = 256
VMEM_LIMIT = 56 * 1024 * 1024


def _rms(x, g):
    return x * lax.rsqrt(jnp.mean(x * x, axis=-1, keepdims=True) + EPS) * g


def _dot(a, b):
    return jnp.dot(a, b, preferred_element_type=F32)


def _dot_nt(a, b):
    return lax.dot_general(a, b, (((1,), (1,)), ((), ())), preferred_element_type=F32)


def _sigmoid(x):
    return 1.0 / (1.0 + jnp.exp(-x))


def _split_bf16(x):
    hi = x.astype(BF16)
    return hi, (x - hi.astype(F32)).astype(BF16)


def _dot3(a, b, dot):
    m = a.shape[0]
    a_hi, a_lo = _split_bf16(a)
    b_hi, b_lo = _split_bf16(b)
    both = dot(jnp.concatenate([a_hi, a_lo], axis=0), b_hi)
    return both[:m] + both[m:] + dot(a_hi, b_lo)


def _mem_kv_kernel(mem_ref, g_ref, wkT_ref, wv_ref, mkT_ref, mv_ref):
    mn = _rms(mem_ref[0], g_ref[...]).astype(BF16)
    mkT_ref[0] = _dot_nt(wkT_ref[...], mn).astype(BF16)
    mv_ref[0] = _dot(mn, wv_ref[...]).astype(BF16)


def _mem_kv(mem, g, wkT, wv):
    B, M, D = mem.shape
    XW = wv.shape[1]
    return pl.pallas_call(
        _mem_kv_kernel,
        grid=(B,),
        in_specs=[pl.BlockSpec((1, M, D), lambda b: (b, 0, 0)),
                  pl.BlockSpec((1, D), lambda b: (0, 0)),
                  pl.BlockSpec((XW, D), lambda b: (0, 0)),
                  pl.BlockSpec((D, XW), lambda b: (0, 0))],
        out_specs=[pl.BlockSpec((1, XW, M), lambda b: (b, 0, 0)),
                   pl.BlockSpec((1, M, XW), lambda b: (b, 0, 0))],
        out_shape=[jax.ShapeDtypeStruct((B, XW, M), BF16),
                   jax.ShapeDtypeStruct((B, M, XW), BF16)],
        name="mem_kv",
    )(mem, g, wkT, wv)


def _qkv_select_kernel(x_ref, g_ref, wk_ref, wqvT_ref, qT_ref, k_ref, vTe_ref, vTo_ref, selbT_ref,
                       km_ref):
    i = pl.program_id(1)
    tq = x_ref.shape[1]
    W = ATT_WIDTH

    @pl.when(i == 0)
    def _():
        km_ref[...] = jnp.zeros_like(km_ref)

    h = _rms(x_ref[0], g_ref[...]).astype(BF16)
    k = _dot(h, wk_ref[...])
    k_ref[0] = k.astype(BF16)
    qvT = _dot_nt(wqvT_ref[...], h)
    qT = qvT[:W] * (1.0 / math.sqrt(ATT_HEAD_DIM))
    qT_ref[0] = (qT * LOG2E).astype(BF16)
    vT = qvT[W:]
    even_head = (lax.broadcasted_iota(jnp.int32, (W, tq), 0) // ATT_HEAD_DIM) % 2 == 0
    vTe_ref[0] = jnp.where(even_head, vT, 1.0).astype(BF16)
    vTo_ref[0] = jnp.where(even_head, 1.0, vT).astype(BF16)

    gate = _dot3(km_ref[...], qT, _dot)
    g3 = gate.reshape(ATT_HEADS, SEL_SLOTS, tq)
    jj = lax.broadcasted_iota(jnp.int32, g3.shape, 1).astype(F32)
    valid = jj < i.astype(F32)
    g3 = jnp.where(valid, g3, -jnp.inf)
    sel = jnp.zeros(g3.shape, F32)
    for _ in range(MOBA_TOPK):
        m = jnp.max(g3, axis=1, keepdims=True)
        first = jnp.min(jnp.where(g3 == m, jj, float(SEL_SLOTS)), axis=1, keepdims=True)
        pick = jj == first
        sel = jnp.where(pick, 1.0, sel)
        g3 = jnp.where(pick, -jnp.inf, g3)
    keep = ((sel > 0.5) & valid) | (jj == i.astype(F32))
    selbT_ref[0] = jnp.where(keep, 0.0, NEG).reshape(ATT_HEADS * SEL_SLOTS, tq).astype(BF16)

    kmean = jnp.sum(k, axis=0, keepdims=True) * (1.0 / tq)
    lane_head = lax.broadcasted_iota(jnp.int32, (1, W), 1) // ATT_HEAD_DIM
    for hh in range(ATT_HEADS):
        km_ref[pl.ds(hh * SEL_SLOTS + i, 1), :] = jnp.where(lane_head == hh, kmean, 0.0)


def _qkv_select(x, g, wk, wqvT):
    B, S, D = x.shape
    W = ATT_WIDTH
    tq = MOBA_BLOCK
    nb = S // tq
    assert nb <= SEL_SLOTS and ATT_HEADS * SEL_SLOTS == LANES
    by_row = pl.BlockSpec((1, tq, W), lambda b, i: (b, i, 0))
    by_col = pl.BlockSpec((1, W, tq), lambda b, i: (b, 0, i))
    return pl.pallas_call(
        _qkv_select_kernel,
        grid=(B, nb),
        in_specs=[pl.BlockSpec((1, tq, D), lambda b, i: (b, i, 0)),
                  pl.BlockSpec((1, D), lambda b, i: (0, 0)),
                  pl.BlockSpec((D, W), lambda b, i: (0, 0)),
                  pl.BlockSpec((2 * W, D), lambda b, i: (0, 0))],
        out_specs=[by_col, by_row, by_col, by_col,
                   pl.BlockSpec((1, LANES, tq), lambda b, i: (b, 0, i))],
        out_shape=[jax.ShapeDtypeStruct((B, W, S), BF16),
                   jax.ShapeDtypeStruct((B, S, W), BF16),
                   jax.ShapeDtypeStruct((B, W, S), BF16),
                   jax.ShapeDtypeStruct((B, W, S), BF16),
                   jax.ShapeDtypeStruct((B, LANES, S), BF16)],
        scratch_shapes=[pltpu.VMEM((LANES, W), F32)],
        compiler_params=pltpu.CompilerParams(
            dimension_semantics=("arbitrary", "arbitrary"), vmem_limit_bytes=VMEM_LIMIT),
        name="qkv_select",
    )(x, g, wk, wqvT)


def _moba_kernel(qT_ref, selbT_ref, k_ref, vTe_ref, vTo_ref, tbT_ref, cfar_ref, eT_ref, o_ref,
                 rhs_ref, m_ref, acc_ref):
    i = pl.program_id(1)
    tq = o_ref.shape[1]
    L = MOBA_BLOCK
    HD = ATT_HEAD_DIM
    H = ATT_HEADS
    selbT = selbT_ref[0].astype(F32)
    row = lax.broadcasted_iota(jnp.int32, (LANES, tq), 0)
    vT_refs = (vTe_ref, vTo_ref)
    slabs = [slice(p * LANES, (p + 1) * LANES) for p in range(H // 2)]

    for h in range(H):
        qTp = qT_ref[0, slabs[h // 2], :].astype(F32)
        own = (row >= HD) if h % 2 else (row < HD)
        rhs_ref[h, :LANES, :] = jnp.where(own, qTp, 0.0).astype(BF16)
        rhs_ref[h, LANES:, :] = jnp.where(row // SEL_SLOTS == h, selbT, 0.0).astype(BF16)
    m_ref[...] = jnp.full(m_ref.shape, -jnp.inf, F32)
    acc_ref[...] = jnp.zeros_like(acc_ref)

    def attend(js, tiles):
        koffs = [pl.multiple_of(j * L, L) for j in js]
        s = []
        for j, koff in zip(js, koffs):
            sj = []
            for p in range(H // 2):
                lhs = jnp.concatenate([k_ref[0, pl.ds(koff, L), slabs[p]], eT_ref[j]], axis=1)
                sj += [_dot(lhs, rhs_ref[2 * p + hh]) for hh in (0, 1)]
            s.append(sj)
        for sj, koff, tile in zip(s, koffs, tiles):
            pr, alpha = [], []
            for h in range(H):
                sh = sj[h] if tile is None else sj[h] + tbT_ref[tile, h]
                cm = jnp.max(jnp.max(sh.reshape(8, L // 8, tq), axis=0), axis=0, keepdims=True)
                m_old = m_ref[h:h + 1, :]
                if tile is None:
                    m_new = jnp.maximum(m_old, cm + cfar_ref[h])
                    pr.append(jnp.exp2(sh - (m_new - cfar_ref[h])).astype(BF16))
                else:
                    m_new = jnp.maximum(m_old, cm)
                    pr.append(jnp.exp2(sh - m_new).astype(BF16))
                m_ref[h:h + 1, :] = m_new
                alpha.append(jnp.exp2(m_old - m_new))
            for h in range(H):
                pv = _dot(vT_refs[h % 2][0, slabs[h // 2], pl.ds(koff, L)], pr[h])
                acc_ref[h] = alpha[h] * acc_ref[h] + pv

    n_far = jnp.maximum(i - 1, 0)

    def far_quad(jj, carry):
        attend([4 * jj + u for u in range(4)], [None] * 4)
        return carry

    lax.fori_loop(0, n_far // 4, far_quad, 0)

    @pl.when(n_far % 4 >= 2)
    def _():
        j0 = n_far // 4 * 4
        attend([j0, j0 + 1], [None, None])

    @pl.when(n_far % 2 == 1)
    def _():
        attend([n_far - 1], [None])

    @pl.when(i >= 1)
    def _():
        attend([i - 1, i], [1, 0])

    @pl.when(i == 0)
    def _():
        attend([i], [0])

    for p in range(H // 2):
        a0, a1 = acc_ref[2 * p], acc_ref[2 * p + 1]
        oT = jnp.concatenate([a0[:HD] / a0[HD:HD + 1], a1[HD:] / a1[0:1]], axis=0)
        o_ref[0, :, slabs[p]] = oT.T.astype(BF16)


def _moba(qT, selbT, k, vT_even, vT_odd, tbT, cfar):
    B, S, W = k.shape
    tq = MOBA_BLOCK
    nb = S // tq
    lane = np.arange(LANES)[None, None, :] % SEL_SLOTS
    eT_all = jnp.asarray(np.broadcast_to(lane == np.arange(SEL_SLOTS)[:, None, None],
                                         (SEL_SLOTS, MOBA_BLOCK, LANES)), BF16)
    whole_T = pl.BlockSpec((1, W, S), lambda b, i: (b, 0, 0))
    return pl.pallas_call(
        _moba_kernel,
        grid=(B, nb),
        in_specs=[pl.BlockSpec((1, W, tq), lambda b, i: (b, 0, i)),
                  pl.BlockSpec((1, LANES, tq), lambda b, i: (b, 0, i)),
                  pl.BlockSpec((1, S, W), lambda b, i: (b, 0, 0)),
                  whole_T, whole_T,
                  pl.BlockSpec(tbT.shape, lambda b, i: (0, 0, 0, 0)),
                  pl.BlockSpec(memory_space=pltpu.SMEM),
                  pl.BlockSpec(eT_all.shape, lambda b, i: (0, 0, 0))],
        out_specs=pl.BlockSpec((1, tq, W), lambda b, i: (b, i, 0)),
        out_shape=jax.ShapeDtypeStruct((B, S, W), BF16),
        scratch_shapes=[pltpu.VMEM((ATT_HEADS, 2 * LANES, tq), BF16),
                        pltpu.VMEM((ATT_HEADS, tq), F32),
                        pltpu.VMEM((ATT_HEADS, LANES, tq), F32)],
        compiler_params=pltpu.CompilerParams(
            dimension_semantics=("arbitrary", "arbitrary"), vmem_limit_bytes=VMEM_LIMIT),
        name="moba",
    )(qT, selbT, k, vT_even, vT_odd, tbT, cfar, eT_all)


def _merge_kernel(tiles_per_seq,
                  x_ref, ya_ref, mkT_ref, mv_ref, g1_ref, wr_ref, cw_ref, bg_ref,
                  wba_ref, wbc_ref, wbx_ref, wo_ref, g2_ref, wrt_ref, br_ref,
                  x1_ref, h2_ref, idx_ref, rank_ref, wt_ref, cnt_ref, zprev_ref):
    i = pl.program_id(0)
    tm, D = x_ref.shape
    CW = cw_ref.shape[1]
    XW = mv_ref.shape[2]

    @pl.when(i == 0)
    def _():
        cnt_ref[...] = jnp.zeros_like(cnt_ref)

    @pl.when(i % tiles_per_seq == 0)
    def _():
        zprev_ref[...] = jnp.zeros_like(zprev_ref)

    x = x_ref[...]
    h = _rms(x, g1_ref[...]).astype(BF16)
    pr = _dot(h, wr_ref[...])

    cb = pr[:, :CW]
    z = pr[:, CW:2 * CW] * pr[:, 2 * CW:3 * CW]
    row = lax.broadcasted_iota(jnp.int32, (tm, CW), 0)
    zp = zprev_ref[...]
    z1 = jnp.where(row == 0, zp[7:8], pltpu.roll(z, 1, 0))
    z2 = jnp.where(row == 0, zp[6:7], jnp.where(row == 1, zp[7:8], pltpu.roll(z, 2, 0)))
    zprev_ref[...] = z[tm - 8:]
    cw = cw_ref[...]
    y_conv = cb * (cw[0:1] * z2 + cw[1:2] * z1 + cw[2:3] * z)

    o0 = 3 * CW
    scale = 1.0 / math.sqrt(XATT_HEAD_DIM)
    ys = []
    for hh in range(XATT_HEADS):
        hs = slice(hh * XATT_HEAD_DIM, (hh + 1) * XATT_HEAD_DIM)
        qx = pr[:, o0 + hh * XATT_HEAD_DIM:o0 + (hh + 1) * XATT_HEAD_DIM].astype(BF16)
        s = _dot(qx, mkT_ref[0, hs, :]) * scale
        e = jnp.exp(s - jnp.max(s, axis=1, keepdims=True))
        l = jnp.sum(e, axis=1, keepdims=True)
        ys.append(_dot(e.astype(BF16), mv_ref[0, :, hs]) / l)
    y_x = jnp.concatenate(ys, axis=1)

    o1 = o0 + XW
    bg = bg_ref[...]
    merged = (_sigmoid(pr[:, o1:o1 + D] + bg[0:1]) * _dot(ya_ref[...], wba_ref[...])
              + _sigmoid(pr[:, o1 + D:o1 + 2 * D] + bg[1:2]) * _dot(y_conv.astype(BF16), wbc_ref[...])
              + _sigmoid(pr[:, o1 + 2 * D:o1 + 3 * D] + bg[2:3]) * _dot(y_x.astype(BF16), wbx_ref[...]))
    x1 = x + _dot(merged.astype(BF16), wo_ref[...])
    x1_ref[...] = x1
    h2 = _rms(x1, g2_ref[...])
    for c in range(D // LANES):
        h2_ref[pl.ds(c, tm, stride=D // LANES), :] = h2[:, c * LANES:(c + 1) * LANES]

    lg = _dot3(wrt_ref[...], h2, _dot_nt) + br_ref[...]
    E = lg.shape[0]
    ee = lax.broadcasted_iota(jnp.int32, (E, tm), 0).astype(F32)
    work = lg
    member = jnp.zeros((E, tm), F32)
    picks, vals = [], []
    for _ in range(TOP_K):
        m = jnp.max(work, axis=0, keepdims=True)
        first = jnp.min(jnp.where(work == m, ee, float(E)), axis=0, keepdims=True)
        pick = ee == first
        work = jnp.where(pick, -jnp.inf, work)
        member = jnp.where(pick, 1.0, member)
        picks.append((pick, first))
        vals.append(m)
    exps = [jnp.exp(v - vals[0]) for v in vals]
    denom = exps[0] + exps[1] + exps[2] + exps[3]

    r_i = lax.broadcasted_iota(jnp.int32, (tm, tm), 0)
    c_i = lax.broadcasted_iota(jnp.int32, (tm, tm), 1)
    upper = jnp.where(r_i < c_i, 1.0, 0.0).astype(BF16)
    before = _dot(member.astype(BF16), upper) + cnt_ref[:, 0:1]
    cnt_ref[...] = cnt_ref[...] + jnp.sum(member, axis=1, keepdims=True)

    wrow = lax.broadcasted_iota(jnp.int32, (LANES, tm), 0)
    wpad = jnp.zeros((LANES, tm), F32)
    for kk in range(TOP_K):
        pick, first = picks[kk]
        idx_ref[kk:kk + 1, :] = first.astype(jnp.int32)
        rank_ref[kk:kk + 1, :] = jnp.sum(jnp.where(pick, before, 0.0), axis=0,
                                         keepdims=True).astype(jnp.int32)
        wpad = jnp.where(wrow == kk, exps[kk] / denom, wpad)
    wt_ref[...] = wpad.T


def _merge(x2, yatt2, mkT, mv, g1, w_rest, conv_w, b_gate, wba, wbc, wbx, wo, g2, wrt, br, S):
    T, D = x2.shape
    tm = TOKEN_TILE
    nt = T // tm
    tps = S // tm
    CW = conv_w.shape[1]
    XW, M = mkT.shape[1], mkT.shape[2]
    E = wrt.shape[0]
    const = lambda shape: pl.BlockSpec(shape, lambda i: (0,) * len(shape))
    return pl.pallas_call(
        functools.partial(_merge_kernel, tps),
        grid=(nt,),
        in_specs=[pl.BlockSpec((tm, D), lambda i: (i, 0)),
                  pl.BlockSpec((tm, yatt2.shape[1]), lambda i: (i, 0)),
                  pl.BlockSpec((1, XW, M), lambda i: (i // tps, 0, 0)),
                  pl.BlockSpec((1, M, XW), lambda i: (i // tps, 0, 0)),
                  const((1, D)), const(w_rest.shape), const(conv_w.shape), const(b_gate.shape),
                  const(wba.shape), const(wbc.shape), const(wbx.shape), const(wo.shape),
                  const((1, D)), const(wrt.shape), const(br.shape)],
        out_specs=[pl.BlockSpec((tm, D), lambda i: (i, 0)),
                   pl.BlockSpec((tm * (D // LANES), LANES), lambda i: (i, 0)),
                   pl.BlockSpec((TOP_K, tm), lambda i: (0, i)),
                   pl.BlockSpec((TOP_K, tm), lambda i: (0, i)),
                   pl.BlockSpec((tm, LANES), lambda i: (i, 0)),
                   pl.BlockSpec((E, LANES), lambda i: (0, 0))],
        out_shape=[jax.ShapeDtypeStruct((T, D), F32),
                   jax.ShapeDtypeStruct((T * (D // LANES), LANES), F32),
                   jax.ShapeDtypeStruct((TOP_K, T), jnp.int32),
                   jax.ShapeDtypeStruct((TOP_K, T), jnp.int32),
                   jax.ShapeDtypeStruct((T, LANES), F32),
                   jax.ShapeDtypeStruct((E, LANES), F32)],
        scratch_shapes=[pltpu.VMEM((8, CW), F32)],
        compiler_params=pltpu.CompilerParams(
            dimension_semantics=("arbitrary",), vmem_limit_bytes=VMEM_LIMIT),
        name="merge_route",
    )(x2, yatt2, mkT, mv, g1, w_rest, conv_w, b_gate, wba, wbc, wbx, wo, g2, wrt, br)


def _invert_kernel(dest_ref, cnt_ref, pe_ref, src_ref):
    bm = EXPERT_ROWS
    n_assign = dest_ref.shape[0]
    assert bm & (bm - 1) == 0
    shift = bm.bit_length() - 1

    def pad_id(r):
        return n_assign + ((r >> shift) & 1) * bm + (r & (bm - 1))

    def put_pad(r, c):
        src_ref[r] = pad_id(r)
        return c

    for e in range(N_EXPERTS):
        first_row = (pe_ref[e - 1] if e else 0) + cnt_ref[e]
        lax.fori_loop(first_row, pe_ref[e], put_pad, 0)

    def put_pad_block(blk, c):
        for r in range(bm):
            src_ref[blk * bm + r] = n_assign + (blk % 2) * bm + r
        return c

    lax.fori_loop(pe_ref[N_EXPERTS - 1] // bm, src_ref.shape[0] // bm, put_pad_block, 0)

    def put(g, c):
        for u in range(32):
            a = g * 32 + u
            src_ref[dest_ref[a]] = a
        return c

    lax.fori_loop(0, n_assign // 32, put, 0)


def _invert(dest_flat, counts, pad_ends, n_src):
    return pl.pallas_call(
        _invert_kernel,
        grid_spec=pltpu.PrefetchScalarGridSpec(
            num_scalar_prefetch=3,
            grid=(1,),
            in_specs=[],
            out_specs=pl.BlockSpec(memory_space=pltpu.SMEM)),
        out_shape=jax.ShapeDtypeStruct((n_src,), jnp.int32),
        name="invert_routing",
    )(dest_flat, counts, pad_ends)


def _moe_kernel(T, be_ref, nu_ref, src_ref, h2t_ref, wgu_ref, bgu_ref, wd_ref, bd_ref, y4_ref,
                xbuf0, xbuf1, ybuf0, ybuf1, wgu_bf, wd_bf, gsem0, gsem1, ssem0, ssem1):
    b = pl.program_id(0)
    n_used = nu_ref[0]
    bm = EXPERT_ROWS
    F = wd_ref.shape[1]
    CH = wgu_ref.shape[1] // LANES
    xbuf = (xbuf0, xbuf1)
    ybuf = (ybuf0, ybuf1)
    gsem = (gsem0, gsem1)
    ssem = (ssem0, ssem1)

    def gather_start(blk, s):
        for r in range(bm):
            tok = src_ref[blk * bm + r] & (T - 1)
            pltpu.make_async_copy(h2t_ref.at[pl.ds(pl.multiple_of(tok * CH, CH), CH)],
                                  xbuf[s].at[pl.ds(r * CH, CH)], gsem[s]).start()

    def scatter_start(blk, s):
        for r in range(bm):
            a = src_ref[blk * bm + r]
            pltpu.make_async_copy(ybuf[s].at[pl.ds(r * CH, CH)],
                                  y4_ref.at[pl.ds(pl.multiple_of(a * CH, CH), CH)],
                                  ssem[s]).start()

    def gather_wait(s):
        pltpu.make_async_copy(h2t_ref.at[pl.ds(0, bm * CH)], xbuf[s], gsem[s]).wait()

    def scatter_wait(s):
        pltpu.make_async_copy(ybuf[s], y4_ref.at[pl.ds(0, bm * CH)], ssem[s]).wait()

    def load_rows(s):
        x = jnp.concatenate([xbuf[s][pl.ds(c, bm, stride=CH), :] for c in range(CH)], axis=1)
        return x.astype(BF16)

    def ffn(x):
        gu = _dot(x, wgu_bf[...]) + bgu_ref[0]
        g = jnp.minimum(gu[:, :F], SWIGLU_LIMIT)
        lin = jnp.clip(gu[:, F:], -SWIGLU_LIMIT, SWIGLU_LIMIT)
        act = (lin + 1.0) * (g * _sigmoid(SWIGLU_ALPHA * g))
        return _dot(act.astype(BF16), wd_bf[...]) + bd_ref[0]

    def store_rows(s, y):
        for c in range(CH):
            ybuf[s][pl.ds(c, bm, stride=CH), :] = y[:, c * LANES:(c + 1) * LANES]

    @pl.when((b < n_used) & ((b == 0) | (be_ref[b] != be_ref[jnp.maximum(b - 1, 0)])))
    def _():
        wgu_bf[...] = wgu_ref[0].astype(BF16)
        wd_bf[...] = wd_ref[0].astype(BF16)

    @pl.when(b == 0)
    def _():
        gather_start(0, 0)
        gather_wait(0)
        gather_start(1, 1)
        store_rows(0, ffn(load_rows(0)))

    for slot in (0, 1):
        other = 1 - slot

        @pl.when((b > 0) & (b < n_used) & (b % 2 == slot))
        def _():
            gather_start(b + 1, other)
            gather_wait(slot)

            @pl.when(b >= 2)
            def _():
                scatter_wait(slot)

            scatter_start(b - 1, other)
            store_rows(slot, ffn(load_rows(slot)))

        @pl.when((b == n_used - 1) & (b % 2 == slot))
        def _():
            scatter_start(b, slot)
            gather_wait(other)

            @pl.when(b >= 1)
            def _():
                scatter_wait(other)

            scatter_wait(slot)


def _moe_experts(T, block_expert, n_used, src, h2t, wgu, bgu, wd, bd):
    bm = EXPERT_ROWS
    E, D, F2 = wgu.shape
    F = wd.shape[1]
    CH = D // LANES
    n_blocks = block_expert.shape[0]
    assert T & (T - 1) == 0 and src.shape[0] == (n_blocks + 1) * bm
    per_e = lambda b, be, nu, s: (be[b], 0, 0)
    return pl.pallas_call(
        functools.partial(_moe_kernel, T),
        grid_spec=pltpu.PrefetchScalarGridSpec(
            num_scalar_prefetch=3,
            grid=(n_blocks,),
            in_specs=[pl.BlockSpec(memory_space=pl.ANY),
                      pl.BlockSpec((1, D, F2), per_e),
                      pl.BlockSpec((1, 1, F2), per_e),
                      pl.BlockSpec((1, F, D), per_e),
                      pl.BlockSpec((1, 1, D), per_e)],
            out_specs=pl.BlockSpec(memory_space=pl.ANY),
            scratch_shapes=[pltpu.VMEM((bm * CH, LANES), F32)] * 4 + [
                            pltpu.VMEM((D, F2), BF16),
                            pltpu.VMEM((F, D), BF16),
                            pltpu.SemaphoreType.DMA(())] + [pltpu.SemaphoreType.DMA(())] * 3),
        out_shape=jax.ShapeDtypeStruct(((TOP_K * T + 2 * bm) * CH, LANES), F32),
        compiler_params=pltpu.CompilerParams(
            dimension_semantics=("arbitrary",), vmem_limit_bytes=VMEM_LIMIT),
        name="moe_experts",
    )(block_expert, n_used, src, h2t, wgu, bgu, wd, bd)


def _finish_kernel(x1_ref, wt_ref, g_ref, y0_ref, y1_ref, y2_ref, y3_ref, o_ref):
    tm, D = x1_ref.shape
    CH = D // LANES
    wt = wt_ref[...]
    y = x1_ref[...]
    for kk, yk_ref in enumerate((y0_ref, y1_ref, y2_ref, y3_ref)):
        yk = jnp.concatenate([yk_ref[pl.ds(c, tm, stride=CH), :] for c in range(CH)], axis=1)
        y = y + wt[:, kk:kk + 1] * yk
    o_ref[...] = _rms(y, g_ref[...])


def _finish(x1, wt, g, y4):
    T, D = x1.shape
    tm = TOKEN_TILE
    CH = D // LANES
    nt = T // tm
    choice = lambda kk: pl.BlockSpec((tm * CH, LANES), lambda i: (kk * nt + i, 0))
    return pl.pallas_call(
        _finish_kernel,
        grid=(nt,),
        in_specs=[pl.BlockSpec((tm, D), lambda i: (i, 0)),
                  pl.BlockSpec((tm, LANES), lambda i: (i, 0)),
                  pl.BlockSpec((1, D), lambda i: (0, 0)),
                  choice(0), choice(1), choice(2), choice(3)],
        out_specs=pl.BlockSpec((tm, D), lambda i: (i, 0)),
        out_shape=jax.ShapeDtypeStruct((T, D), F32),
        compiler_params=pltpu.CompilerParams(
            dimension_semantics=("arbitrary",), vmem_limit_bytes=VMEM_LIMIT),
        name="finish",
    )(x1, wt, g, y4, y4, y4, y4)


def _t5_bucket(dist):
    n = jnp.maximum(dist, 0)
    max_exact = REL_BUCKETS // 2
    nf = jnp.maximum(n, 1).astype(F32)
    large = max_exact + (jnp.log(nf / max_exact) / math.log(REL_MAX_DIST / max_exact)
                         * (REL_BUCKETS - max_exact)).astype(jnp.int32)
    large = jnp.minimum(large, REL_BUCKETS - 1)
    return jnp.where(n < max_exact, n, large)


def _bias_tiles(rel_bias):
    L = MOBA_BLOCK
    assert REL_MAX_DIST <= L
    d = np.arange(-L, 2 * L + 1)
    onehot = (_t5_bucket(jnp.asarray(d))[:, None] == jnp.arange(REL_BUCKETS)).astype(F32)
    by_dist = jnp.dot(onehot, rel_bias.astype(F32), precision=HIGHEST)
    by_dist = jnp.where((d >= 0)[:, None], by_dist, NEG).T * LOG2E
    H = by_dist.shape[0]
    tiles = []
    for t in range(2):
        v = jnp.concatenate([by_dist[:, (t + 1) * L:(t + 2) * L], by_dist[:, t * L:(t + 1) * L]],
                            axis=1)
        flat = jnp.tile(v, (1, L))[:, :L * (2 * L - 1)]
        tiles.append(flat.reshape(H, L, 2 * L - 1)[:, :, :L])
    return jnp.stack(tiles), by_dist[:, 3 * L]


def kernel(x, mem, rel_bias, norm_mix_g, w_in, b_gate, conv_w, norm_mem_g, w_mem_kv, w_br_att,
           w_br_conv, w_br_xatt, w_out, norm_ffn_g, w_router, b_router, w_gu, b_gu, w_down,
           b_down, norm_final_g):
    B, S, D = x.shape
    T = B * S
    depth = w_in.shape[0]
    assert depth == 1, "the finish step applies the final norm: single-layer configuration only"
    W = ATT_WIDTH
    XW = XATT_HEADS * XATT_HEAD_DIM
    E = w_router.shape[2]
    bm = EXPERT_ROWS
    tb, cfar = _bias_tiles(rel_bias)

    xc = x
    for l in range(depth):
        w_l = w_in[l]
        wk = w_l[:, W:2 * W].astype(BF16)
        wqvT = jnp.concatenate([w_l[:, :W], w_l[:, 2 * W:3 * W]], axis=1).T.astype(BF16)
        w_rest = w_l[:, 3 * W:].astype(BF16)
        wm = w_mem_kv[l]

        mkT, mv = _mem_kv(mem, norm_mem_g[l][None], wm[:, :XW].T.astype(BF16),
                          wm[:, XW:].astype(BF16))
        qT, k, vT_even, vT_odd, selbT = _qkv_select(xc, norm_mix_g[l][None], wk, wqvT)
        y_att = _moba(qT, selbT, k, vT_even, vT_odd, tb, cfar)

        x1, h2, idx, rank, wt, cnt = _merge(
            xc.reshape(T, D), y_att.reshape(T, W), mkT, mv, norm_mix_g[l][None], w_rest,
            conv_w[l], b_gate[l], w_br_att[l].astype(BF16), w_br_conv[l].astype(BF16),
            w_br_xatt[l].astype(BF16), w_out[l].astype(BF16), norm_ffn_g[l][None],
            w_router[l].T, b_router[l][:, None], S)

        counts = cnt[:, 0].astype(jnp.int32)
        padded = (counts + bm - 1) // bm * bm
        pad_ends = jnp.cumsum(padded).astype(jnp.int32)
        pad_starts = pad_ends - padded
        onehot = idx[..., None] == jnp.arange(E, dtype=jnp.int32)
        dest = (jnp.sum(jnp.where(onehot, pad_starts, 0), axis=-1) + rank).reshape(-1)
        n_rows = T * TOP_K + E * bm
        n_blocks = n_rows // bm
        n_used = pad_ends[-1] // bm
        blk = jnp.minimum(jnp.arange(n_blocks, dtype=jnp.int32), n_used - 1) * bm
        block_expert = jnp.minimum(jnp.sum(blk[:, None] >= pad_ends[None, :], axis=1),
                                   E - 1).astype(jnp.int32)

        src = _invert(dest, counts, pad_ends, n_rows + bm)

        y4 = _moe_experts(T, block_expert, n_used[None], src, h2, w_gu[l], b_gu[l][:, None],
                          w_down[l], b_down[l][:, None])
        xc = _finish(x1, wt, norm_final_g[None], y4).reshape(B, S, D)
    return xc
```

```python
import functools
import math

import jax
import jax.numpy as jnp
import numpy as np
from jax import lax
from jax.experimental import pallas as pl
from jax.experimental.pallas import tpu as pltpu

F32 = jnp.float32
BF16 = jnp.bfloat16
HIGHEST = lax.Precision.HIGHEST

ATT_HEADS = 8
ATT_HEAD_DIM = 64
ATT_WIDTH = ATT_HEADS * ATT_HEAD_DIM
MOBA_BLOCK = 256
MOBA_TOPK = 3
REL_BUCKETS = 32
REL_MAX_DIST = 128
XATT_HEADS = 4
XATT_HEAD_DIM = 128
N_EXPERTS = 32
TOP_K = 4
SWIGLU_LIMIT = 7.0
SWIGLU_ALPHA = 1.702
EPS = 1e-5
NEG = -1e30
LOG2E = math.log2(math.e)

LANES = 128
SEL_SLOTS = 16
TOKEN_TILE = 512
EXPERT_ROWS = 256
VMEM_LIMIT = 56 * 1024 * 1024


def _rms(x, g):
    return x * lax.rsqrt(jnp.mean(x * x, axis=-1, keepdims=True) + EPS) * g


def _dot(a, b):
    return jnp.dot(a, b, preferred_element_type=F32)


def _dot_nt(a, b):
    return lax.dot_general(a, b, (((1,), (1,)), ((), ())), preferred_element_type=F32)


def _sigmoid(x):
    return 1.0 / (1.0 + jnp.exp(-x))


def _split_bf16(x):
    hi = x.astype(BF16)
    return hi, (x - hi.astype(F32)).astype(BF16)


def _dot3(a, b, dot):
    m = a.shape[0]
    a_hi, a_lo = _split_bf16(a)
    b_hi, b_lo = _split_bf16(b)
    both = dot(jnp.concatenate([a_hi, a_lo], axis=0), b_hi)
    return both[:m] + both[m:] + dot(a_hi, b_lo)


def _mem_kv_kernel(mem_ref, g_ref, wkT_ref, wv_ref, mkT_ref, mv_ref):
    mn = _rms(mem_ref[0], g_ref[...]).astype(BF16)
    mkT_ref[0] = _dot_nt(wkT_ref[...], mn).astype(BF16)
    mv_ref[0] = _dot(mn, wv_ref[...]).astype(BF16)


def _mem_kv(mem, g, wkT, wv):
    B, M, D = mem.shape
    XW = wv.shape[1]
    return pl.pallas_call(
        _mem_kv_kernel,
        grid=(B,),
        in_specs=[pl.BlockSpec((1, M, D), lambda b: (b, 0, 0)),
                  pl.BlockSpec((1, D), lambda b: (0, 0)),
                  pl.BlockSpec((XW, D), lambda b: (0, 0)),
                  pl.BlockSpec((D, XW), lambda b: (0, 0))],
        out_specs=[pl.BlockSpec((1, XW, M), lambda b: (b, 0, 0)),
                   pl.BlockSpec((1, M, XW), lambda b: (b, 0, 0))],
        out_shape=[jax.ShapeDtypeStruct((B, XW, M), BF16),
                   jax.ShapeDtypeStruct((B, M, XW), BF16)],
        name="mem_kv",
    )(mem, g, wkT, wv)


def _qkv_select_kernel(x_ref, g_ref, wk_ref, wqvT_ref, qT_ref, k_ref, vTe_ref, vTo_ref, selbT_ref,
                       km_ref):
    i = pl.program_id(1)
    tq = x_ref.shape[1]
    W = ATT_WIDTH

    @pl.when(i == 0)
    def _():
        km_ref[...] = jnp.zeros_like(km_ref)

    h = _rms(x_ref[0], g_ref[...]).astype(BF16)
    k = _dot(h, wk_ref[...])
    k_ref[0] = k.astype(BF16)
    qvT = _dot_nt(wqvT_ref[...], h)
    qT = qvT[:W] * (1.0 / math.sqrt(ATT_HEAD_DIM))
    qT_ref[0] = (qT * LOG2E).astype(BF16)
    vT = qvT[W:]
    even_head = (lax.broadcasted_iota(jnp.int32, (W, tq), 0) // ATT_HEAD_DIM) % 2 == 0
    vTe_ref[0] = jnp.where(even_head, vT, 1.0).astype(BF16)
    vTo_ref[0] = jnp.where(even_head, 1.0, vT).astype(BF16)

    gate = _dot3(km_ref[...], qT, _dot)
    g3 = gate.reshape(ATT_HEADS, SEL_SLOTS, tq)
    jj = lax.broadcasted_iota(jnp.int32, g3.shape, 1).astype(F32)
    valid = jj < i.astype(F32)
    g3 = jnp.where(valid, g3, -jnp.inf)
    sel = jnp.zeros(g3.shape, F32)
    for _ in range(MOBA_TOPK):
        m = jnp.max(g3, axis=1, keepdims=True)
        first = jnp.min(jnp.where(g3 == m, jj, float(SEL_SLOTS)), axis=1, keepdims=True)
        pick = jj == first
        sel = jnp.where(pick, 1.0, sel)
        g3 = jnp.where(pick, -jnp.inf, g3)
    keep = ((sel > 0.5) & valid) | (jj == i.astype(F32))
    selbT_ref[0] = jnp.where(keep, 0.0, NEG).reshape(ATT_HEADS * SEL_SLOTS, tq).astype(BF16)

    kmean = jnp.sum(k, axis=0, keepdims=True) * (1.0 / tq)
    lane_head = lax.broadcasted_iota(jnp.int32, (1, W), 1) // ATT_HEAD_DIM
    for hh in range(ATT_HEADS):
        km_ref[pl.ds(hh * SEL_SLOTS + i, 1), :] = jnp.where(lane_head == hh, kmean, 0.0)


def _qkv_select(x, g, wk, wqvT):
    B, S, D = x.shape
    W = ATT_WIDTH
    tq = MOBA_BLOCK
    nb = S // tq
    assert nb <= SEL_SLOTS and ATT_HEADS * SEL_SLOTS == LANES
    by_row = pl.BlockSpec((1, tq, W), lambda b, i: (b, i, 0))
    by_col = pl.BlockSpec((1, W, tq), lambda b, i: (b, 0, i))
    return pl.pallas_call(
        _qkv_select_kernel,
        grid=(B, nb),
        in_specs=[pl.BlockSpec((1, tq, D), lambda b, i: (b, i, 0)),
                  pl.BlockSpec((1, D), lambda b, i: (0, 0)),
                  pl.BlockSpec((D, W), lambda b, i: (0, 0)),
                  pl.BlockSpec((2 * W, D), lambda b, i: (0, 0))],
        out_specs=[by_col, by_row, by_col, by_col,
                   pl.BlockSpec((1, LANES, tq), lambda b, i: (b, 0, i))],
        out_shape=[jax.ShapeDtypeStruct((B, W, S), BF16),
                   jax.ShapeDtypeStruct((B, S, W), BF16),
                   jax.ShapeDtypeStruct((B, W, S), BF16),
                   jax.ShapeDtypeStruct((B, W, S), BF16),
                   jax.ShapeDtypeStruct((B, LANES, S), BF16)],
        scratch_shapes=[pltpu.VMEM((LANES, W), F32)],
        compiler_params=pltpu.CompilerParams(
            dimension_semantics=("arbitrary", "arbitrary"), vmem_limit_bytes=VMEM_LIMIT),
        name="qkv_select",
    )(x, g, wk, wqvT)


def _moba_kernel(qT_ref, selbT_ref, k_ref, vTe_ref, vTo_ref, tbT_ref, cfar_ref, eT_ref, o_ref,
                 rhs_ref, m_ref, acc_ref):
    i = pl.program_id(1)
    tq = o_ref.shape[1]
    L = MOBA_BLOCK
    HD = ATT_HEAD_DIM
    H = ATT_HEADS
    selbT = selbT_ref[0].astype(F32)
    row = lax.broadcasted_iota(jnp.int32, (LANES, tq), 0)
    vT_refs = (vTe_ref, vTo_ref)
    slabs = [slice(p * LANES, (p + 1) * LANES) for p in range(H // 2)]

    for h in range(H):
        qTp = qT_ref[0, slabs[h // 2], :].astype(F32)
        own = (row >= HD) if h % 2 else (row < HD)
        rhs_ref[h, :LANES, :] = jnp.where(own, qTp, 0.0).astype(BF16)
        rhs_ref[h, LANES:, :] = jnp.where(row // SEL_SLOTS == h, selbT, 0.0).astype(BF16)
    m_ref[...] = jnp.full(m_ref.shape, -jnp.inf, F32)
    acc_ref[...] = jnp.zeros_like(acc_ref)

    def attend(js, tiles):
        koffs = [pl.multiple_of(j * L, L) for j in js]
        s = []
        for j, koff in zip(js, koffs):
            sj = []
            for p in range(H // 2):
                lhs = jnp.concatenate([k_ref[0, pl.ds(koff, L), slabs[p]], eT_ref[j]], axis=1)
                sj += [_dot(lhs, rhs_ref[2 * p + hh]) for hh in (0, 1)]
            s.append(sj)
        for sj, koff, tile in zip(s, koffs, tiles):
            pr, alpha = [], []
            for h in range(H):
                sh = sj[h] if tile is None else sj[h] + tbT_ref[tile, h]
                cm = jnp.max(jnp.max(sh.reshape(8, L // 8, tq), axis=0), axis=0, keepdims=True)
                m_old = m_ref[h:h + 1, :]
                if tile is None:
                    m_new = jnp.maximum(m_old, cm + cfar_ref[h])
                    pr.append(jnp.exp2(sh - (m_new - cfar_ref[h])).astype(BF16))
                else:
                    m_new = jnp.maximum(m_old, cm)
                    pr.append(jnp.exp2(sh - m_new).astype(BF16))
                m_ref[h:h + 1, :] = m_new
                alpha.append(jnp.exp2(m_old - m_new))
            for h in range(H):
                pv = _dot(vT_refs[h % 2][0, slabs[h // 2], pl.ds(koff, L)], pr[h])
                acc_ref[h] = alpha[h] * acc_ref[h] + pv

    n_far = jnp.maximum(i - 1, 0)

    def far_quad(jj, carry):
        attend([4 * jj + u for u in range(4)], [None] * 4)
        return carry

    lax.fori_loop(0, n_far // 4, far_quad, 0)

    @pl.when(n_far % 4 >= 2)
    def _():
        j0 = n_far // 4 * 4
        attend([j0, j0 + 1], [None, None])

    @pl.when(n_far % 2 == 1)
    def _():
        attend([n_far - 1], [None])

    @pl.when(i >= 1)
    def _():
        attend([i - 1, i], [1, 0])

    @pl.when(i == 0)
    def _():
        attend([i], [0])

    for p in range(H // 2):
        a0, a1 = acc_ref[2 * p], acc_ref[2 * p + 1]
        oT = jnp.concatenate([a0[:HD] / a0[HD:HD + 1], a1[HD:] / a1[0:1]], axis=0)
        o_ref[0, :, slabs[p]] = oT.T.astype(BF16)


def _moba(qT, selbT, k, vT_even, vT_odd, tbT, cfar):
    B, S, W = k.shape
    tq = MOBA_BLOCK
    nb = S // tq
    lane = np.arange(LANES)[None, None, :] % SEL_SLOTS
    eT_all = jnp.asarray(np.broadcast_to(lane == np.arange(SEL_SLOTS)[:, None, None],
                                         (SEL_SLOTS, MOBA_BLOCK, LANES)), BF16)
    whole_T = pl.BlockSpec((1, W, S), lambda b, i: (b, 0, 0))
    return pl.pallas_call(
        _moba_kernel,
        grid=(B, nb),
        in_specs=[pl.BlockSpec((1, W, tq), lambda b, i: (b, 0, i)),
                  pl.BlockSpec((1, LANES, tq), lambda b, i: (b, 0, i)),
                  pl.BlockSpec((1, S, W), lambda b, i: (b, 0, 0)),
                  whole_T, whole_T,
                  pl.BlockSpec(tbT.shape, lambda b, i: (0, 0, 0, 0)),
                  pl.BlockSpec(memory_space=pltpu.SMEM),
                  pl.BlockSpec(eT_all.shape, lambda b, i: (0, 0, 0))],
        out_specs=pl.BlockSpec((1, tq, W), lambda b, i: (b, i, 0)),
        out_shape=jax.ShapeDtypeStruct((B, S, W), BF16),
        scratch_shapes=[pltpu.VMEM((ATT_HEADS, 2 * LANES, tq), BF16),
                        pltpu.VMEM((ATT_HEADS, tq), F32),
                        pltpu.VMEM((ATT_HEADS, LANES, tq), F32)],
        compiler_params=pltpu.CompilerParams(
            dimension_semantics=("arbitrary", "arbitrary"), vmem_limit_bytes=VMEM_LIMIT),
        name="moba",
    )(qT, selbT, k, vT_even, vT_odd, tbT, cfar, eT_all)


def _merge_kernel(tiles_per_seq,
                  x_ref, ya_ref, mkT_ref, mv_ref, g1_ref, wr_ref, cw_ref, bg_ref,
                  wba_ref, wbc_ref, wbx_ref, wo_ref, g2_ref, wrt_ref, br_ref,
                  x1_ref, h2_ref, idx_ref, rank_ref, wt_ref, cnt_ref, zprev_ref):
    i = pl.program_id(0)
    tm, D = x_ref.shape
    CW = cw_ref.shape[1]
    XW = mv_ref.shape[2]

    @pl.when(i == 0)
    def _():
        cnt_ref[...] = jnp.zeros_like(cnt_ref)

    @pl.when(i % tiles_per_seq == 0)
    def _():
        zprev_ref[...] = jnp.zeros_like(zprev_ref)

    x = x_ref[...]
    h = _rms(x, g1_ref[...]).astype(BF16)
    pr = _dot(h, wr_ref[...])

    cb = pr[:, :CW]
    z = pr[:, CW:2 * CW] * pr[:, 2 * CW:3 * CW]
    row = lax.broadcasted_iota(jnp.int32, (tm, CW), 0)
    zp = zprev_ref[...]
    z1 = jnp.where(row == 0, zp[7:8], pltpu.roll(z, 1, 0))
    z2 = jnp.where(row == 0, zp[6:7], jnp.where(row == 1, zp[7:8], pltpu.roll(z, 2, 0)))
    zprev_ref[...] = z[tm - 8:]
    cw = cw_ref[...]
    y_conv = cb * (cw[0:1] * z2 + cw[1:2] * z1 + cw[2:3] * z)

    o0 = 3 * CW
    scale = 1.0 / math.sqrt(XATT_HEAD_DIM)
    ys = []
    for hh in range(XATT_HEADS):
        hs = slice(hh * XATT_HEAD_DIM, (hh + 1) * XATT_HEAD_DIM)
        qx = pr[:, o0 + hh * XATT_HEAD_DIM:o0 + (hh + 1) * XATT_HEAD_DIM].astype(BF16)
        s = _dot(qx, mkT_ref[0, hs, :]) * scale
        e = jnp.exp(s - jnp.max(s, axis=1, keepdims=True))
        l = jnp.sum(e, axis=1, keepdims=True)
        ys.append(_dot(e.astype(BF16), mv_ref[0, :, hs]) / l)
    y_x = jnp.concatenate(ys, axis=1)

    o1 = o0 + XW
    bg = bg_ref[...]
    merged = (_sigmoid(pr[:, o1:o1 + D] + bg[0:1]) * _dot(ya_ref[...], wba_ref[...])
              + _sigmoid(pr[:, o1 + D:o1 + 2 * D] + bg[1:2]) * _dot(y_conv.astype(BF16), wbc_ref[...])
              + _sigmoid(pr[:, o1 + 2 * D:o1 + 3 * D] + bg[2:3]) * _dot(y_x.astype(BF16), wbx_ref[...]))
    x1 = x + _dot(merged.astype(BF16), wo_ref[...])
    x1_ref[...] = x1
    h2 = _rms(x1, g2_ref[...])
    for c in range(D // LANES):
        h2_ref[pl.ds(c, tm, stride=D // LANES), :] = h2[:, c * LANES:(c + 1) * LANES]

    lg = _dot3(wrt_ref[...], h2, _dot_nt) + br_ref[...]
    E = lg.shape[0]
    ee = lax.broadcasted_iota(jnp.int32, (E, tm), 0).astype(F32)
    work = lg
    member = jnp.zeros((E, tm), F32)
    picks, vals = [], []
    for _ in range(TOP_K):
        m = jnp.max(work, axis=0, keepdims=True)
        first = jnp.min(jnp.where(work == m, ee, float(E)), axis=0, keepdims=True)
        pick = ee == first
        work = jnp.where(pick, -jnp.inf, work)
        member = jnp.where(pick, 1.0, member)
        picks.append((pick, first))
        vals.append(m)
    exps = [jnp.exp(v - vals[0]) for v in vals]
    denom = exps[0] + exps[1] + exps[2] + exps[3]

    r_i = lax.broadcasted_iota(jnp.int32, (tm, tm), 0)
    c_i = lax.broadcasted_iota(jnp.int32, (tm, tm), 1)
    upper = jnp.where(r_i < c_i, 1.0, 0.0).astype(BF16)
    before = _dot(member.astype(BF16), upper) + cnt_ref[:, 0:1]
    cnt_ref[...] = cnt_ref[...] + jnp.sum(member, axis=1, keepdims=True)

    wrow = lax.broadcasted_iota(jnp.int32, (LANES, tm), 0)
    wpad = jnp.zeros((LANES, tm), F32)
    for kk in range(TOP_K):
        pick, first = picks[kk]
        idx_ref[kk:kk + 1, :] = first.astype(jnp.int32)
        rank_ref[kk:kk + 1, :] = jnp.sum(jnp.where(pick, before, 0.0), axis=0,
                                         keepdims=True).astype(jnp.int32)
        wpad = jnp.where(wrow == kk, exps[kk] / denom, wpad)
    wt_ref[...] = wpad.T


def _merge(x2, yatt2, mkT, mv, g1, w_rest, conv_w, b_gate, wba, wbc, wbx, wo, g2, wrt, br, S):
    T, D = x2.shape
    tm = TOKEN_TILE
    nt = T // tm
    tps = S // tm
    CW = conv_w.shape[1]
    XW, M = mkT.shape[1], mkT.shape[2]
    E = wrt.shape[0]
    const = lambda shape: pl.BlockSpec(shape, lambda i: (0,) * len(shape))
    return pl.pallas_call(
        functools.partial(_merge_kernel, tps),
        grid=(nt,),
        in_specs=[pl.BlockSpec((tm, D), lambda i: (i, 0)),
                  pl.BlockSpec((tm, yatt2.shape[1]), lambda i: (i, 0)),
                  pl.BlockSpec((1, XW, M), lambda i: (i // tps, 0, 0)),
                  pl.BlockSpec((1, M, XW), lambda i: (i // tps, 0, 0)),
                  const((1, D)), const(w_rest.shape), const(conv_w.shape), const(b_gate.shape),
                  const(wba.shape), const(wbc.shape), const(wbx.shape), const(wo.shape),
                  const((1, D)), const(wrt.shape), const(br.shape)],
        out_specs=[pl.BlockSpec((tm, D), lambda i: (i, 0)),
                   pl.BlockSpec((tm * (D // LANES), LANES), lambda i: (i, 0)),
                   pl.BlockSpec((TOP_K, tm), lambda i: (0, i)),
                   pl.BlockSpec((TOP_K, tm), lambda i: (0, i)),
                   pl.BlockSpec((tm, LANES), lambda i: (i, 0)),
                   pl.BlockSpec((E, LANES), lambda i: (0, 0))],
        out_shape=[jax.ShapeDtypeStruct((T, D), F32),
                   jax.ShapeDtypeStruct((T * (D // LANES), LANES), F32),
                   jax.ShapeDtypeStruct((TOP_K, T), jnp.int32),
                   jax.ShapeDtypeStruct((TOP_K, T), jnp.int32),
                   jax.ShapeDtypeStruct((T, LANES), F32),
                   jax.ShapeDtypeStruct((E, LANES), F32)],
        scratch_shapes=[pltpu.VMEM((8, CW), F32)],
        compiler_params=pltpu.CompilerParams(
            dimension_semantics=("arbitrary",), vmem_limit_bytes=VMEM_LIMIT),
        name="merge_route",
    )(x2, yatt2, mkT, mv, g1, w_rest, conv_w, b_gate, wba, wbc, wbx, wo, g2, wrt, br)


def _invert_kernel(dest_ref, pe_ref, src_ref):
    bm = EXPERT_ROWS
    n_assign = dest_ref.shape[0]

    def put_pad_block(blk, c):
        for r in range(bm):
            src_ref[blk * bm + r] = n_assign + (blk % 2) * bm + r
        return c

    def put_last_block(e, c):
        first_row = jnp.where(e > 0, pe_ref[jnp.maximum(e - 1, 0)], 0)

        @pl.when(pe_ref[e] > first_row)
        def _():
            put_pad_block(pe_ref[e] // bm - 1, 0)

        return c

    lax.fori_loop(0, N_EXPERTS, put_last_block, 0)

    lax.fori_loop(pe_ref[N_EXPERTS - 1] // bm, src_ref.shape[0] // bm, put_pad_block, 0)

    def put(g, c):
        for u in range(32):
            a = g * 32 + u
            src_ref[dest_ref[a]] = a
        return c

    lax.fori_loop(0, n_assign // 32, put, 0)


def _invert(dest_flat, pad_ends, n_src):
    return pl.pallas_call(
        _invert_kernel,
        grid_spec=pltpu.PrefetchScalarGridSpec(
            num_scalar_prefetch=2,
            grid=(1,),
            in_specs=[],
            out_specs=pl.BlockSpec(memory_space=pltpu.SMEM)),
        out_shape=jax.ShapeDtypeStruct((n_src,), jnp.int32),
        name="invert_routing",
    )(dest_flat, pad_ends)


def _moe_kernel(T, be_ref, nu_ref, src_ref, h2t_ref, wgu_ref, bgu_ref, wd_ref, bd_ref, y4_ref,
                xbuf0, xbuf1, ybuf0, ybuf1, wgu_bf, wd_bf, gsem0, gsem1, ssem0, ssem1):
    b = pl.program_id(0)
    n_used = nu_ref[0]
    bm = EXPERT_ROWS
    F = wd_ref.shape[1]
    CH = wgu_ref.shape[1] // LANES
    xbuf = (xbuf0, xbuf1)
    ybuf = (ybuf0, ybuf1)
    gsem = (gsem0, gsem1)
    ssem = (ssem0, ssem1)

    def gather_start(blk, s):
        for r in range(bm):
            tok = src_ref[blk * bm + r] & (T - 1)
            pltpu.make_async_copy(h2t_ref.at[pl.ds(pl.multiple_of(tok * CH, CH), CH)],
                                  xbuf[s].at[pl.ds(r * CH, CH)], gsem[s]).start()

    def scatter_start(blk, s):
        for r in range(bm):
            a = src_ref[blk * bm + r]
            pltpu.make_async_copy(ybuf[s].at[pl.ds(r * CH, CH)],
                                  y4_ref.at[pl.ds(pl.multiple_of(a * CH, CH), CH)],
                                  ssem[s]).start()

    def gather_wait(s):
        pltpu.make_async_copy(h2t_ref.at[pl.ds(0, bm * CH)], xbuf[s], gsem[s]).wait()

    def scatter_wait(s):
        pltpu.make_async_copy(ybuf[s], y4_ref.at[pl.ds(0, bm * CH)], ssem[s]).wait()

    def load_rows(s):
        x = jnp.concatenate([xbuf[s][pl.ds(c, bm, stride=CH), :] for c in range(CH)], axis=1)
        return x.astype(BF16)

    def ffn(x):
        gu = _dot(x, wgu_bf[...]) + bgu_ref[0]
        g = jnp.minimum(gu[:, :F], SWIGLU_LIMIT)
        lin = jnp.clip(gu[:, F:], -SWIGLU_LIMIT, SWIGLU_LIMIT)
        act = (lin + 1.0) * (g * _sigmoid(SWIGLU_ALPHA * g))
        return _dot(act.astype(BF16), wd_bf[...]) + bd_ref[0]

    def store_rows(s, y):
        for c in range(CH):
            ybuf[s][pl.ds(c, bm, stride=CH), :] = y[:, c * LANES:(c + 1) * LANES]

    @pl.when((b < n_used) & ((b == 0) | (be_ref[b] != be_ref[jnp.maximum(b - 1, 0)])))
    def _():
        wgu_bf[...] = wgu_ref[0].astype(BF16)
        wd_bf[...] = wd_ref[0].astype(BF16)

    @pl.when(b == 0)
    def _():
        gather_start(0, 0)
        gather_wait(0)
        gather_start(1, 1)
        store_rows(0, ffn(load_rows(0)))

    for slot in (0, 1):
        other = 1 - slot

        @pl.when((b > 0) & (b < n_used) & (b % 2 == slot))
        def _():
            gather_start(b + 1, other)
            gather_wait(slot)

            @pl.when(b >= 2)
            def _():
                scatter_wait(slot)

            scatter_start(b - 1, other)
            store_rows(slot, ffn(load_rows(slot)))

        @pl.when((b == n_used - 1) & (b % 2 == slot))
        def _():
            scatter_start(b, slot)
            gather_wait(other)

            @pl.when(b >= 1)
            def _():
                scatter_wait(other)

            scatter_wait(slot)


def _moe_experts(T, block_expert, n_used, src, h2t, wgu, bgu, wd, bd):
    bm = EXPERT_ROWS
    E, D, F2 = wgu.shape
    F = wd.shape[1]
    CH = D // LANES
    n_blocks = block_expert.shape[0]
    assert T & (T - 1) == 0 and src.shape[0] == (n_blocks + 1) * bm
    per_e = lambda b, be, nu, s: (be[b], 0, 0)
    return pl.pallas_call(
        functools.partial(_moe_kernel, T),
        grid_spec=pltpu.PrefetchScalarGridSpec(
            num_scalar_prefetch=3,
            grid=(n_blocks,),
            in_specs=[pl.BlockSpec(memory_space=pl.ANY),
                      pl.BlockSpec((1, D, F2), per_e),
                      pl.BlockSpec((1, 1, F2), per_e),
                      pl.BlockSpec((1, F, D), per_e),
                      pl.BlockSpec((1, 1, D), per_e)],
            out_specs=pl.BlockSpec(memory_space=pl.ANY),
            scratch_shapes=[pltpu.VMEM((bm * CH, LANES), F32)] * 4 + [
                            pltpu.VMEM((D, F2), BF16),
                            pltpu.VMEM((F, D), BF16),
                            pltpu.SemaphoreType.DMA(())] + [pltpu.SemaphoreType.DMA(())] * 3),
        out_shape=jax.ShapeDtypeStruct(((TOP_K * T + 2 * bm) * CH, LANES), F32),
        compiler_params=pltpu.CompilerParams(
            dimension_semantics=("arbitrary",), vmem_limit_bytes=VMEM_LIMIT),
        name="moe_experts",
    )(block_expert, n_used, src, h2t, wgu, bgu, wd, bd)


def _finish_kernel(x1_ref, wt_ref, g_ref, y0_ref, y1_ref, y2_ref, y3_ref, o_ref):
    tm, D = x1_ref.shape
    CH = D // LANES
    wt = wt_ref[...]
    y = x1_ref[...]
    for kk, yk_ref in enumerate((y0_ref, y1_ref, y2_ref, y3_ref)):
        yk = jnp.concatenate([yk_ref[pl.ds(c, tm, stride=CH), :] for c in range(CH)], axis=1)
        y = y + wt[:, kk:kk + 1] * yk
    o_ref[...] = _rms(y, g_ref[...])


def _finish(x1, wt, g, y4):
    T, D = x1.shape
    tm = TOKEN_TILE
    CH = D // LANES
    nt = T // tm
    choice = lambda kk: pl.BlockSpec((tm * CH, LANES), lambda i: (kk * nt + i, 0))
    return pl.pallas_call(
        _finish_kernel,
        grid=(nt,),
        in_specs=[pl.BlockSpec((tm, D), lambda i: (i, 0)),
                  pl.BlockSpec((tm, LANES), lambda i: (i, 0)),
                  pl.BlockSpec((1, D), lambda i: (0, 0)),
                  choice(0), choice(1), choice(2), choice(3)],
        out_specs=pl.BlockSpec((tm, D), lambda i: (i, 0)),
        out_shape=jax.ShapeDtypeStruct((T, D), F32),
        compiler_params=pltpu.CompilerParams(
            dimension_semantics=("arbitrary",), vmem_limit_bytes=VMEM_LIMIT),
        name="finish",
    )(x1, wt, g, y4, y4, y4, y4)


def _t5_bucket(dist):
    n = jnp.maximum(dist, 0)
    max_exact = REL_BUCKETS // 2
    nf = jnp.maximum(n, 1).astype(F32)
    large = max_exact + (jnp.log(nf / max_exact) / math.log(REL_MAX_DIST / max_exact)
                         * (REL_BUCKETS - max_exact)).astype(jnp.int32)
    large = jnp.minimum(large, REL_BUCKETS - 1)
    return jnp.where(n < max_exact, n, large)


def _bias_tiles(rel_bias):
    L = MOBA_BLOCK
    assert REL_MAX_DIST <= L
    d = np.arange(-L, 2 * L + 1)
    onehot = (_t5_bucket(jnp.asarray(d))[:, None] == jnp.arange(REL_BUCKETS)).astype(F32)
    by_dist = jnp.dot(onehot, rel_bias.astype(F32), precision=HIGHEST)
    by_dist = jnp.where((d >= 0)[:, None], by_dist, NEG).T * LOG2E
    H = by_dist.shape[0]
    tiles = []
    for t in range(2):
        v = jnp.concatenate([by_dist[:, (t + 1) * L:(t + 2) * L], by_dist[:, t * L:(t + 1) * L]],
                            axis=1)
        flat = jnp.tile(v, (1, L))[:, :L * (2 * L - 1)]
        tiles.append(flat.reshape(H, L, 2 * L - 1)[:, :, :L])
    return jnp.stack(tiles), by_dist[:, 3 * L]


def kernel(x, mem, rel_bias, norm_mix_g, w_in, b_gate, conv_w, norm_mem_g, w_mem_kv, w_br_att,
           w_br_conv, w_br_xatt, w_out, norm_ffn_g, w_router, b_router, w_gu, b_gu, w_down,
           b_down, norm_final_g):
    B, S, D = x.shape
    T = B * S
    depth = w_in.shape[0]
    assert depth == 1, "the finish step applies the final norm: single-layer configuration only"
    W = ATT_WIDTH
    XW = XATT_HEADS * XATT_HEAD_DIM
    E = w_router.shape[2]
    bm = EXPERT_ROWS
    tb, cfar = _bias_tiles(rel_bias)

    xc = x
    for l in range(depth):
        w_l = w_in[l]
        wk = w_l[:, W:2 * W].astype(BF16)
        wqvT = jnp.concatenate([w_l[:, :W], w_l[:, 2 * W:3 * W]], axis=1).T.astype(BF16)
        w_rest = w_l[:, 3 * W:].astype(BF16)
        wm = w_mem_kv[l]

        mkT, mv = _mem_kv(mem, norm_mem_g[l][None], wm[:, :XW].T.astype(BF16),
                          wm[:, XW:].astype(BF16))
        qT, k, vT_even, vT_odd, selbT = _qkv_select(xc, norm_mix_g[l][None], wk, wqvT)
        y_att = _moba(qT, selbT, k, vT_even, vT_odd, tb, cfar)

        x1, h2, idx, rank, wt, cnt = _merge(
            xc.reshape(T, D), y_att.reshape(T, W), mkT, mv, norm_mix_g[l][None], w_rest,
            conv_w[l], b_gate[l], w_br_att[l].astype(BF16), w_br_conv[l].astype(BF16),
            w_br_xatt[l].astype(BF16), w_out[l].astype(BF16), norm_ffn_g[l][None],
            w_router[l].T, b_router[l][:, None], S)

        counts = cnt[:, 0].astype(jnp.int32)
        padded = (counts + bm - 1) // bm * bm
        pad_ends = jnp.cumsum(padded).astype(jnp.int32)
        pad_starts = pad_ends - padded
        onehot = idx[..., None] == jnp.arange(E, dtype=jnp.int32)
        dest = (jnp.sum(jnp.where(onehot, pad_starts, 0), axis=-1) + rank).reshape(-1)
        n_rows = T * TOP_K + E * bm
        n_blocks = n_rows // bm
        n_used = pad_ends[-1] // bm
        blk = jnp.minimum(jnp.arange(n_blocks, dtype=jnp.int32), n_used - 1) * bm
        block_expert = jnp.minimum(jnp.sum(blk[:, None] >= pad_ends[None, :], axis=1),
                                   E - 1).astype(jnp.int32)

        src = _invert(dest, pad_ends, n_rows + bm)

        y4 = _moe_experts(T, block_expert, n_used[None], src, h2, w_gu[l], b_gu[l][:, None],
                          w_down[l], b_down[l][:, None])
        xc = _finish(x1, wt, norm_final_g[None], y4).reshape(B, S, D)
    return xc
```

```python
import functools
import math

import jax
import jax.numpy as jnp
import numpy as np
from jax import lax
from jax.experimental import pallas as pl
from jax.experimental.pallas import tpu as pltpu

F32 = jnp.float32
BF16 = jnp.bfloat16
HIGHEST = lax.Precision.HIGHEST

ATT_HEADS = 8
ATT_HEAD_DIM = 64
ATT_WIDTH = ATT_HEADS * ATT_HEAD_DIM
MOBA_BLOCK = 256
MOBA_TOPK = 3
REL_BUCKETS = 32
REL_MAX_DIST = 128
XATT_HEADS = 4
XATT_HEAD_DIM = 128
N_EXPERTS = 32
TOP_K = 4
SWIGLU_LIMIT = 7.0
SWIGLU_ALPHA = 1.702
EPS = 1e-5
NEG = -1e30
LOG2E = math.log2(math.e)

LANES = 128
SEL_SLOTS = 16
TOKEN_TILE = 512
EXPERT_ROWS = 256
VMEM_LIMIT = 56 * 1024 * 1024


def _rms(x, g):
    return x * lax.rsqrt(jnp.mean(x * x, axis=-1, keepdims=True) + EPS) * g


def _dot(a, b):
    return jnp.dot(a, b, preferred_element_type=F32)


def _dot_nt(a, b):
    return lax.dot_general(a, b, (((1,), (1,)), ((), ())), preferred_element_type=F32)


def _sigmoid(x):
    return 1.0 / (1.0 + jnp.exp(-x))


def _split_bf16(x):
    hi = x.astype(BF16)
    return hi, (x - hi.astype(F32)).astype(BF16)


def _dot3(a, b, dot):
    m = a.shape[0]
    a_hi, a_lo = _split_bf16(a)
    b_hi, b_lo = _split_bf16(b)
    both = dot(jnp.concatenate([a_hi, a_lo], axis=0), b_hi)
    return both[:m] + both[m:] + dot(a_hi, b_lo)


def _mem_kv_kernel(mem_ref, g_ref, wkT_ref, wv_ref, mkT_ref, mv_ref):
    mn = _rms(mem_ref[0], g_ref[...]).astype(BF16)
    mkT_ref[0] = _dot_nt(wkT_ref[...], mn).astype(BF16)
    mv_ref[0] = _dot(mn, wv_ref[...]).astype(BF16)


def _mem_kv(mem, g, wkT, wv):
    B, M, D = mem.shape
    XW = wv.shape[1]
    return pl.pallas_call(
        _mem_kv_kernel,
        grid=(B,),
        in_specs=[pl.BlockSpec((1, M, D), lambda b: (b, 0, 0)),
                  pl.BlockSpec((1, D), lambda b: (0, 0)),
                  pl.BlockSpec((XW, D), lambda b: (0, 0)),
                  pl.BlockSpec((D, XW), lambda b: (0, 0))],
        out_specs=[pl.BlockSpec((1, XW, M), lambda b: (b, 0, 0)),
                   pl.BlockSpec((1, M, XW), lambda b: (b, 0, 0))],
        out_shape=[jax.ShapeDtypeStruct((B, XW, M), BF16),
                   jax.ShapeDtypeStruct((B, M, XW), BF16)],
        name="mem_kv",
    )(mem, g, wkT, wv)


def _qkv_select_kernel(x_ref, g_ref, wk_ref, wqvT_ref, qT_ref, k_ref, vTe_ref, vTo_ref, selbT_ref,
                       km_ref):
    i = pl.program_id(1)
    tq = x_ref.shape[1]
    W = ATT_WIDTH

    @pl.when(i == 0)
    def _():
        km_ref[...] = jnp.zeros_like(km_ref)

    h = _rms(x_ref[0], g_ref[...]).astype(BF16)
    k = _dot(h, wk_ref[...])
    k_ref[0] = k.astype(BF16)
    qvT = _dot_nt(wqvT_ref[...], h)
    qT = qvT[:W] * (1.0 / math.sqrt(ATT_HEAD_DIM))
    qT_ref[0] = (qT * LOG2E).astype(BF16)
    vT = qvT[W:]
    even_head = (lax.broadcasted_iota(jnp.int32, (W, tq), 0) // ATT_HEAD_DIM) % 2 == 0
    vTe_ref[0] = jnp.where(even_head, vT, 1.0).astype(BF16)
    vTo_ref[0] = jnp.where(even_head, 1.0, vT).astype(BF16)

    gate = _dot3(km_ref[...], qT, _dot)
    g3 = gate.reshape(ATT_HEADS, SEL_SLOTS, tq)
    jj = lax.broadcasted_iota(jnp.int32, g3.shape, 1).astype(F32)
    valid = jj < i.astype(F32)
    g3 = jnp.where(valid, g3, -jnp.inf)
    sel = jnp.zeros(g3.shape, F32)
    for _ in range(MOBA_TOPK):
        m = jnp.max(g3, axis=1, keepdims=True)
        first = jnp.min(jnp.where(g3 == m, jj, float(SEL_SLOTS)), axis=1, keepdims=True)
        pick = jj == first
        sel = jnp.where(pick, 1.0, sel)
        g3 = jnp.where(pick, -jnp.inf, g3)
    keep = ((sel > 0.5) & valid) | (jj == i.astype(F32))
    selbT_ref[0] = jnp.where(keep, 0.0, NEG).reshape(ATT_HEADS * SEL_SLOTS, tq).astype(BF16)

    kmean = jnp.sum(k, axis=0, keepdims=True) * (1.0 / tq)
    lane_head = lax.broadcasted_iota(jnp.int32, (1, W), 1) // ATT_HEAD_DIM
    for hh in range(ATT_HEADS):
        km_ref[pl.ds(hh * SEL_SLOTS + i, 1), :] = jnp.where(lane_head == hh, kmean, 0.0)


def _qkv_select(x, g, wk, wqvT):
    B, S, D = x.shape
    W = ATT_WIDTH
    tq = MOBA_BLOCK
    nb = S // tq
    assert nb <= SEL_SLOTS and ATT_HEADS * SEL_SLOTS == LANES
    by_row = pl.BlockSpec((1, tq, W), lambda b, i: (b, i, 0))
    by_col = pl.BlockSpec((1, W, tq), lambda b, i: (b, 0, i))
    return pl.pallas_call(
        _qkv_select_kernel,
        grid=(B, nb),
        in_specs=[pl.BlockSpec((1, tq, D), lambda b, i: (b, i, 0)),
                  pl.BlockSpec((1, D), lambda b, i: (0, 0)),
                  pl.BlockSpec((D, W), lambda b, i: (0, 0)),
                  pl.BlockSpec((2 * W, D), lambda b, i: (0, 0))],
        out_specs=[by_col, by_row, by_col, by_col,
                   pl.BlockSpec((1, LANES, tq), lambda b, i: (b, 0, i))],
        out_shape=[jax.ShapeDtypeStruct((B, W, S), BF16),
                   jax.ShapeDtypeStruct((B, S, W), BF16),
                   jax.ShapeDtypeStruct((B, W, S), BF16),
                   jax.ShapeDtypeStruct((B, W, S), BF16),
                   jax.ShapeDtypeStruct((B, LANES, S), BF16)],
        scratch_shapes=[pltpu.VMEM((LANES, W), F32)],
        compiler_params=pltpu.CompilerParams(
            dimension_semantics=("arbitrary", "arbitrary"), vmem_limit_bytes=VMEM_LIMIT),
        name="qkv_select",
    )(x, g, wk, wqvT)


def _moba_kernel(qT_ref, selbT_ref, k_ref, vTe_ref, vTo_ref, tbT_ref, cfar_ref, eT_ref, o_ref,
                 rhs_ref, m_ref, acc_ref):
    i = pl.program_id(1)
    tq = o_ref.shape[1]
    L = MOBA_BLOCK
    HD = ATT_HEAD_DIM
    H = ATT_HEADS
    selbT = selbT_ref[0].astype(F32)
    row = lax.broadcasted_iota(jnp.int32, (LANES, tq), 0)
    vT_refs = (vTe_ref, vTo_ref)
    slabs = [slice(p * LANES, (p + 1) * LANES) for p in range(H // 2)]

    for h in range(H):
        qTp = qT_ref[0, slabs[h // 2], :].astype(F32)
        own = (row >= HD) if h % 2 else (row < HD)
        rhs_ref[h, :LANES, :] = jnp.where(own, qTp, 0.0).astype(BF16)
        rhs_ref[h, LANES:, :] = jnp.where(row // SEL_SLOTS == h, selbT, 0.0).astype(BF16)
    m_ref[...] = jnp.full(m_ref.shape, -jnp.inf, F32)
    acc_ref[...] = jnp.zeros_like(acc_ref)

    def attend(js, tiles):
        koffs = [pl.multiple_of(j * L, L) for j in js]
        s = []
        for j, koff in zip(js, koffs):
            sj = []
            for p in range(H // 2):
                lhs = jnp.concatenate([k_ref[0, pl.ds(koff, L), slabs[p]], eT_ref[j]], axis=1)
                sj += [_dot(lhs, rhs_ref[2 * p + hh]) for hh in (0, 1)]
            s.append(sj)
        for sj, koff, tile in zip(s, koffs, tiles):
            pr, alpha = [], []
            for h in range(H):
                sh = sj[h] if tile is None else sj[h] + tbT_ref[tile, h]
                cm = jnp.max(jnp.max(sh.reshape(8, L // 8, tq), axis=0), axis=0, keepdims=True)
                m_old = m_ref[h:h + 1, :]
                if tile is None:
                    m_new = jnp.maximum(m_old, cm + cfar_ref[h])
                    pr.append(jnp.exp2(sh - (m_new - cfar_ref[h])).astype(BF16))
                else:
                    m_new = jnp.maximum(m_old, cm)
                    pr.append(jnp.exp2(sh - m_new).astype(BF16))
                m_ref[h:h + 1, :] = m_new
                alpha.append(jnp.exp2(m_old - m_new))
            for h in range(H):
                pv = _dot(vT_refs[h % 2][0, slabs[h // 2], pl.ds(koff, L)], pr[h])
                acc_ref[h] = alpha[h] * acc_ref[h] + pv

    n_far = jnp.maximum(i - 1, 0)

    def far_quad(jj, carry):
        attend([4 * jj + u for u in range(4)], [None] * 4)
        return carry

    lax.fori_loop(0, n_far // 4, far_quad, 0)

    @pl.when(n_far % 4 >= 2)
    def _():
        j0 = n_far // 4 * 4
        attend([j0, j0 + 1], [None, None])

    @pl.when(n_far % 2 == 1)
    def _():
        attend([n_far - 1], [None])

    @pl.when(i >= 1)
    def _():
        attend([i - 1, i], [1, 0])

    @pl.when(i == 0)
    def _():
        attend([i], [0])

    for p in range(H // 2):
        a0, a1 = acc_ref[2 * p], acc_ref[2 * p + 1]
        oT = jnp.concatenate([a0[:HD] / a0[HD:HD + 1], a1[HD:] / a1[0:1]], axis=0)
        o_ref[0, :, slabs[p]] = oT.T.astype(BF16)


def _moba(qT, selbT, k, vT_even, vT_odd, tbT, cfar):
    B, S, W = k.shape
    tq = MOBA_BLOCK
    nb = S // tq
    lane = np.arange(LANES)[None, None, :] % SEL_SLOTS
    eT_all = jnp.asarray(np.broadcast_to(lane == np.arange(SEL_SLOTS)[:, None, None],
                                         (SEL_SLOTS, MOBA_BLOCK, LANES)), BF16)
    whole_T = pl.BlockSpec((1, W, S), lambda b, i: (b, 0, 0))
    return pl.pallas_call(
        _moba_kernel,
        grid=(B, nb),
        in_specs=[pl.BlockSpec((1, W, tq), lambda b, i: (b, 0, i)),
                  pl.BlockSpec((1, LANES, tq), lambda b, i: (b, 0, i)),
                  pl.BlockSpec((1, S, W), lambda b, i: (b, 0, 0)),
                  whole_T, whole_T,
                  pl.BlockSpec(tbT.shape, lambda b, i: (0, 0, 0, 0)),
                  pl.BlockSpec(memory_space=pltpu.SMEM),
                  pl.BlockSpec(eT_all.shape, lambda b, i: (0, 0, 0))],
        out_specs=pl.BlockSpec((1, tq, W), lambda b, i: (b, i, 0)),
        out_shape=jax.ShapeDtypeStruct((B, S, W), BF16),
        scratch_shapes=[pltpu.VMEM((ATT_HEADS, 2 * LANES, tq), BF16),
                        pltpu.VMEM((ATT_HEADS, tq), F32),
                        pltpu.VMEM((ATT_HEADS, LANES, tq), F32)],
        compiler_params=pltpu.CompilerParams(
            dimension_semantics=("arbitrary", "arbitrary"), vmem_limit_bytes=VMEM_LIMIT),
        name="moba",
    )(qT, selbT, k, vT_even, vT_odd, tbT, cfar, eT_all)


def _merge_kernel(tiles_per_seq,
                  x_ref, ya_ref, mkT_ref, mv_ref, g1_ref, wr_ref, cw_ref, bg_ref,
                  wba_ref, wbc_ref, wbx_ref, wo_ref, g2_ref, wrt_ref, br_ref,
                  x1_ref, h2_ref, idx_ref, rank_ref, wt_ref, cnt_ref, zprev_ref):
    i = pl.program_id(0)
    tm, D = x_ref.shape
    CW = cw_ref.shape[1]
    XW = mv_ref.shape[2]

    @pl.when(i == 0)
    def _():
        cnt_ref[...] = jnp.zeros_like(cnt_ref)

    @pl.when(i % tiles_per_seq == 0)
    def _():
        zprev_ref[...] = jnp.zeros_like(zprev_ref)

    x = x_ref[...]
    h = _rms(x, g1_ref[...]).astype(BF16)
    pr = _dot(h, wr_ref[...])

    cb = pr[:, :CW]
    z = pr[:, CW:2 * CW] * pr[:, 2 * CW:3 * CW]
    row = lax.broadcasted_iota(jnp.int32, (tm, CW), 0)
    zp = zprev_ref[...]
    z1 = jnp.where(row == 0, zp[7:8], pltpu.roll(z, 1, 0))
    z2 = jnp.where(row == 0, zp[6:7], jnp.where(row == 1, zp[7:8], pltpu.roll(z, 2, 0)))
    zprev_ref[...] = z[tm - 8:]
    cw = cw_ref[...]
    y_conv = cb * (cw[0:1] * z2 + cw[1:2] * z1 + cw[2:3] * z)

    o0 = 3 * CW
    scale = 1.0 / math.sqrt(XATT_HEAD_DIM)
    ys = []
    for hh in range(XATT_HEADS):
        hs = slice(hh * XATT_HEAD_DIM, (hh + 1) * XATT_HEAD_DIM)
        qx = pr[:, o0 + hh * XATT_HEAD_DIM:o0 + (hh + 1) * XATT_HEAD_DIM].astype(BF16)
        s = _dot(qx, mkT_ref[0, hs, :]) * scale
        e = jnp.exp(s - jnp.max(s, axis=1, keepdims=True))
        l = jnp.sum(e, axis=1, keepdims=True)
        ys.append(_dot(e.astype(BF16), mv_ref[0, :, hs]) / l)
    y_x = jnp.concatenate(ys, axis=1)

    o1 = o0 + XW
    bg = bg_ref[...]
    merged = (_sigmoid(pr[:, o1:o1 + D] + bg[0:1]) * _dot(ya_ref[...], wba_ref[...])
              + _sigmoid(pr[:, o1 + D:o1 + 2 * D] + bg[1:2]) * _dot(y_conv.astype(BF16), wbc_ref[...])
              + _sigmoid(pr[:, o1 + 2 * D:o1 + 3 * D] + bg[2:3]) * _dot(y_x.astype(BF16), wbx_ref[...]))
    x1 = x + _dot(merged.astype(BF16), wo_ref[...])
    x1_ref[...] = x1
    h2 = _rms(x1, g2_ref[...])
    for c in range(D // LANES):
        h2_ref[pl.ds(c, tm, stride=D // LANES), :] = h2[:, c * LANES:(c + 1) * LANES]

    lg = _dot3(wrt_ref[...], h2, _dot_nt) + br_ref[...]
    E = lg.shape[0]
    ee = lax.broadcasted_iota(jnp.int32, (E, tm), 0).astype(F32)
    work = lg
    member = jnp.zeros((E, tm), F32)
    picks, vals = [], []
    for _ in range(TOP_K):
        m = jnp.max(work, axis=0, keepdims=True)
        first = jnp.min(jnp.where(work == m, ee, float(E)), axis=0, keepdims=True)
        pick = ee == first
        work = jnp.where(pick, -jnp.inf, work)
        member = jnp.where(pick, 1.0, member)
        picks.append((pick, first))
        vals.append(m)
    exps = [jnp.exp(v - vals[0]) for v in vals]
    denom = exps[0] + exps[1] + exps[2] + exps[3]

    r_i = lax.broadcasted_iota(jnp.int32, (tm, tm), 0)
    c_i = lax.broadcasted_iota(jnp.int32, (tm, tm), 1)
    upper = jnp.where(r_i < c_i, 1.0, 0.0).astype(BF16)
    before = _dot(member.astype(BF16), upper) + cnt_ref[:, 0:1]
    cnt_ref[...] = cnt_ref[...] + jnp.sum(member, axis=1, keepdims=True)

    wrow = lax.broadcasted_iota(jnp.int32, (LANES, tm), 0)
    wpad = jnp.zeros((LANES, tm), F32)
    for kk in range(TOP_K):
        pick, first = picks[kk]
        idx_ref[kk:kk + 1, :] = first.astype(jnp.int32)
        rank_ref[kk:kk + 1, :] = jnp.sum(jnp.where(pick, before, 0.0), axis=0,
                                         keepdims=True).astype(jnp.int32)
        wpad = jnp.where(wrow == kk, exps[kk] / denom, wpad)
    wt_ref[...] = wpad.T


def _merge(x2, yatt2, mkT, mv, g1, w_rest, conv_w, b_gate, wba, wbc, wbx, wo, g2, wrt, br, S):
    T, D = x2.shape
    tm = TOKEN_TILE
    nt = T // tm
    tps = S // tm
    CW = conv_w.shape[1]
    XW, M = mkT.shape[1], mkT.shape[2]
    E = wrt.shape[0]
    const = lambda shape: pl.BlockSpec(shape, lambda i: (0,) * len(shape))
    return pl.pallas_call(
        functools.partial(_merge_kernel, tps),
        grid=(nt,),
        in_specs=[pl.BlockSpec((tm, D), lambda i: (i, 0)),
                  pl.BlockSpec((tm, yatt2.shape[1]), lambda i: (i, 0)),
                  pl.BlockSpec((1, XW, M), lambda i: (i // tps, 0, 0)),
                  pl.BlockSpec((1, M, XW), lambda i: (i // tps, 0, 0)),
                  const((1, D)), const(w_rest.shape), const(conv_w.shape), const(b_gate.shape),
                  const(wba.shape), const(wbc.shape), const(wbx.shape), const(wo.shape),
                  const((1, D)), const(wrt.shape), const(br.shape)],
        out_specs=[pl.BlockSpec((tm, D), lambda i: (i, 0)),
                   pl.BlockSpec((tm * (D // LANES), LANES), lambda i: (i, 0)),
                   pl.BlockSpec((TOP_K, tm), lambda i: (0, i)),
                   pl.BlockSpec((TOP_K, tm), lambda i: (0, i)),
                   pl.BlockSpec((tm, LANES), lambda i: (i, 0)),
                   pl.BlockSpec((E, LANES), lambda i: (0, 0))],
        out_shape=[jax.ShapeDtypeStruct((T, D), F32),
                   jax.ShapeDtypeStruct((T * (D // LANES), LANES), F32),
                   jax.ShapeDtypeStruct((TOP_K, T), jnp.int32),
                   jax.ShapeDtypeStruct((TOP_K, T), jnp.int32),
                   jax.ShapeDtypeStruct((T, LANES), F32),
                   jax.ShapeDtypeStruct((E, LANES), F32)],
        scratch_shapes=[pltpu.VMEM((8, CW), F32)],
        compiler_params=pltpu.CompilerParams(
            dimension_semantics=("arbitrary",), vmem_limit_bytes=VMEM_LIMIT),
        name="merge_route",
    )(x2, yatt2, mkT, mv, g1, w_rest, conv_w, b_gate, wba, wbc, wbx, wo, g2, wrt, br)


def _invert_kernel(dest_ref, pe_ref, src_ref):
    bm = EXPERT_ROWS
    n_assign = dest_ref.shape[0]

    def put_pad_block(blk, c):
        def put8(r8, c8):
            for u in range(8):
                src_ref[blk * bm + r8 * 8 + u] = n_assign + (blk % 2) * bm + r8 * 8 + u
            return c8

        return lax.fori_loop(0, bm // 8, put8, c)

    assert bm & (bm - 1) == 0
    shift = bm.bit_length() - 1

    def put_last_block(e, c):
        def put8(g, c8):
            r0 = g * 8
            base = n_assign + ((r0 >> shift) & 1) * bm + (r0 & (bm - 1))
            for u in range(8):
                src_ref[r0 + u] = base + u
            return c8

        return lax.fori_loop(jnp.maximum(pe_ref[e] - bm, 0) // 8, pe_ref[e] // 8, put8, c)

    lax.fori_loop(0, N_EXPERTS, put_last_block, 0)

    lax.fori_loop(pe_ref[N_EXPERTS - 1] // bm, src_ref.shape[0] // bm, put_pad_block, 0)

    def put(g, c):
        for u in range(16):
            a = g * 16 + u
            src_ref[dest_ref[a]] = a
        return c

    lax.fori_loop(0, n_assign // 16, put, 0)


def _invert(dest_flat, pad_ends, n_src):
    return pl.pallas_call(
        _invert_kernel,
        grid_spec=pltpu.PrefetchScalarGridSpec(
            num_scalar_prefetch=2,
            grid=(1,),
            in_specs=[],
            out_specs=pl.BlockSpec(memory_space=pltpu.SMEM)),
        out_shape=jax.ShapeDtypeStruct((n_src,), jnp.int32),
        name="invert_routing",
    )(dest_flat, pad_ends)


def _moe_kernel(T, be_ref, nu_ref, src_ref, h2t_ref, wgu_ref, bgu_ref, wd_ref, bd_ref, y4_ref,
                xbuf0, xbuf1, ybuf0, ybuf1, wgu_bf, wd_bf, gsem0, gsem1, ssem0, ssem1):
    b = pl.program_id(0)
    n_used = nu_ref[0]
    bm = EXPERT_ROWS
    F = wd_ref.shape[1]
    CH = wgu_ref.shape[1] // LANES
    xbuf = (xbuf0, xbuf1)
    ybuf = (ybuf0, ybuf1)
    gsem = (gsem0, gsem1)
    ssem = (ssem0, ssem1)

    def gather_start(blk, s):
        for r in range(bm):
            tok = src_ref[blk * bm + r] & (T - 1)
            pltpu.make_async_copy(h2t_ref.at[pl.ds(pl.multiple_of(tok * CH, CH), CH)],
                                  xbuf[s].at[pl.ds(r * CH, CH)], gsem[s]).start()

    def scatter_start(blk, s):
        for r in range(bm):
            a = src_ref[blk * bm + r]
            pltpu.make_async_copy(ybuf[s].at[pl.ds(r * CH, CH)],
                                  y4_ref.at[pl.ds(pl.multiple_of(a * CH, CH), CH)],
                                  ssem[s]).start()

    def gather_wait(s):
        pltpu.make_async_copy(h2t_ref.at[pl.ds(0, bm * CH)], xbuf[s], gsem[s]).wait()

    def scatter_wait(s):
        pltpu.make_async_copy(ybuf[s], y4_ref.at[pl.ds(0, bm * CH)], ssem[s]).wait()

    def load_rows(s):
        x = jnp.concatenate([xbuf[s][pl.ds(c, bm, stride=CH), :] for c in range(CH)], axis=1)
        return x.astype(BF16)

    def ffn(x):
        gu = _dot(x, wgu_bf[...]) + bgu_ref[0]
        g = jnp.minimum(gu[:, :F], SWIGLU_LIMIT)
        lin = jnp.clip(gu[:, F:], -SWIGLU_LIMIT, SWIGLU_LIMIT)
        act = (lin + 1.0) * (g * _sigmoid(SWIGLU_ALPHA * g))
        return _dot(act.astype(BF16), wd_bf[...]) + bd_ref[0]

    def store_rows(s, y):
        for c in range(CH):
            ybuf[s][pl.ds(c, bm, stride=CH), :] = y[:, c * LANES:(c + 1) * LANES]

    @pl.when((b < n_used) & ((b == 0) | (be_ref[b] != be_ref[jnp.maximum(b - 1, 0)])))
    def _():
        wgu_bf[...] = wgu_ref[0].astype(BF16)
        wd_bf[...] = wd_ref[0].astype(BF16)

    @pl.when(b == 0)
    def _():
        gather_start(0, 0)
        gather_wait(0)
        gather_start(1, 1)
        store_rows(0, ffn(load_rows(0)))

    for slot in (0, 1):
        other = 1 - slot

        @pl.when((b > 0) & (b < n_used) & (b % 2 == slot))
        def _():
            gather_start(b + 1, other)
            gather_wait(slot)

            @pl.when(b >= 2)
            def _():
                scatter_wait(slot)

            scatter_start(b - 1, other)
            store_rows(slot, ffn(load_rows(slot)))

        @pl.when((b == n_used - 1) & (b % 2 == slot))
        def _():
            scatter_start(b, slot)
            gather_wait(other)

            @pl.when(b >= 1)
            def _():
                scatter_wait(other)

            scatter_wait(slot)


def _moe_experts(T, block_expert, n_used, src, h2t, wgu, bgu, wd, bd):
    bm = EXPERT_ROWS
    E, D, F2 = wgu.shape
    F = wd.shape[1]
    CH = D // LANES
    n_blocks = block_expert.shape[0]
    assert T & (T - 1) == 0 and src.shape[0] == (n_blocks + 1) * bm
    per_e = lambda b, be, nu, s: (be[b], 0, 0)
    return pl.pallas_call(
        functools.partial(_moe_kernel, T),
        grid_spec=pltpu.PrefetchScalarGridSpec(
            num_scalar_prefetch=3,
            grid=(n_blocks,),
            in_specs=[pl.BlockSpec(memory_space=pl.ANY),
                      pl.BlockSpec((1, D, F2), per_e),
                      pl.BlockSpec((1, 1, F2), per_e),
                      pl.BlockSpec((1, F, D), per_e),
                      pl.BlockSpec((1, 1, D), per_e)],
            out_specs=pl.BlockSpec(memory_space=pl.ANY),
            scratch_shapes=[pltpu.VMEM((bm * CH, LANES), F32)] * 4 + [
                            pltpu.VMEM((D, F2), BF16),
                            pltpu.VMEM((F, D), BF16),
                            pltpu.SemaphoreType.DMA(())] + [pltpu.SemaphoreType.DMA(())] * 3),
        out_shape=jax.ShapeDtypeStruct(((TOP_K * T + 2 * bm) * CH, LANES), F32),
        compiler_params=pltpu.CompilerParams(
            dimension_semantics=("arbitrary",), vmem_limit_bytes=VMEM_LIMIT),
        name="moe_experts",
    )(block_expert, n_used, src, h2t, wgu, bgu, wd, bd)


def _finish_kernel(x1_ref, wt_ref, g_ref, y0_ref, y1_ref, y2_ref, y3_ref, o_ref):
    tm, D = x1_ref.shape
    CH = D // LANES
    wt = wt_ref[...]
    y = x1_ref[...]
    for kk, yk_ref in enumerate((y0_ref, y1_ref, y2_ref, y3_ref)):
        yk = jnp.concatenate([yk_ref[pl.ds(c, tm, stride=CH), :] for c in range(CH)], axis=1)
        y = y + wt[:, kk:kk + 1] * yk
    o_ref[...] = _rms(y, g_ref[...])


def _finish(x1, wt, g, y4):
    T, D = x1.shape
    tm = TOKEN_TILE
    CH = D // LANES
    nt = T // tm
    choice = lambda kk: pl.BlockSpec((tm * CH, LANES), lambda i: (kk * nt + i, 0))
    return pl.pallas_call(
        _finish_kernel,
        grid=(nt,),
        in_specs=[pl.BlockSpec((tm, D), lambda i: (i, 0)),
                  pl.BlockSpec((tm, LANES), lambda i: (i, 0)),
                  pl.BlockSpec((1, D), lambda i: (0, 0)),
                  choice(0), choice(1), choice(2), choice(3)],
        out_specs=pl.BlockSpec((tm, D), lambda i: (i, 0)),
        out_shape=jax.ShapeDtypeStruct((T, D), F32),
        compiler_params=pltpu.CompilerParams(
            dimension_semantics=("arbitrary",), vmem_limit_bytes=VMEM_LIMIT),
        name="finish",
    )(x1, wt, g, y4, y4, y4, y4)


def _t5_bucket(dist):
    n = jnp.maximum(dist, 0)
    max_exact = REL_BUCKETS // 2
    nf = jnp.maximum(n, 1).astype(F32)
    large = max_exact + (jnp.log(nf / max_exact) / math.log(REL_MAX_DIST / max_exact)
                         * (REL_BUCKETS - max_exact)).astype(jnp.int32)
    large = jnp.minimum(large, REL_BUCKETS - 1)
    return jnp.where(n < max_exact, n, large)


def _bias_tiles(rel_bias):
    L = MOBA_BLOCK
    assert REL_MAX_DIST <= L
    d = np.arange(-L, 2 * L + 1)
    onehot = (_t5_bucket(jnp.asarray(d))[:, None] == jnp.arange(REL_BUCKETS)).astype(F32)
    by_dist = jnp.dot(onehot, rel_bias.astype(F32), precision=HIGHEST)
    by_dist = jnp.where((d >= 0)[:, None], by_dist, NEG).T * LOG2E
    H = by_dist.shape[0]
    tiles = []
    for t in range(2):
        v = jnp.concatenate([by_dist[:, (t + 1) * L:(t + 2) * L], by_dist[:, t * L:(t + 1) * L]],
                            axis=1)
        flat = jnp.tile(v, (1, L))[:, :L * (2 * L - 1)]
        tiles.append(flat.reshape(H, L, 2 * L - 1)[:, :, :L])
    return jnp.stack(tiles), by_dist[:, 3 * L]


def kernel(x, mem, rel_bias, norm_mix_g, w_in, b_gate, conv_w, norm_mem_g, w_mem_kv, w_br_att,
           w_br_conv, w_br_xatt, w_out, norm_ffn_g, w_router, b_router, w_gu, b_gu, w_down,
           b_down, norm_final_g):
    B, S, D = x.shape
    T = B * S
    depth = w_in.shape[0]
    assert depth == 1, "the finish step applies the final norm: single-layer configuration only"
    W = ATT_WIDTH
    XW = XATT_HEADS * XATT_HEAD_DIM
    E = w_router.shape[2]
    bm = EXPERT_ROWS
    tb, cfar = _bias_tiles(rel_bias)

    xc = x
    for l in range(depth):
        w_l = w_in[l]
        wk = w_l[:, W:2 * W].astype(BF16)
        wqvT = jnp.concatenate([w_l[:, :W], w_l[:, 2 * W:3 * W]], axis=1).T.astype(BF16)
        w_rest = w_l[:, 3 * W:].astype(BF16)
        wm = w_mem_kv[l]

        mkT, mv = _mem_kv(mem, norm_mem_g[l][None], wm[:, :XW].T.astype(BF16),
                          wm[:, XW:].astype(BF16))
        qT, k, vT_even, vT_odd, selbT = _qkv_select(xc, norm_mix_g[l][None], wk, wqvT)
        y_att = _moba(qT, selbT, k, vT_even, vT_odd, tb, cfar)

        x1, h2, idx, rank, wt, cnt = _merge(
            xc.reshape(T, D), y_att.reshape(T, W), mkT, mv, norm_mix_g[l][None], w_rest,
            conv_w[l], b_gate[l], w_br_att[l].astype(BF16), w_br_conv[l].astype(BF16),
            w_br_xatt[l].astype(BF16), w_out[l].astype(BF16), norm_ffn_g[l][None],
            w_router[l].T, b_router[l][:, None], S)

        counts = cnt[:, 0].astype(jnp.int32)
        padded = (counts + bm - 1) // bm * bm
        pad_ends = jnp.cumsum(padded).astype(jnp.int32)
        pad_starts = pad_ends - padded
        onehot = idx[..., None] == jnp.arange(E, dtype=jnp.int32)
        dest = (jnp.sum(jnp.where(onehot, pad_starts, 0), axis=-1) + rank).reshape(-1)
        n_rows = T * TOP_K + E * bm
        n_blocks = n_rows // bm
        n_used = pad_ends[-1] // bm
        blk = jnp.minimum(jnp.arange(n_blocks, dtype=jnp.int32), n_used - 1) * bm
        block_expert = jnp.minimum(jnp.sum(blk[:, None] >= pad_ends[None, :], axis=1),
                                   E - 1).astype(jnp.int32)

        src = _invert(dest, pad_ends, n_rows + bm)

        y4 = _moe_experts(T, block_expert, n_used[None], src, h2, w_gu[l], b_gu[l][:, None],
                          w_down[l], b_down[l][:, None])
        xc = _finish(x1, wt, norm_final_g[None], y4).reshape(B, S, D)
    return xc
```

```python
import functools
import math

import jax
import jax.numpy as jnp
import numpy as np
from jax import lax
from jax.experimental import pallas as pl
from jax.experimental.pallas import tpu as pltpu

F32 = jnp.float32
BF16 = jnp.bfloat16
HIGHEST = lax.Precision.HIGHEST

ATT_HEADS = 8
ATT_HEAD_DIM = 64
ATT_WIDTH = ATT_HEADS * ATT_HEAD_DIM
MOBA_BLOCK = 256
MOBA_TOPK = 3
REL_BUCKETS = 32
REL_MAX_DIST = 128
XATT_HEADS = 4
XATT_HEAD_DIM = 128
N_EXPERTS = 32
TOP_K = 4
SWIGLU_LIMIT = 7.0
SWIGLU_ALPHA = 1.702
EPS = 1e-5
NEG = -1e30
LOG2E = math.log2(math.e)

LANES = 128
SEL_SLOTS = 16
TOKEN_TILE = 512
EXPERT_ROWS = 256
VMEM_LIMIT = 56 * 1024 * 1024


def _rms(x, g):
    return x * lax.rsqrt(jnp.mean(x * x, axis=-1, keepdims=True) + EPS) * g


def _dot(a, b):
    return jnp.dot(a, b, preferred_element_type=F32)


def _dot_nt(a, b):
    return lax.dot_general(a, b, (((1,), (1,)), ((), ())), preferred_element_type=F32)


def _sigmoid(x):
    return 1.0 / (1.0 + jnp.exp(-x))


def _split_bf16(x):
    hi = x.astype(BF16)
    return hi, (x - hi.astype(F32)).astype(BF16)


def _dot3(a, b, dot):
    m = a.shape[0]
    a_hi, a_lo = _split_bf16(a)
    b_hi, b_lo = _split_bf16(b)
    both = dot(jnp.concatenate([a_hi, a_lo], axis=0), b_hi)
    return both[:m] + both[m:] + dot(a_hi, b_lo)


def _mem_kv_kernel(mem_ref, g_ref, wkT_ref, wv_ref, mkT_ref, mv_ref):
    mn = _rms(mem_ref[0], g_ref[...]).astype(BF16)
    mkT_ref[0] = _dot_nt(wkT_ref[...], mn).astype(BF16)
    mv_ref[0] = _dot(mn, wv_ref[...]).astype(BF16)


def _mem_kv(mem, g, wkT, wv):
    B, M, D = mem.shape
    XW = wv.shape[1]
    return pl.pallas_call(
        _mem_kv_kernel,
        grid=(B,),
        in_specs=[pl.BlockSpec((1, M, D), lambda b: (b, 0, 0)),
                  pl.BlockSpec((1, D), lambda b: (0, 0)),
                  pl.BlockSpec((XW, D), lambda b: (0, 0)),
                  pl.BlockSpec((D, XW), lambda b: (0, 0))],
        out_specs=[pl.BlockSpec((1, XW, M), lambda b: (b, 0, 0)),
                   pl.BlockSpec((1, M, XW), lambda b: (b, 0, 0))],
        out_shape=[jax.ShapeDtypeStruct((B, XW, M), BF16),
                   jax.ShapeDtypeStruct((B, M, XW), BF16)],
        name="mem_kv",
    )(mem, g, wkT, wv)


def _qkv_select_kernel(x_ref, g_ref, wk_ref, wqvT_ref, qT_ref, k_ref, vTe_ref, vTo_ref, selbT_ref,
                       km_ref):
    i = pl.program_id(1)
    tq = x_ref.shape[1]
    W = ATT_WIDTH

    @pl.when(i == 0)
    def _():
        km_ref[...] = jnp.zeros_like(km_ref)

    h = _rms(x_ref[0], g_ref[...]).astype(BF16)
    k = _dot(h, wk_ref[...])
    k_ref[0] = k.astype(BF16)
    qvT = _dot_nt(wqvT_ref[...], h)
    qT = qvT[:W] * (1.0 / math.sqrt(ATT_HEAD_DIM))
    qT_ref[0] = (qT * LOG2E).astype(BF16)
    vT = qvT[W:]
    even_head = (lax.broadcasted_iota(jnp.int32, (W, tq), 0) // ATT_HEAD_DIM) % 2 == 0
    vTe_ref[0] = jnp.where(even_head, vT, 1.0).astype(BF16)
    vTo_ref[0] = jnp.where(even_head, 1.0, vT).astype(BF16)

    gate = _dot3(km_ref[...], qT, _dot)
    g3 = gate.reshape(ATT_HEADS, SEL_SLOTS, tq)
    jj = lax.broadcasted_iota(jnp.int32, g3.shape, 1).astype(F32)
    valid = jj < i.astype(F32)
    g3 = jnp.where(valid, g3, -jnp.inf)
    sel = jnp.zeros(g3.shape, F32)
    for _ in range(MOBA_TOPK):
        m = jnp.max(g3, axis=1, keepdims=True)
        first = jnp.min(jnp.where(g3 == m, jj, float(SEL_SLOTS)), axis=1, keepdims=True)
        pick = jj == first
        sel = jnp.where(pick, 1.0, sel)
        g3 = jnp.where(pick, -jnp.inf, g3)
    keep = ((sel > 0.5) & valid) | (jj == i.astype(F32))
    selbT_ref[0] = jnp.where(keep, 0.0, NEG).reshape(ATT_HEADS * SEL_SLOTS, tq).astype(BF16)

    kmean = jnp.sum(k, axis=0, keepdims=True) * (1.0 / tq)
    lane_head = lax.broadcasted_iota(jnp.int32, (1, W), 1) // ATT_HEAD_DIM
    for hh in range(ATT_HEADS):
        km_ref[pl.ds(hh * SEL_SLOTS + i, 1), :] = jnp.where(lane_head == hh, kmean, 0.0)


def _qkv_select(x, g, wk, wqvT):
    B, S, D = x.shape
    W = ATT_WIDTH
    tq = MOBA_BLOCK
    nb = S // tq
    assert nb <= SEL_SLOTS and ATT_HEADS * SEL_SLOTS == LANES
    by_row = pl.BlockSpec((1, tq, W), lambda b, i: (b, i, 0))
    by_col = pl.BlockSpec((1, W, tq), lambda b, i: (b, 0, i))
    return pl.pallas_call(
        _qkv_select_kernel,
        grid=(B, nb),
        in_specs=[pl.BlockSpec((1, tq, D), lambda b, i: (b, i, 0)),
                  pl.BlockSpec((1, D), lambda b, i: (0, 0)),
                  pl.BlockSpec((D, W), lambda b, i: (0, 0)),
                  pl.BlockSpec((2 * W, D), lambda b, i: (0, 0))],
        out_specs=[by_col, by_row, by_col, by_col,
                   pl.BlockSpec((1, LANES, tq), lambda b, i: (b, 0, i))],
        out_shape=[jax.ShapeDtypeStruct((B, W, S), BF16),
                   jax.ShapeDtypeStruct((B, S, W), BF16),
                   jax.ShapeDtypeStruct((B, W, S), BF16),
                   jax.ShapeDtypeStruct((B, W, S), BF16),
                   jax.ShapeDtypeStruct((B, LANES, S), BF16)],
        scratch_shapes=[pltpu.VMEM((LANES, W), F32)],
        compiler_params=pltpu.CompilerParams(
            dimension_semantics=("arbitrary", "arbitrary"), vmem_limit_bytes=VMEM_LIMIT),
        name="qkv_select",
    )(x, g, wk, wqvT)


def _moba_kernel(qT_ref, selbT_ref, k_ref, vTe_ref, vTo_ref, tbT_ref, cfar_ref, eT_ref, o_ref,
                 rhs_ref, m_ref, acc_ref):
    i = pl.program_id(1)
    tq = o_ref.shape[1]
    L = MOBA_BLOCK
    HD = ATT_HEAD_DIM
    H = ATT_HEADS
    selbT = selbT_ref[0].astype(F32)
    row = lax.broadcasted_iota(jnp.int32, (LANES, tq), 0)
    vT_refs = (vTe_ref, vTo_ref)
    slabs = [slice(p * LANES, (p + 1) * LANES) for p in range(H // 2)]

    for h in range(H):
        qTp = qT_ref[0, slabs[h // 2], :].astype(F32)
        own = (row >= HD) if h % 2 else (row < HD)
        rhs_ref[h, :LANES, :] = jnp.where(own, qTp, 0.0).astype(BF16)
        rhs_ref[h, LANES:, :] = jnp.where(row // SEL_SLOTS == h, selbT, 0.0).astype(BF16)
    m_ref[...] = jnp.full(m_ref.shape, -jnp.inf, F32)
    acc_ref[...] = jnp.zeros_like(acc_ref)

    def attend(js, tiles):
        koffs = [pl.multiple_of(j * L, L) for j in js]
        s = []
        for j, koff in zip(js, koffs):
            sj = []
            for p in range(H // 2):
                lhs = jnp.concatenate([k_ref[0, pl.ds(koff, L), slabs[p]], eT_ref[j]], axis=1)
                sj += [_dot(lhs, rhs_ref[2 * p + hh]) for hh in (0, 1)]
            s.append(sj)
        for sj, koff, tile in zip(s, koffs, tiles):
            pr, alpha = [], []
            for h in range(H):
                sh = sj[h] if tile is None else sj[h] + tbT_ref[tile, h]
                cm = jnp.max(jnp.max(sh.reshape(8, L // 8, tq), axis=0), axis=0, keepdims=True)
                m_old = m_ref[h:h + 1, :]
                if tile is None:
                    m_new = jnp.maximum(m_old, cm + cfar_ref[h])
                    pr.append(jnp.exp2(sh - (m_new - cfar_ref[h])).astype(BF16))
                else:
                    m_new = jnp.maximum(m_old, cm)
                    pr.append(jnp.exp2(sh - m_new).astype(BF16))
                m_ref[h:h + 1, :] = m_new
                alpha.append(jnp.exp2(m_old - m_new))
            for h in range(H):
                pv = _dot(vT_refs[h % 2][0, slabs[h // 2], pl.ds(koff, L)], pr[h])
                acc_ref[h] = alpha[h] * acc_ref[h] + pv

    n_far = jnp.maximum(i - 1, 0)

    def far_quad(jj, carry):
        attend([4 * jj + u for u in range(4)], [None] * 4)
        return carry

    lax.fori_loop(0, n_far // 4, far_quad, 0)

    @pl.when(n_far % 4 >= 2)
    def _():
        j0 = n_far // 4 * 4
        attend([j0, j0 + 1], [None, None])

    @pl.when(n_far % 2 == 1)
    def _():
        attend([n_far - 1], [None])

    @pl.when(i >= 1)
    def _():
        attend([i - 1, i], [1, 0])

    @pl.when(i == 0)
    def _():
        attend([i], [0])

    for p in range(H // 2):
        a0, a1 = acc_ref[2 * p], acc_ref[2 * p + 1]
        oT = jnp.concatenate([a0[:HD] / a0[HD:HD + 1], a1[HD:] / a1[0:1]], axis=0)
        o_ref[0, :, slabs[p]] = oT.T.astype(BF16)


def _moba(qT, selbT, k, vT_even, vT_odd, tbT, cfar):
    B, S, W = k.shape
    tq = MOBA_BLOCK
    nb = S // tq
    lane = np.arange(LANES)[None, None, :] % SEL_SLOTS
    eT_all = jnp.asarray(np.broadcast_to(lane == np.arange(SEL_SLOTS)[:, None, None],
                                         (SEL_SLOTS, MOBA_BLOCK, LANES)), BF16)
    whole_T = pl.BlockSpec((1, W, S), lambda b, i: (b, 0, 0))
    return pl.pallas_call(
        _moba_kernel,
        grid=(B, nb),
        in_specs=[pl.BlockSpec((1, W, tq), lambda b, i: (b, 0, i)),
                  pl.BlockSpec((1, LANES, tq), lambda b, i: (b, 0, i)),
                  pl.BlockSpec((1, S, W), lambda b, i: (b, 0, 0)),
                  whole_T, whole_T,
                  pl.BlockSpec(tbT.shape, lambda b, i: (0, 0, 0, 0)),
                  pl.BlockSpec(memory_space=pltpu.SMEM),
                  pl.BlockSpec(eT_all.shape, lambda b, i: (0, 0, 0))],
        out_specs=pl.BlockSpec((1, tq, W), lambda b, i: (b, i, 0)),
        out_shape=jax.ShapeDtypeStruct((B, S, W), BF16),
        scratch_shapes=[pltpu.VMEM((ATT_HEADS, 2 * LANES, tq), BF16),
                        pltpu.VMEM((ATT_HEADS, tq), F32),
                        pltpu.VMEM((ATT_HEADS, LANES, tq), F32)],
        compiler_params=pltpu.CompilerParams(
            dimension_semantics=("arbitrary", "arbitrary"), vmem_limit_bytes=VMEM_LIMIT),
        name="moba",
    )(qT, selbT, k, vT_even, vT_odd, tbT, cfar, eT_all)


def _merge_kernel(tiles_per_seq,
                  x_ref, ya_ref, mkT_ref, mv_ref, g1_ref, wr_ref, cw_ref, bg_ref,
                  wba_ref, wbc_ref, wbx_ref, wo_ref, g2_ref, wrt_ref, br_ref,
                  x1_ref, h2_ref, idx_ref, rank_ref, wt_ref, cnt_ref, zprev_ref):
    i = pl.program_id(0)
    tm, D = x_ref.shape
    CW = cw_ref.shape[1]
    XW = mv_ref.shape[2]

    @pl.when(i == 0)
    def _():
        cnt_ref[...] = jnp.zeros_like(cnt_ref)

    @pl.when(i % tiles_per_seq == 0)
    def _():
        zprev_ref[...] = jnp.zeros_like(zprev_ref)

    x = x_ref[...]
    h = _rms(x, g1_ref[...]).astype(BF16)
    pr = _dot(h, wr_ref[...])

    cb = pr[:, :CW]
    z = pr[:, CW:2 * CW] * pr[:, 2 * CW:3 * CW]
    row = lax.broadcasted_iota(jnp.int32, (tm, CW), 0)
    zp = zprev_ref[...]
    z1 = jnp.where(row == 0, zp[7:8], pltpu.roll(z, 1, 0))
    z2 = jnp.where(row == 0, zp[6:7], jnp.where(row == 1, zp[7:8], pltpu.roll(z, 2, 0)))
    zprev_ref[...] = z[tm - 8:]
    cw = cw_ref[...]
    y_conv = cb * (cw[0:1] * z2 + cw[1:2] * z1 + cw[2:3] * z)

    o0 = 3 * CW
    scale = 1.0 / math.sqrt(XATT_HEAD_DIM)
    ys = []
    for hh in range(XATT_HEADS):
        hs = slice(hh * XATT_HEAD_DIM, (hh + 1) * XATT_HEAD_DIM)
        qx = pr[:, o0 + hh * XATT_HEAD_DIM:o0 + (hh + 1) * XATT_HEAD_DIM].astype(BF16)
        s = _dot(qx, mkT_ref[0, hs, :]) * scale
        e = jnp.exp(s - jnp.max(s, axis=1, keepdims=True))
        l = jnp.sum(e, axis=1, keepdims=True)
        ys.append(_dot(e.astype(BF16), mv_ref[0, :, hs]) / l)
    y_x = jnp.concatenate(ys, axis=1)

    o1 = o0 + XW
    bg = bg_ref[...]
    merged = (_sigmoid(pr[:, o1:o1 + D] + bg[0:1]) * _dot(ya_ref[...], wba_ref[...])
              + _sigmoid(pr[:, o1 + D:o1 + 2 * D] + bg[1:2]) * _dot(y_conv.astype(BF16), wbc_ref[...])
              + _sigmoid(pr[:, o1 + 2 * D:o1 + 3 * D] + bg[2:3]) * _dot(y_x.astype(BF16), wbx_ref[...]))
    x1 = x + _dot(merged.astype(BF16), wo_ref[...])
    x1_ref[...] = x1
    h2 = _rms(x1, g2_ref[...])
    for c in range(D // LANES):
        h2_ref[pl.ds(c, tm, stride=D // LANES), :] = h2[:, c * LANES:(c + 1) * LANES]

    lg = _dot3(wrt_ref[...], h2, _dot_nt) + br_ref[...]
    E = lg.shape[0]
    ee = lax.broadcasted_iota(jnp.int32, (E, tm), 0).astype(F32)
    work = lg
    member = jnp.zeros((E, tm), F32)
    picks, vals = [], []
    for _ in range(TOP_K):
        m = jnp.max(work, axis=0, keepdims=True)
        first = jnp.min(jnp.where(work == m, ee, float(E)), axis=0, keepdims=True)
        pick = ee == first
        work = jnp.where(pick, -jnp.inf, work)
        member = jnp.where(pick, 1.0, member)
        picks.append((pick, first))
        vals.append(m)
    exps = [jnp.exp(v - vals[0]) for v in vals]
    denom = exps[0] + exps[1] + exps[2] + exps[3]

    r_i = lax.broadcasted_iota(jnp.int32, (tm, tm), 0)
    c_i = lax.broadcasted_iota(jnp.int32, (tm, tm), 1)
    upper = jnp.where(r_i < c_i, 1.0, 0.0).astype(BF16)
    before = _dot(member.astype(BF16), upper) + cnt_ref[:, 0:1]
    cnt_ref[...] = cnt_ref[...] + jnp.sum(member, axis=1, keepdims=True)

    wrow = lax.broadcasted_iota(jnp.int32, (LANES, tm), 0)
    wpad = jnp.zeros((LANES, tm), F32)
    for kk in range(TOP_K):
        pick, first = picks[kk]
        idx_ref[kk:kk + 1, :] = first.astype(jnp.int32)
        rank_ref[kk:kk + 1, :] = jnp.sum(jnp.where(pick, before, 0.0), axis=0,
                                         keepdims=True).astype(jnp.int32)
        wpad = jnp.where(wrow == kk, exps[kk] / denom, wpad)
    wt_ref[...] = wpad.T


def _merge(x2, yatt2, mkT, mv, g1, w_rest, conv_w, b_gate, wba, wbc, wbx, wo, g2, wrt, br, S):
    T, D = x2.shape
    tm = TOKEN_TILE
    nt = T // tm
    tps = S // tm
    CW = conv_w.shape[1]
    XW, M = mkT.shape[1], mkT.shape[2]
    E = wrt.shape[0]
    const = lambda shape: pl.BlockSpec(shape, lambda i: (0,) * len(shape))
    return pl.pallas_call(
        functools.partial(_merge_kernel, tps),
        grid=(nt,),
        in_specs=[pl.BlockSpec((tm, D), lambda i: (i, 0)),
                  pl.BlockSpec((tm, yatt2.shape[1]), lambda i: (i, 0)),
                  pl.BlockSpec((1, XW, M), lambda i: (i // tps, 0, 0)),
                  pl.BlockSpec((1, M, XW), lambda i: (i // tps, 0, 0)),
                  const((1, D)), const(w_rest.shape), const(conv_w.shape), const(b_gate.shape),
                  const(wba.shape), const(wbc.shape), const(wbx.shape), const(wo.shape),
                  const((1, D)), const(wrt.shape), const(br.shape)],
        out_specs=[pl.BlockSpec((tm, D), lambda i: (i, 0)),
                   pl.BlockSpec((tm * (D // LANES), LANES), lambda i: (i, 0)),
                   pl.BlockSpec((TOP_K, tm), lambda i: (0, i)),
                   pl.BlockSpec((TOP_K, tm), lambda i: (0, i)),
                   pl.BlockSpec((tm, LANES), lambda i: (i, 0)),
                   pl.BlockSpec((E, LANES), lambda i: (0, 0))],
        out_shape=[jax.ShapeDtypeStruct((T, D), F32),
                   jax.ShapeDtypeStruct((T * (D // LANES), LANES), F32),
                   jax.ShapeDtypeStruct((TOP_K, T), jnp.int32),
                   jax.ShapeDtypeStruct((TOP_K, T), jnp.int32),
                   jax.ShapeDtypeStruct((T, LANES), F32),
                   jax.ShapeDtypeStruct((E, LANES), F32)],
        scratch_shapes=[pltpu.VMEM((8, CW), F32)],
        compiler_params=pltpu.CompilerParams(
            dimension_semantics=("arbitrary",), vmem_limit_bytes=VMEM_LIMIT),
        name="merge_route",
    )(x2, yatt2, mkT, mv, g1, w_rest, conv_w, b_gate, wba, wbc, wbx, wo, g2, wrt, br)


def _invert_kernel(dest_ref, pe_ref, src_ref):
    bm = EXPERT_ROWS
    n_assign = dest_ref.shape[0]

    def put_pad_block(blk, c):
        def put8(r8, c8):
            for u in range(8):
                src_ref[blk * bm + r8 * 8 + u] = n_assign + (blk % 2) * bm + r8 * 8 + u
            return c8

        return lax.fori_loop(0, bm // 8, put8, c)

    assert bm & (bm - 1) == 0
    shift = bm.bit_length() - 1

    def put_last_block(e, c):
        def put8(g, c8):
            r0 = g * 8
            base = n_assign + ((r0 >> shift) & 1) * bm + (r0 & (bm - 1))
            for u in range(8):
                src_ref[r0 + u] = base + u
            return c8

        return lax.fori_loop(jnp.maximum(pe_ref[e] - bm, 0) // 8, pe_ref[e] // 8, put8, c)

    lax.fori_loop(0, N_EXPERTS, put_last_block, 0)

    lax.fori_loop(pe_ref[N_EXPERTS - 1] // bm, src_ref.shape[0] // bm, put_pad_block, 0)

    def put(g, c):
        for u in range(32):
            a = g * 32 + u
            src_ref[dest_ref[a]] = a
        return c

    lax.fori_loop(0, n_assign // 32, put, 0)


def _invert(dest_flat, pad_ends, n_src):
    return pl.pallas_call(
        _invert_kernel,
        grid_spec=pltpu.PrefetchScalarGridSpec(
            num_scalar_prefetch=2,
            grid=(1,),
            in_specs=[],
            out_specs=pl.BlockSpec(memory_space=pltpu.SMEM)),
        out_shape=jax.ShapeDtypeStruct((n_src,), jnp.int32),
        name="invert_routing",
    )(dest_flat, pad_ends)


def _moe_kernel(T, be_ref, nu_ref, src_ref, h2t_ref, wgu_ref, bgu_ref, wd_ref, bd_ref, y4_ref,
                xbuf0, xbuf1, ybuf0, ybuf1, wgu_bf, wd_bf, gsem0, gsem1, ssem0, ssem1):
    b = pl.program_id(0)
    n_used = nu_ref[0]
    bm = EXPERT_ROWS
    F = wd_ref.shape[1]
    CH = wgu_ref.shape[1] // LANES
    xbuf = (xbuf0, xbuf1)
    ybuf = (ybuf0, ybuf1)
    gsem = (gsem0, gsem1)
    ssem = (ssem0, ssem1)

    def gather_start(blk, s):
        for r in range(bm):
            tok = src_ref[blk * bm + r] & (T - 1)
            pltpu.make_async_copy(h2t_ref.at[pl.ds(pl.multiple_of(tok * CH, CH), CH)],
                                  xbuf[s].at[pl.ds(r * CH, CH)], gsem[s]).start()

    def scatter_start(blk, s):
        for r in range(bm):
            a = src_ref[blk * bm + r]
            pltpu.make_async_copy(ybuf[s].at[pl.ds(r * CH, CH)],
                                  y4_ref.at[pl.ds(pl.multiple_of(a * CH, CH), CH)],
                                  ssem[s]).start()

    def gather_wait(s):
        pltpu.make_async_copy(h2t_ref.at[pl.ds(0, bm * CH)], xbuf[s], gsem[s]).wait()

    def scatter_wait(s):
        pltpu.make_async_copy(ybuf[s], y4_ref.at[pl.ds(0, bm * CH)], ssem[s]).wait()

    def load_rows(s):
        x = jnp.concatenate([xbuf[s][pl.ds(c, bm, stride=CH), :] for c in range(CH)], axis=1)
        return x.astype(BF16)

    def ffn(x):
        gu = _dot(x, wgu_bf[...]) + bgu_ref[0]
        g = jnp.minimum(gu[:, :F], SWIGLU_LIMIT)
        lin = jnp.clip(gu[:, F:], -SWIGLU_LIMIT, SWIGLU_LIMIT)
        act = (lin + 1.0) * (g * _sigmoid(SWIGLU_ALPHA * g))
        return _dot(act.astype(BF16), wd_bf[...]) + bd_ref[0]

    def store_rows(s, y):
        for c in range(CH):
            ybuf[s][pl.ds(c, bm, stride=CH), :] = y[:, c * LANES:(c + 1) * LANES]

    @pl.when((b < n_used) & ((b == 0) | (be_ref[b] != be_ref[jnp.maximum(b - 1, 0)])))
    def _():
        wgu_bf[...] = wgu_ref[0].astype(BF16)
        wd_bf[...] = wd_ref[0].astype(BF16)

    @pl.when(b == 0)
    def _():
        gather_start(0, 0)
        gather_wait(0)
        gather_start(1, 1)
        store_rows(0, ffn(load_rows(0)))

    for slot in (0, 1):
        other = 1 - slot

        @pl.when((b > 0) & (b < n_used) & (b % 2 == slot))
        def _():
            gather_start(b + 1, other)
            gather_wait(slot)

            @pl.when(b >= 2)
            def _():
                scatter_wait(slot)

            scatter_start(b - 1, other)
            store_rows(slot, ffn(load_rows(slot)))

        @pl.when((b == n_used - 1) & (b % 2 == slot))
        def _():
            scatter_start(b, slot)
            gather_wait(other)

            @pl.when(b >= 1)
            def _():
                scatter_wait(other)

            scatter_wait(slot)


def _moe_experts(T, block_expert, n_used, src, h2t, wgu, bgu, wd, bd):
    bm = EXPERT_ROWS
    E, D, F2 = wgu.shape
    F = wd.shape[1]
    CH = D // LANES
    n_blocks = block_expert.shape[0]
    assert T & (T - 1) == 0 and src.shape[0] == (n_blocks + 1) * bm
    per_e = lambda b, be, nu, s: (be[b], 0, 0)
    return pl.pallas_call(
        functools.partial(_moe_kernel, T),
        grid_spec=pltpu.PrefetchScalarGridSpec(
            num_scalar_prefetch=3,
            grid=(n_blocks,),
            in_specs=[pl.BlockSpec(memory_space=pl.ANY),
                      pl.BlockSpec((1, D, F2), per_e),
                      pl.BlockSpec((1, 1, F2), per_e),
                      pl.BlockSpec((1, F, D), per_e),
                      pl.BlockSpec((1, 1, D), per_e)],
            out_specs=pl.BlockSpec(memory_space=pl.ANY),
            scratch_shapes=[pltpu.VMEM((bm * CH, LANES), F32)] * 4 + [
                            pltpu.VMEM((D, F2), BF16),
                            pltpu.VMEM((F, D), BF16),
                            pltpu.SemaphoreType.DMA(())] + [pltpu.SemaphoreType.DMA(())] * 3),
        out_shape=jax.ShapeDtypeStruct(((TOP_K * T + 2 * bm) * CH, LANES), F32),
        compiler_params=pltpu.CompilerParams(
            dimension_semantics=("arbitrary",), vmem_limit_bytes=VMEM_LIMIT),
        name="moe_experts",
    )(block_expert, n_used, src, h2t, wgu, bgu, wd, bd)


def _finish_kernel(x1_ref, wt_ref, g_ref, y0_ref, y1_ref, y2_ref, y3_ref, o_ref):
    tm, D = x1_ref.shape
    CH = D // LANES
    wt = wt_ref[...]
    y = x1_ref[...]
    for kk, yk_ref in enumerate((y0_ref, y1_ref, y2_ref, y3_ref)):
        yk = jnp.concatenate([yk_ref[pl.ds(c, tm, stride=CH), :] for c in range(CH)], axis=1)
        y = y + wt[:, kk:kk + 1] * yk
    o_ref[...] = _rms(y, g_ref[...])


def _finish(x1, wt, g, y4):
    T, D = x1.shape
    tm = TOKEN_TILE
    CH = D // LANES
    nt = T // tm
    choice = lambda kk: pl.BlockSpec((tm * CH, LANES), lambda i: (kk * nt + i, 0))
    return pl.pallas_call(
        _finish_kernel,
        grid=(nt,),
        in_specs=[pl.BlockSpec((tm, D), lambda i: (i, 0)),
                  pl.BlockSpec((tm, LANES), lambda i: (i, 0)),
                  pl.BlockSpec((1, D), lambda i: (0, 0)),
                  choice(0), choice(1), choice(2), choice(3)],
        out_specs=pl.BlockSpec((tm, D), lambda i: (i, 0)),
        out_shape=jax.ShapeDtypeStruct((T, D), F32),
        compiler_params=pltpu.CompilerParams(
            dimension_semantics=("arbitrary",), vmem_limit_bytes=VMEM_LIMIT),
        name="finish",
    )(x1, wt, g, y4, y4, y4, y4)


def _t5_bucket(dist):
    n = jnp.maximum(dist, 0)
    max_exact = REL_BUCKETS // 2
    nf = jnp.maximum(n, 1).astype(F32)
    large = max_exact + (jnp.log(nf / max_exact) / math.log(REL_MAX_DIST / max_exact)
                         * (REL_BUCKETS - max_exact)).astype(jnp.int32)
    large = jnp.minimum(large, REL_BUCKETS - 1)
    return jnp.where(n < max_exact, n, large)


def _bias_tiles(rel_bias):
    L = MOBA_BLOCK
    assert REL_MAX_DIST <= L
    d = np.arange(-L, 2 * L + 1)
    onehot = (_t5_bucket(jnp.asarray(d))[:, None] == jnp.arange(REL_BUCKETS)).astype(F32)
    by_dist = jnp.dot(onehot, rel_bias.astype(F32), precision=HIGHEST)
    by_dist = jnp.where((d >= 0)[:, None], by_dist, NEG).T * LOG2E
    H = by_dist.shape[0]
    tiles = []
    for t in range(2):
        v = jnp.concatenate([by_dist[:, (t + 1) * L:(t + 2) * L], by_dist[:, t * L:(t + 1) * L]],
                            axis=1)
        flat = jnp.tile(v, (1, L))[:, :L * (2 * L - 1)]
        tiles.append(flat.reshape(H, L, 2 * L - 1)[:, :, :L])
    return jnp.stack(tiles), by_dist[:, 3 * L]


def kernel(x, mem, rel_bias, norm_mix_g, w_in, b_gate, conv_w, norm_mem_g, w_mem_kv, w_br_att,
           w_br_conv, w_br_xatt, w_out, norm_ffn_g, w_router, b_router, w_gu, b_gu, w_down,
           b_down, norm_final_g):
    B, S, D = x.shape
    T = B * S
    depth = w_in.shape[0]
    assert depth == 1, "the finish step applies the final norm: single-layer configuration only"
    W = ATT_WIDTH
    XW = XATT_HEADS * XATT_HEAD_DIM
    E = w_router.shape[2]
    bm = EXPERT_ROWS
    tb, cfar = _bias_tiles(rel_bias)

    xc = x
    for l in range(depth):
        w_l = w_in[l]
        wk = w_l[:, W:2 * W].astype(BF16)
        wqvT = jnp.concatenate([w_l[:, :W], w_l[:, 2 * W:3 * W]], axis=1).T.astype(BF16)
        w_rest = w_l[:, 3 * W:].astype(BF16)
        wm = w_mem_kv[l]

        mkT, mv = _mem_kv(mem, norm_mem_g[l][None], wm[:, :XW].T.astype(BF16),
                          wm[:, XW:].astype(BF16))
        qT, k, vT_even, vT_odd, selbT = _qkv_select(xc, norm_mix_g[l][None], wk, wqvT)
        y_att = _moba(qT, selbT, k, vT_even, vT_odd, tb, cfar)

        x1, h2, idx, rank, wt, cnt = _merge(
            xc.reshape(T, D), y_att.reshape(T, W), mkT, mv, norm_mix_g[l][None], w_rest,
            conv_w[l], b_gate[l], w_br_att[l].astype(BF16), w_br_conv[l].astype(BF16),
            w_br_xatt[l].astype(BF16), w_out[l].astype(BF16), norm_ffn_g[l][None],
            w_router[l].T, b_router[l][:, None], S)

        counts = cnt[:, 0].astype(jnp.int32)
        padded = (counts + bm - 1) // bm * bm
        pad_ends = jnp.cumsum(padded).astype(jnp.int32)
        pad_starts = pad_ends - padded
        onehot = idx[..., None] == jnp.arange(E, dtype=jnp.int32)
        dest = (jnp.sum(jnp.where(onehot, pad_starts, 0), axis=-1) + rank).reshape(-1)
        n_rows = T * TOP_K + E * bm
        n_blocks = n_rows // bm
        n_used = pad_ends[-1] // bm
        blk = jnp.minimum(jnp.arange(n_blocks, dtype=jnp.int32), n_used - 1) * bm
        block_expert = jnp.minimum(jnp.sum(blk[:, None] >= pad_ends[None, :], axis=1),
                                   E - 1).astype(jnp.int32)

        src = _invert(dest, pad_ends, n_rows + bm)

        y4 = _moe_experts(T, block_expert, n_used[None], src, h2, w_gu[l], b_gu[l][:, None],
                          w_down[l], b_down[l][:, None])
        xc = _finish(x1, wt, norm_final_g[None], y4).reshape(B, S, D)
    return xc
```
